```python
import jax, jax.numpy as jnp
from jax import lax
import numpy as np

D_MODEL = 1024
BATCH = 2
SEQ = 16384
DEPTH = 1

GLA_HEADS = 4
GLA_DK = D_MODEL // 2
GLA_DV = D_MODEL
GLA_HEAD_K = GLA_DK // GLA_HEADS
GLA_HEAD_V = GLA_DV // GLA_HEADS
GLA_GATE_RANK = 16
GLA_GATE_TEMP = 16.0
GLA_CHUNK = 64
POOL_WIDTH = D_MODEL // 2
POOL_GROUPS = 4
POOL_GROUP_DIM = POOL_WIDTH // POOL_GROUPS
POOL_WINDOWS = (2, 4, 8, 16)
N_BRANCHES = 2
IN_SPLIT_SIZES = (GLA_DK, GLA_DK, GLA_DV, GLA_DV, GLA_GATE_RANK, POOL_WIDTH, D_MODEL, D_MODEL)
IN_COLS = sum(IN_SPLIT_SIZES)
N_EXPERTS = 64
TOP_K = 8
N_GROUPS = 8
TOP_GROUPS = 4
EXPERTS_PER_GROUP = N_EXPERTS // N_GROUPS
EXPERT_FF = 256
SHARED_FF = 256
ROUTE_SCALE = 2.5
MOE_BLOCK = 128
LN_EPS = 1e-5
RMS_EPS = 1e-6
DEEPNORM_ALPHA = (2.0 * DEPTH) ** 0.25
DEEPNORM_BETA = (8.0 * DEPTH) ** -0.25

kernel_name = "gla_pool_gated_moe_deepnorm_block"


def layer_norm(x, g, b):
    xf = x.astype(jnp.float32)
    mu = xf.mean(-1, keepdims=True)
    var = jnp.square(xf - mu).mean(-1, keepdims=True)
    y = (xf - mu) * lax.rsqrt(var + LN_EPS) * g.astype(jnp.float32) + b.astype(jnp.float32)
    return y.astype(x.dtype)


def rms_norm(x, g):
    xf = x.astype(jnp.float32)
    y = xf * lax.rsqrt(jnp.square(xf).mean(-1, keepdims=True) + RMS_EPS)
    return y * g.astype(jnp.float32)


def gla_chunked(q, k, v, log_g):
    B, S, H, dk = q.shape
    dv = v.shape[-1]
    n_chunks = S // GLA_CHUNK

    def to_chunks(t):
        return t.reshape(B, n_chunks, GLA_CHUNK, H, t.shape[-1]).transpose(1, 0, 3, 2, 4)

    causal = jnp.tril(jnp.ones((GLA_CHUNK, GLA_CHUNK), dtype=bool))[:, :, None]

    def step(state, inp):
        qi, ki, vi, gi = inp
        b = jnp.cumsum(gi, axis=-2)
        diff = b[..., :, None, :] - b[..., None, :, :]
        decay = jnp.exp(jnp.where(causal, diff, -jnp.inf))
        scores = jnp.einsum('bhid,bhjd,bhijd->bhij', qi, ki, decay)
        o_intra = jnp.einsum('bhij,bhjv->bhiv', scores, vi)
        o_inter = jnp.einsum('bhid,bhdv->bhiv', qi * jnp.exp(b), state)
        b_last = b[..., -1:, :]
        k_dec = ki * jnp.exp(b_last - b)
        new_state = (jnp.exp(b_last[..., 0, :])[..., None] * state
                     + jnp.einsum('bhjd,bhjv->bhdv', k_dec, vi))
        return new_state, o_intra + o_inter

    state0 = jnp.zeros((B, H, dk, dv), jnp.float32)
    _, o = lax.scan(step, state0, (to_chunks(q), to_chunks(k), to_chunks(v), to_chunks(log_g)))
    return o.transpose(1, 0, 3, 2, 4).reshape(B, S, H, dv)


def causal_window_mean(u, window):
    S = u.shape[1]
    cs = jnp.cumsum(u.astype(jnp.float32), axis=1)
    shifted = jnp.pad(cs, ((0, 0), (window, 0), (0, 0)))[:, :S]
    count = jnp.minimum(jnp.arange(1, S + 1), window).astype(jnp.float32)
    return (cs - shifted) / count[None, :, None]


def hybrid_mixer(x, w_in, w_gate_up, b_gate, gla_norm_g, w_gla_up,
                 w_pool_grp, pool_scale, w_pool_up, w_out):
    B, S, _ = x.shape
    proj = jnp.einsum('bsd,de->bse', x, w_in)
    split_points = [int(p) for p in np.cumsum(IN_SPLIT_SIZES)[:-1]]
    q, k, v, r, g_lr, u, gate_a, gate_b = jnp.split(proj, split_points, axis=-1)

    log_decay = jax.nn.log_sigmoid(
        jnp.einsum('bsr,rk->bsk', g_lr, w_gate_up).astype(jnp.float32)
        + b_gate.astype(jnp.float32)) / GLA_GATE_TEMP

    def heads(t, d):
        return t.reshape(B, S, GLA_HEADS, d).astype(jnp.float32)

    o = gla_chunked(heads(q, GLA_HEAD_K) * (GLA_HEAD_K ** -0.5), heads(k, GLA_HEAD_K),
                    heads(v, GLA_HEAD_V), log_decay.reshape(B, S, GLA_HEADS, GLA_HEAD_K))
    o = rms_norm(o, gla_norm_g).reshape(B, S, GLA_DV).astype(x.dtype)
    o = o * jax.nn.silu(r)
    y_gla = jnp.einsum('bsv,vd->bsd', o, w_gla_up)

    ug = u.reshape(B, S, POOL_GROUPS, POOL_GROUP_DIM)
    pooled = jnp.stack([causal_window_mean(ug[:, :, i], w) for i, w in enumerate(POOL_WINDOWS)],
                       axis=2) - ug.astype(jnp.float32)
    mixed = jnp.einsum('bsgc,gce->bsge', pooled.astype(x.dtype), w_pool_grp)
    mixed = mixed.reshape(B, S, POOL_WIDTH) * pool_scale
    y_pool = jnp.einsum('bsp,pd->bsd', mixed, w_pool_up)

    merged = jax.nn.sigmoid(gate_a) * y_gla + jax.nn.sigmoid(gate_b) * y_pool
    return jnp.einsum('bsd,de->bse', merged, w_out)


def moe_ffn(x, w_router, router_bias, w_exp_gate, w_exp_up, w_exp_down,
            w_sh_gate, w_sh_up, w_sh_down):
    B, S, D = x.shape
    x2 = x.reshape(B * S, D)
    T = x2.shape[0]
    scores = jax.nn.sigmoid(jnp.einsum('td,de->te', x2, w_router).astype(jnp.float32))
    biased = scores + router_bias.astype(jnp.float32)
    grp = biased.reshape(T, N_GROUPS, EXPERTS_PER_GROUP)
    grp_score = lax.top_k(grp, 2)[0].sum(-1)
    _, top_g = lax.top_k(grp_score, TOP_GROUPS)
    gmask = jax.nn.one_hot(top_g, N_GROUPS, dtype=jnp.float32).sum(-2) > 0
    emask = jnp.repeat(gmask, EXPERTS_PER_GROUP, axis=-1)
    _, top_e = lax.top_k(jnp.where(emask, biased, -jnp.inf), TOP_K)
    sel = jnp.take_along_axis(scores, top_e, axis=-1)
    weights = sel / sel.sum(-1, keepdims=True) * ROUTE_SCALE
    gates = (jax.nn.one_hot(top_e, N_EXPERTS, dtype=jnp.float32) * weights[..., None]).sum(-2)
    gates = gates.astype(x.dtype)

    def expert_block(args):
        xb, gb = args
        h = (jax.nn.silu(jnp.einsum('td,edf->tef', xb, w_exp_gate))
             * jnp.einsum('td,edf->tef', xb, w_exp_up))
        return jnp.einsum('tef,efd->td', h * gb[..., None], w_exp_down)

    n_blk = T // MOE_BLOCK
    routed = lax.map(expert_block, (x2.reshape(n_blk, MOE_BLOCK, D),
                                     gates.reshape(n_blk, MOE_BLOCK, N_EXPERTS)))
    routed = routed.reshape(T, D)
    shared = jnp.einsum('tf,fd->td', jax.nn.silu(x2 @ w_sh_gate) * (x2 @ w_sh_up), w_sh_down)
    return (routed + shared).reshape(B, S, D)


def setup_inputs(seed: int = 0) -> dict:
    key = jax.random.key(seed)
    ks = jax.random.split(key, 24)
    f32 = jnp.float32

    def nrm(k, shape, scale):
        return jax.random.normal(k, shape, f32) * scale

    L = DEPTH
    return {
        "x": nrm(ks[0], (BATCH, SEQ, D_MODEL), 1.0),
        "w_in": nrm(ks[1], (L, D_MODEL, IN_COLS), D_MODEL ** -0.5),
        "w_gate_up": nrm(ks[2], (L, GLA_GATE_RANK, GLA_DK), GLA_GATE_RANK ** -0.5),
        "b_gate": nrm(ks[3], (L, GLA_DK), 0.1),
        "gla_norm_g": 1.0 + nrm(ks[4], (L, GLA_HEAD_V), 0.02),
        "w_gla_up": nrm(ks[5], (L, GLA_DV, D_MODEL), GLA_DV ** -0.5),
        "w_pool_grp": nrm(ks[6], (L, POOL_GROUPS, POOL_GROUP_DIM, POOL_GROUP_DIM), POOL_GROUP_DIM ** -0.5),
        "pool_scale": 1.0 + nrm(ks[7], (L, POOL_WIDTH), 0.02),
        "w_pool_up": nrm(ks[8], (L, POOL_WIDTH, D_MODEL), POOL_WIDTH ** -0.5),
        "w_out": nrm(ks[9], (L, D_MODEL, D_MODEL), D_MODEL ** -0.5) * DEEPNORM_BETA,
        "ln1_g": 1.0 + nrm(ks[10], (L, D_MODEL), 0.02),
        "ln1_b": nrm(ks[11], (L, D_MODEL), 0.02),
        "w_router": nrm(ks[12], (L, D_MODEL, N_EXPERTS), D_MODEL ** -0.5),
        "router_bias": nrm(ks[13], (L, N_EXPERTS), 0.01),
        "w_exp_gate": nrm(ks[14], (L, N_EXPERTS, D_MODEL, EXPERT_FF), D_MODEL ** -0.5),
        "w_exp_up": nrm(ks[15], (L, N_EXPERTS, D_MODEL, EXPERT_FF), D_MODEL ** -0.5),
        "w_exp_down": nrm(ks[16], (L, N_EXPERTS, EXPERT_FF, D_MODEL), EXPERT_FF ** -0.5) * DEEPNORM_BETA,
        "w_sh_gate": nrm(ks[17], (L, D_MODEL, SHARED_FF), D_MODEL ** -0.5),
        "w_sh_up": nrm(ks[18], (L, D_MODEL, SHARED_FF), D_MODEL ** -0.5),
        "w_sh_down": nrm(ks[19], (L, SHARED_FF, D_MODEL), SHARED_FF ** -0.5) * DEEPNORM_BETA,
        "ln2_g": 1.0 + nrm(ks[20], (L, D_MODEL), 0.02),
        "ln2_b": nrm(ks[21], (L, D_MODEL), 0.02),
    }


def reference(x, w_in, w_gate_up, b_gate, gla_norm_g, w_gla_up, w_pool_grp, pool_scale,
              w_pool_up, w_out, ln1_g, ln1_b, w_router, router_bias, w_exp_gate, w_exp_up,
              w_exp_down, w_sh_gate, w_sh_up, w_sh_down, ln2_g, ln2_b):
    h = x
    for layer in range(DEPTH):
        mix = hybrid_mixer(h, w_in[layer], w_gate_up[layer], b_gate[layer], gla_norm_g[layer],
                           w_gla_up[layer], w_pool_grp[layer], pool_scale[layer],
                           w_pool_up[layer], w_out[layer])
        h = layer_norm(DEEPNORM_ALPHA * h + mix, ln1_g[layer], ln1_b[layer])
        ffn = moe_ffn(h, w_router[layer], router_bias[layer], w_exp_gate[layer], w_exp_up[layer],
                      w_exp_down[layer], w_sh_gate[layer], w_sh_up[layer], w_sh_down[layer])
        h = layer_norm(DEEPNORM_ALPHA * h + ffn, ln2_g[layer], ln2_b[layer])
    return h
```

```python
import functools

import jax
import jax.numpy as jnp
from jax import lax
from jax.experimental import pallas as pl
from jax.experimental.pallas import tpu as pltpu

F32 = jnp.float32
BF16 = jnp.bfloat16

D_MODEL = 1024
GLA_HEADS = 4
GLA_DK = D_MODEL // 2
GLA_DV = D_MODEL
HEAD_K = GLA_DK // GLA_HEADS
HEAD_V = GLA_DV // GLA_HEADS
GATE_RANK = 16
GATE_TEMP = 16.0
POOL_WIDTH = D_MODEL // 2
POOL_GROUPS = 4
POOL_GROUP_DIM = POOL_WIDTH // POOL_GROUPS
POOL_WINDOWS = (2, 4, 8, 16)
POOL_HALO = 16
N_EXPERTS = 64
TOP_K = 8
N_GROUPS = 8
TOP_GROUPS = 4
EXPERTS_PER_GROUP = N_EXPERTS // N_GROUPS
EXPERT_FF = 256
SHARED_FF = 256
ROUTE_SCALE = 2.5
LN_EPS = 1e-5
RMS_EPS = 1e-6
LANES = 128

MIX_TILE = 256
ROUTER_TILE = 512
MOE_TILE = 1024
VMEM_LIMIT = 56 * 1024 * 1024

_dot = functools.partial(jnp.dot, preferred_element_type=F32)
_NT = (((1,), (1,)), ((), ()))
_TN = (((0,), (0,)), ((), ()))


def _layer_norm(y, g, b):
    mu = jnp.mean(y, axis=-1, keepdims=True)
    yc = y - mu
    var = jnp.mean(yc * yc, axis=-1, keepdims=True)
    return yc * lax.rsqrt(var + LN_EPS) * g + b


def _split3(a):
    hi = a.astype(BF16)
    r1 = a - hi.astype(F32)
    mid = r1.astype(BF16)
    lo = (r1 - mid.astype(F32)).astype(BF16)
    return hi, mid, lo


def _mixer_kernel(x_ref, wqk_ref, wv_ref, wr_ref, wglr_ref, wgup_ref, bgate_ref, wu_ref,
                  wga_ref, wgb_ref, gnorm_ref, wglaup_ref, wpgrp_ref, pscale_ref, wpup_ref,
                  wout_ref, lng_ref, lnb_ref, out_ref, state_ref, ucat_ref, *, alpha, tm):
    j = pl.program_id(1)

    @pl.when(j == 0)
    def _():
        state_ref[...] = jnp.zeros_like(state_ref)
        ucat_ref[0:POOL_HALO, :] = jnp.zeros((POOL_HALO, POOL_WIDTH), F32)

    x = x_ref[0]
    xb = x.astype(BF16)

    qk = _dot(xb, wqk_ref[...])
    v = _dot(xb, wv_ref[...])
    g_lr = _dot(xb, wglr_ref[...])
    z = _dot(g_lr.astype(BF16), wgup_ref[...]) + bgate_ref[...]
    log_decay = (jnp.minimum(z, 0.0) - jnp.log1p(jnp.exp(-jnp.abs(z)))) * (1.0 / GATE_TEMP)

    row = lax.broadcasted_iota(jnp.int32, (tm, tm), 0)
    col = lax.broadcasted_iota(jnp.int32, (tm, tm), 1)
    causal = row >= col
    tri = causal.astype(BF16)
    g_hi, g_mid, g_lo = _split3(log_decay)
    bcum = _dot(tri, g_hi) + _dot(tri, g_mid) + _dot(tri, g_lo)

    gnorm = gnorm_ref[...]
    heads = []
    for h in range(GLA_HEADS):
        ks = slice(h * HEAD_K, (h + 1) * HEAD_K)
        b = bcum[:, ks]
        b_last = b[tm - 1:tm, :]
        q_h = qk[:, ks] * (HEAD_K ** -0.5)
        k_h = qk[:, GLA_DK + h * HEAD_K:GLA_DK + (h + 1) * HEAD_K]
        v_h = v[:, h * HEAD_V:(h + 1) * HEAD_V].astype(BF16)
        q_in = (q_h * jnp.exp(b)).astype(BF16)
        k_out = (k_h * jnp.exp(-b)).astype(BF16)
        k_end = (k_h * jnp.exp(b_last - b)).astype(BF16)
        scores = lax.dot_general(q_in, k_out, _NT, preferred_element_type=F32)
        scores = jnp.where(causal, scores, 0.0).astype(BF16)
        st = state_ref[h]
        o = _dot(scores, v_h) + lax.dot_general(q_in, st.astype(BF16), _NT,
                                                preferred_element_type=F32)
        state_ref[h] = st * jnp.exp(b_last) + lax.dot_general(
            v_h, k_end, _TN, preferred_element_type=F32)
        ms = jnp.mean(o * o, axis=-1, keepdims=True)
        heads.append(o * lax.rsqrt(ms + RMS_EPS) * gnorm)
    o = jnp.concatenate(heads, axis=1)
    r = _dot(xb, wr_ref[...])
    o = o * (r * jax.nn.sigmoid(r))
    y_gla = _dot(o.astype(BF16), wglaup_ref[...])

    u = _dot(xb, wu_ref[...])
    ucat_ref[POOL_HALO:, :] = u
    pos = j * tm - POOL_HALO + lax.broadcasted_iota(jnp.int32, (tm + POOL_HALO, 1), 0)
    mixed = []
    for gi, window in enumerate(POOL_WINDOWS):
        a = ucat_ref[:, gi * POOL_GROUP_DIM:(gi + 1) * POOL_GROUP_DIM]
        s = a
        step = 1
        while step < window:
            s = s + pltpu.roll(s, step, 0)
            step *= 2
        count = jnp.clip(pos + 1, 1, window).astype(F32)
        pooled = (s / count - a)[POOL_HALO:, :]
        mixed.append(_dot(pooled.astype(BF16), wpgrp_ref[gi]))
    ucat_ref[0:POOL_HALO, :] = u[tm - POOL_HALO:, :]
    mixed = jnp.concatenate(mixed, axis=1) * pscale_ref[...]
    y_pool = _dot(mixed.astype(BF16), wpup_ref[...])

    gate_a = _dot(xb, wga_ref[...])
    gate_b = _dot(xb, wgb_ref[...])
    merged = jax.nn.sigmoid(gate_a) * y_gla + jax.nn.sigmoid(gate_b) * y_pool
    mix = _dot(merged.astype(BF16), wout_ref[...])
    out_ref[0] = _layer_norm(alpha * x + mix, lng_ref[...], lnb_ref[...])


def _const_spec(shape):
    zeros = (0,) * len(shape)
    return pl.BlockSpec(shape, lambda *_: zeros)


def _mixer(x, w_in, w_gate_up, b_gate, gla_norm_g, w_gla_up, w_pool_grp, pool_scale,
           w_pool_up, w_out, ln_g, ln_b, alpha):
    B, S, D = x.shape
    tm = MIX_TILE
    assert D == D_MODEL and S % tm == 0
    c = 0
    w_qk = w_in[:, c:c + 2 * GLA_DK].astype(BF16); c += 2 * GLA_DK
    w_v = w_in[:, c:c + GLA_DV].astype(BF16); c += GLA_DV
    w_r = w_in[:, c:c + GLA_DV].astype(BF16); c += GLA_DV
    w_glr = jnp.pad(w_in[:, c:c + GATE_RANK], ((0, 0), (0, LANES - GATE_RANK))).astype(BF16); c += GATE_RANK
    w_u = w_in[:, c:c + POOL_WIDTH].astype(BF16); c += POOL_WIDTH
    w_ga = w_in[:, c:c + D].astype(BF16); c += D
    w_gb = w_in[:, c:c + D].astype(BF16); c += D
    assert c == w_in.shape[1]
    w_gup = jnp.pad(w_gate_up, ((0, LANES - GATE_RANK), (0, 0))).astype(BF16)
    operands = (
        w_qk, w_v, w_r, w_glr, w_gup, b_gate.reshape(1, GLA_DK), w_u, w_ga, w_gb,
        gla_norm_g.reshape(1, HEAD_V), w_gla_up.astype(BF16), w_pool_grp.astype(BF16),
        pool_scale.reshape(1, POOL_WIDTH), w_pool_up.astype(BF16), w_out.astype(BF16),
        ln_g.reshape(1, D), ln_b.reshape(1, D))
    return pl.pallas_call(
        functools.partial(_mixer_kernel, alpha=alpha, tm=tm),
        grid=(B, S // tm),
        in_specs=[pl.BlockSpec((1, tm, D), lambda b, j: (b, j, 0))]
        + [_const_spec(w.shape) for w in operands],
        out_specs=pl.BlockSpec((1, tm, D), lambda b, j: (b, j, 0)),
        out_shape=jax.ShapeDtypeStruct((B, S, D), F32),
        scratch_shapes=[
            pltpu.VMEM((GLA_HEADS, HEAD_V, HEAD_K), F32),
            pltpu.VMEM((tm + POOL_HALO, POOL_WIDTH), F32),
        ],
        compiler_params=pltpu.CompilerParams(
            dimension_semantics=("arbitrary", "arbitrary"), vmem_limit_bytes=VMEM_LIMIT),
    )(x, *operands)


def _rank_desc(vals, n):
    idx = lax.broadcasted_iota(jnp.int32, vals.shape, 0)
    rank = jnp.zeros(vals.shape, jnp.int32)
    for other in range(n):
        o = vals[other:other + 1, :]
        before = (o > vals) | ((o == vals) & (other < idx))
        rank = rank + before.astype(jnp.int32)
    return rank


def _router_kernel(h_ref, wrt_ref, bias_ref, gates_ref):
    h = h_ref[...]
    tr = h.shape[0]
    logits = lax.dot_general(wrt_ref[...], h, _NT, preferred_element_type=F32,
                             precision=lax.Precision.HIGHEST)
    scores = jax.nn.sigmoid(logits)
    biased = scores + bias_ref[...]
    grp = biased.reshape(N_GROUPS, EXPERTS_PER_GROUP, tr)
    slot = lax.broadcasted_iota(jnp.int32, grp.shape, 1)
    top1 = jnp.max(grp, axis=1, keepdims=True)
    first = jnp.min(jnp.where(grp == top1, slot, EXPERTS_PER_GROUP), axis=1, keepdims=True)
    top2 = jnp.max(jnp.where(slot == first, -jnp.inf, grp), axis=1)
    grp_score = top1[:, 0, :] + top2
    grp_on = _rank_desc(grp_score, N_GROUPS) < TOP_GROUPS
    emask = jnp.broadcast_to(grp_on[:, None, :], grp.shape).reshape(N_EXPERTS, tr)
    masked = jnp.where(emask, biased, -jnp.inf)
    chosen = (_rank_desc(masked, N_EXPERTS) < TOP_K) & emask
    sel = jnp.where(chosen, scores, 0.0)
    gates = sel / jnp.sum(sel, axis=0, keepdims=True) * ROUTE_SCALE
    gates_ref[...] = gates.T


def _router(h2, w_router, router_bias):
    T, D = h2.shape
    tr = ROUTER_TILE
    assert T % tr == 0
    return pl.pallas_call(
        _router_kernel,
        grid=(T // tr,),
        in_specs=[pl.BlockSpec((tr, D), lambda i: (i, 0)),
                  _const_spec((N_EXPERTS, D)), _const_spec((N_EXPERTS, 1))],
        out_specs=pl.BlockSpec((tr, N_EXPERTS), lambda i: (i, 0)),
        out_shape=jax.ShapeDtypeStruct((T, N_EXPERTS), F32),
        compiler_params=pltpu.CompilerParams(
            dimension_semantics=("arbitrary",), vmem_limit_bytes=VMEM_LIMIT),
    )(h2, w_router.T, router_bias.reshape(N_EXPERTS, 1))


def _moe_kernel(h_ref, gates_ref, wgu_ref, wd_ref, wsgu_ref, wsd_ref, lng_ref, lnb_ref,
                out_ref, hb_ref, acc_ref, *, alpha):
    e = pl.program_id(1)

    @pl.when(e == 0)
    def _():
        hb = h_ref[...].astype(BF16)
        hb_ref[...] = hb
        a = _dot(hb, wsgu_ref[...])
        act = a[:, :SHARED_FF]
        hid = act * jax.nn.sigmoid(act) * a[:, SHARED_FF:]
        acc_ref[...] = _dot(hid.astype(BF16), wsd_ref[...])

    gates = gates_ref[...]
    lane = lax.broadcasted_iota(jnp.int32, gates.shape, 1)
    gate = jnp.sum(jnp.where(lane == e, gates, 0.0), axis=1, keepdims=True)
    a = _dot(hb_ref[...], wgu_ref[0])
    act = a[:, :EXPERT_FF]
    hid = act * jax.nn.sigmoid(act) * a[:, EXPERT_FF:] * gate
    acc_ref[...] += _dot(hid.astype(BF16), wd_ref[0])

    @pl.when(e == N_EXPERTS - 1)
    def _():
        out_ref[...] = _layer_norm(alpha * h_ref[...] + acc_ref[...], lng_ref[...], lnb_ref[...])


def _moe(h2, gates, w_exp_gate, w_exp_up, w_exp_down, w_sh_gate, w_sh_up, w_sh_down,
         ln_g, ln_b, alpha):
    T, D = h2.shape
    tm = MOE_TILE
    assert T % tm == 0
    w_gu = jnp.concatenate([w_exp_gate, w_exp_up], axis=-1).astype(BF16)
    w_d = w_exp_down.astype(BF16)
    w_sgu = jnp.concatenate([w_sh_gate, w_sh_up], axis=-1).astype(BF16)
    w_sd = w_sh_down.astype(BF16)
    return pl.pallas_call(
        functools.partial(_moe_kernel, alpha=alpha),
        grid=(T // tm, N_EXPERTS),
        in_specs=[pl.BlockSpec((tm, D), lambda i, e: (i, 0)),
                  pl.BlockSpec((tm, N_EXPERTS), lambda i, e: (i, 0)),
                  pl.BlockSpec((1, D, 2 * EXPERT_FF), lambda i, e: (e, 0, 0)),
                  pl.BlockSpec((1, EXPERT_FF, D), lambda i, e: (e, 0, 0)),
                  _const_spec((D, 2 * SHARED_FF)), _const_spec((SHARED_FF, D)),
                  _const_spec((1, D)), _const_spec((1, D))],
        out_specs=pl.BlockSpec((tm, D), lambda i, e: (i, 0)),
        out_shape=jax.ShapeDtypeStruct((T, D), F32),
        scratch_shapes=[pltpu.VMEM((tm, D), BF16), pltpu.VMEM((tm, D), F32)],
        compiler_params=pltpu.CompilerParams(
            dimension_semantics=("arbitrary", "arbitrary"), vmem_limit_bytes=VMEM_LIMIT),
    )(h2, gates, w_gu, w_d, w_sgu, w_sd, ln_g.reshape(1, D), ln_b.reshape(1, D))


def kernel(x, w_in, w_gate_up, b_gate, gla_norm_g, w_gla_up, w_pool_grp, pool_scale, w_pool_up,
           w_out, ln1_g, ln1_b, w_router, router_bias, w_exp_gate, w_exp_up, w_exp_down,
           w_sh_gate, w_sh_up, w_sh_down, ln2_g, ln2_b):
    B, S, D = x.shape
    depth = w_in.shape[0]
    alpha = (2.0 * depth) ** 0.25
    h = x
    for l in range(depth):
        h = _mixer(h, w_in[l], w_gate_up[l], b_gate[l], gla_norm_g[l], w_gla_up[l],
                   w_pool_grp[l], pool_scale[l], w_pool_up[l], w_out[l], ln1_g[l], ln1_b[l], alpha)
        h2 = h.reshape(B * S, D)
        gates = _router(h2, w_router[l], router_bias[l])
        h2 = _moe(h2, gates, w_exp_gate[l], w_exp_up[l], w_exp_down[l], w_sh_gate[l],
                  w_sh_up[l], w_sh_down[l], ln2_g[l], ln2_b[l], alpha)
        h = h2.reshape(B, S, D)
    return h
```

```python
import functools

import jax
import jax.numpy as jnp
from jax import lax
from jax.experimental import pallas as pl
from jax.experimental.pallas import tpu as pltpu
from jax.experimental.pallas import tpu_sc as plsc

F32 = jnp.float32
BF16 = jnp.bfloat16
I32 = jnp.int32
U32 = jnp.uint32

D_MODEL = 1024
GLA_HEADS = 4
GLA_DK = D_MODEL // 2
GLA_DV = D_MODEL
HEAD_K = GLA_DK // GLA_HEADS
HEAD_V = GLA_DV // GLA_HEADS
GATE_RANK = 16
GATE_TEMP = 16.0
POOL_WIDTH = D_MODEL // 2
POOL_GROUPS = 4
POOL_GROUP_DIM = POOL_WIDTH // POOL_GROUPS
POOL_WINDOWS = (2, 4, 8, 16)
POOL_HALO = 16
N_EXPERTS = 64
TOP_K = 8
N_GROUPS = 8
TOP_GROUPS = 4
EXPERTS_PER_GROUP = N_EXPERTS // N_GROUPS
EXPERT_FF = 256
SHARED_FF = 256
ROUTE_SCALE = 2.5
LN_EPS = 1e-5
RMS_EPS = 1e-6
LANES = 128

MIX_TILE = 256
ROUTER_TILE = 512
FINAL_TILE = 512
EXPERT_ROWS = 512
SC_CORES = 2
SC_SUBCORES = 16
SC_CHUNK = 64
HALF = D_MODEL // 2
VMEM_LIMIT = 56 * 1024 * 1024

_dot = functools.partial(jnp.dot, preferred_element_type=F32)
_NT = (((1,), (1,)), ((), ()))
_TN = (((0,), (0,)), ((), ()))


def _layer_norm(y, g, b):
    mu = jnp.mean(y, axis=-1, keepdims=True)
    yc = y - mu
    var = jnp.mean(yc * yc, axis=-1, keepdims=True)
    return yc * lax.rsqrt(var + LN_EPS) * g + b


def _split3(a):
    hi = a.astype(BF16)
    r1 = a - hi.astype(F32)
    mid = r1.astype(BF16)
    lo = (r1 - mid.astype(F32)).astype(BF16)
    return hi, mid, lo


def _pack_rows(y):
    lo = lax.bitcast_convert_type(y[:, :HALF].astype(BF16).astype(F32), U32)
    hi = lax.bitcast_convert_type(y[:, HALF:].astype(BF16).astype(F32), U32)
    return lax.bitcast_convert_type(hi | (lo >> 16), I32)


def _unpack_rows(w):
    u = lax.bitcast_convert_type(w, U32)
    lo = lax.bitcast_convert_type(u << 16, F32)
    hi = lax.bitcast_convert_type(u & jnp.uint32(0xFFFF0000), F32)
    return lo, hi


def _mixer_kernel(x_ref, wqk_ref, wv_ref, wr_ref, wglr_ref, wgup_ref, bgate_ref, wu_ref,
                  wga_ref, wgb_ref, gnorm_ref, wglaup_ref, wpgrp_ref, pscale_ref, wpup_ref,
                  wout_ref, lng_ref, lnb_ref, out_ref, words_ref, state_ref, ucat_ref,
                  *, alpha, tm):
    j = pl.program_id(1)

    @pl.when(j == 0)
    def _():
        state_ref[...] = jnp.zeros_like(state_ref)
        ucat_ref[0:POOL_HALO, :] = jnp.zeros((POOL_HALO, POOL_WIDTH), F32)

    x = x_ref[0]
    xb = x.astype(BF16)

    qk = _dot(xb, wqk_ref[...])
    v = _dot(xb, wv_ref[...])
    g_lr = _dot(xb, wglr_ref[...])
    z = _dot(g_lr.astype(BF16), wgup_ref[...]) + bgate_ref[...]
    log_decay = (jnp.minimum(z, 0.0) - jnp.log1p(jnp.exp(-jnp.abs(z)))) * (1.0 / GATE_TEMP)

    row = lax.broadcasted_iota(I32, (tm, tm), 0)
    col = lax.broadcasted_iota(I32, (tm, tm), 1)
    causal = row >= col
    tri = causal.astype(BF16)
    g_hi, g_mid, g_lo = _split3(log_decay)
    bcum = _dot(tri, g_hi) + _dot(tri, g_mid) + _dot(tri, g_lo)

    gnorm = gnorm_ref[...]
    heads = []
    for h in range(GLA_HEADS):
        ks = slice(h * HEAD_K, (h + 1) * HEAD_K)
        b = bcum[:, ks]
        b_last = b[tm - 1:tm, :]
        q_h = qk[:, ks] * (HEAD_K ** -0.5)
        k_h = qk[:, GLA_DK + h * HEAD_K:GLA_DK + (h + 1) * HEAD_K]
        v_h = v[:, h * HEAD_V:(h + 1) * HEAD_V].astype(BF16)
        q_in = (q_h * jnp.exp(b)).astype(BF16)
        k_out = (k_h * jnp.exp(-b)).astype(BF16)
        k_end = (k_h * jnp.exp(b_last - b)).astype(BF16)
        scores = lax.dot_general(q_in, k_out, _NT, preferred_element_type=F32)
        scores = jnp.where(causal, scores, 0.0).astype(BF16)
        st = state_ref[h]
        o = _dot(scores, v_h) + lax.dot_general(q_in, st.astype(BF16), _NT,
                                                preferred_element_type=F32)
        state_ref[h] = st * jnp.exp(b_last) + lax.dot_general(
            v_h, k_end, _TN, preferred_element_type=F32)
        ms = jnp.mean(o * o, axis=-1, keepdims=True)
        heads.append(o * lax.rsqrt(ms + RMS_EPS) * gnorm)
    o = jnp.concatenate(heads, axis=1)
    r = _dot(xb, wr_ref[...])
    o = o * (r * jax.nn.sigmoid(r))
    y_gla = _dot(o.astype(BF16), wglaup_ref[...])

    u = _dot(xb, wu_ref[...])
    ucat_ref[POOL_HALO:, :] = u
    pos = j * tm - POOL_HALO + lax.broadcasted_iota(I32, (tm + POOL_HALO, 1), 0)
    mixed = []
    for gi, window in enumerate(POOL_WINDOWS):
        a = ucat_ref[:, gi * POOL_GROUP_DIM:(gi + 1) * POOL_GROUP_DIM]
        s = a
        step = 1
        while step < window:
            s = s + pltpu.roll(s, step, 0)
            step *= 2
        count = jnp.clip(pos + 1, 1, window).astype(F32)
        pooled = (s / count - a)[POOL_HALO:, :]
        mixed.append(_dot(pooled.astype(BF16), wpgrp_ref[gi]))
    ucat_ref[0:POOL_HALO, :] = u[tm - POOL_HALO:, :]
    mixed = jnp.concatenate(mixed, axis=1) * pscale_ref[...]
    y_pool = _dot(mixed.astype(BF16), wpup_ref[...])

    gate_a = _dot(xb, wga_ref[...])
    gate_b = _dot(xb, wgb_ref[...])
    merged = jax.nn.sigmoid(gate_a) * y_gla + jax.nn.sigmoid(gate_b) * y_pool
    mix = _dot(merged.astype(BF16), wout_ref[...])
    y = _layer_norm(alpha * x + mix, lng_ref[...], lnb_ref[...])
    out_ref[0] = y
    words_ref[0] = _pack_rows(y)


def _const_spec(shape):
    zeros = (0,) * len(shape)
    return pl.BlockSpec(shape, lambda *_: zeros)


def _mixer(x, w_in, w_gate_up, b_gate, gla_norm_g, w_gla_up, w_pool_grp, pool_scale,
           w_pool_up, w_out, ln_g, ln_b, alpha):
    B, S, D = x.shape
    tm = MIX_TILE
    assert D == D_MODEL and S % tm == 0
    c = 0
    w_qk = w_in[:, c:c + 2 * GLA_DK].astype(BF16); c += 2 * GLA_DK
    w_v = w_in[:, c:c + GLA_DV].astype(BF16); c += GLA_DV
    w_r = w_in[:, c:c + GLA_DV].astype(BF16); c += GLA_DV
    w_glr = jnp.pad(w_in[:, c:c + GATE_RANK], ((0, 0), (0, LANES - GATE_RANK))).astype(BF16); c += GATE_RANK
    w_u = w_in[:, c:c + POOL_WIDTH].astype(BF16); c += POOL_WIDTH
    w_ga = w_in[:, c:c + D].astype(BF16); c += D
    w_gb = w_in[:, c:c + D].astype(BF16); c += D
    assert c == w_in.shape[1]
    w_gup = jnp.pad(w_gate_up, ((0, LANES - GATE_RANK), (0, 0))).astype(BF16)
    operands = (
        w_qk, w_v, w_r, w_glr, w_gup, b_gate.reshape(1, GLA_DK), w_u, w_ga, w_gb,
        gla_norm_g.reshape(1, HEAD_V), w_gla_up.astype(BF16), w_pool_grp.astype(BF16),
        pool_scale.reshape(1, POOL_WIDTH), w_pool_up.astype(BF16), w_out.astype(BF16),
        ln_g.reshape(1, D), ln_b.reshape(1, D))
    return pl.pallas_call(
        functools.partial(_mixer_kernel, alpha=alpha, tm=tm),
        name="mixer",
        grid=(B, S // tm),
        in_specs=[pl.BlockSpec((1, tm, D), lambda b, j: (b, j, 0))]
        + [_const_spec(w.shape) for w in operands],
        out_specs=[pl.BlockSpec((1, tm, D), lambda b, j: (b, j, 0)),
                   pl.BlockSpec((1, tm, HALF), lambda b, j: (b, j, 0))],
        out_shape=[jax.ShapeDtypeStruct((B, S, D), F32),
                   jax.ShapeDtypeStruct((B, S, HALF), I32)],
        scratch_shapes=[
            pltpu.VMEM((GLA_HEADS, HEAD_V, HEAD_K), F32),
            pltpu.VMEM((tm + POOL_HALO, POOL_WIDTH), F32),
        ],
        compiler_params=pltpu.CompilerParams(
            dimension_semantics=("arbitrary", "arbitrary"), vmem_limit_bytes=VMEM_LIMIT),
    )(x, *operands)


def _rank_desc(vals, n):
    idx = lax.broadcasted_iota(I32, vals.shape, 0)
    rank = jnp.zeros(vals.shape, I32)
    for other in range(n):
        o = vals[other:other + 1, :]
        before = (o > vals) | ((o == vals) & (other < idx))
        rank = rank + before.astype(I32)
    return rank


def _router_kernel(h_ref, wrt_ref, bias_ref, eid_ref, pos_ref, wk_ref, cnt_ref, carry_ref):
    i = pl.program_id(0)

    @pl.when(i == 0)
    def _():
        carry_ref[...] = jnp.zeros_like(carry_ref)

    h = h_ref[...]
    tr = h.shape[0]
    logits = lax.dot_general(wrt_ref[...], h, _NT, preferred_element_type=F32,
                             precision=lax.Precision.HIGHEST)
    scores = jax.nn.sigmoid(logits)
    biased = scores + bias_ref[...]
    grp = biased.reshape(N_GROUPS, EXPERTS_PER_GROUP, tr)
    slot = lax.broadcasted_iota(I32, grp.shape, 1)
    top1 = jnp.max(grp, axis=1, keepdims=True)
    first = jnp.min(jnp.where(grp == top1, slot, EXPERTS_PER_GROUP), axis=1, keepdims=True)
    top2 = jnp.max(jnp.where(slot == first, -jnp.inf, grp), axis=1)
    grp_score = top1[:, 0, :] + top2
    grp_on = _rank_desc(grp_score, N_GROUPS) < TOP_GROUPS
    emask = jnp.broadcast_to(grp_on[:, None, :], grp.shape).reshape(N_EXPERTS, tr)
    masked = jnp.where(emask, biased, -jnp.inf)
    erank = _rank_desc(masked, N_EXPERTS)
    chosen = (erank < TOP_K) & emask
    sel = jnp.where(chosen, scores, 0.0)
    gates = sel / jnp.sum(sel, axis=0, keepdims=True) * ROUTE_SCALE

    t_row = lax.broadcasted_iota(I32, (tr, tr), 0)
    t_col = lax.broadcasted_iota(I32, (tr, tr), 1)
    earlier = (t_row < t_col).astype(BF16)
    pos = carry_ref[...] + _dot(chosen.astype(BF16), earlier)
    carry_ref[...] += jnp.sum(chosen.astype(F32), axis=1, keepdims=True)
    cnt_ref[...] = carry_ref[...]

    eidx = lax.broadcasted_iota(I32, (N_EXPERTS, tr), 0).astype(F32)
    eids, poss, wks = [], [], []
    for k in range(TOP_K):
        m = chosen & (erank == k)
        eids.append(jnp.sum(jnp.where(m, eidx, 0.0), axis=0, keepdims=True))
        poss.append(jnp.sum(jnp.where(m, pos, 0.0), axis=0, keepdims=True))
        wks.append(jnp.sum(jnp.where(m, gates, 0.0), axis=0, keepdims=True))
    eid_ref[...] = jnp.concatenate(eids, axis=0).astype(I32)
    pos_ref[...] = jnp.concatenate(poss, axis=0).astype(I32)
    wk_ref[...] = jnp.concatenate(wks, axis=0).T


def _router(h2, w_router, router_bias):
    T, D = h2.shape
    tr = ROUTER_TILE
    assert T % tr == 0
    return pl.pallas_call(
        _router_kernel,
        name="router",
        grid=(T // tr,),
        in_specs=[pl.BlockSpec((tr, D), lambda i: (i, 0)),
                  _const_spec((N_EXPERTS, D)), _const_spec((N_EXPERTS, 1))],
        out_specs=[pl.BlockSpec((TOP_K, tr), lambda i: (0, i)),
                   pl.BlockSpec((TOP_K, tr), lambda i: (0, i)),
                   pl.BlockSpec((tr, TOP_K), lambda i: (i, 0)),
                   _const_spec((N_EXPERTS, 1))],
        out_shape=[jax.ShapeDtypeStruct((TOP_K, T), I32),
                   jax.ShapeDtypeStruct((TOP_K, T), I32),
                   jax.ShapeDtypeStruct((T, TOP_K), F32),
                   jax.ShapeDtypeStruct((N_EXPERTS, 1), F32)],
        scratch_shapes=[pltpu.VMEM((N_EXPERTS, 1), F32)],
        compiler_params=pltpu.CompilerParams(
            dimension_semantics=("arbitrary",), vmem_limit_bytes=VMEM_LIMIT),
    )(h2, w_router.T, router_bias.reshape(N_EXPERTS, 1))


def _slot_kernel(offs_ref, eid_ref, pos_ref, slot_ref):
    eid = eid_ref[...]
    base = jnp.zeros(eid.shape, I32)
    for e in range(N_EXPERTS):
        base = jnp.where(eid == e, offs_ref[e], base)
    slot_ref[...] = pos_ref[...] + base


def _slots(offsets, eid, pos):
    shape = eid.shape
    return pl.pallas_call(
        _slot_kernel,
        name="slots",
        grid_spec=pltpu.PrefetchScalarGridSpec(
            num_scalar_prefetch=1, grid=(1,),
            in_specs=[pl.BlockSpec(shape, lambda i, offs: (0, 0)),
                      pl.BlockSpec(shape, lambda i, offs: (0, 0))],
            out_specs=pl.BlockSpec(shape, lambda i, offs: (0, 0))),
        out_shape=jax.ShapeDtypeStruct(shape, I32),
        compiler_params=pltpu.CompilerParams(vmem_limit_bytes=VMEM_LIMIT),
    )(offsets, eid, pos)


def _sc_mesh():
    return plsc.VectorSubcoreMesh(core_axis_name="core", subcore_axis_name="subcore",
                                  num_cores=SC_CORES, num_subcores=SC_SUBCORES)


def _sc_worker():
    return lax.axis_index("subcore") * SC_CORES + lax.axis_index("core")


def _sc_dispatch(words, slots, n_rows):
    T = words.shape[0]
    n_chunks = slots.shape[0]
    workers = SC_CORES * SC_SUBCORES
    assert T == n_chunks * SC_CHUNK and n_chunks % workers == 0
    per_worker = n_chunks // workers

    @functools.partial(
        pl.kernel, mesh=_sc_mesh(),
        out_type=jax.ShapeDtypeStruct((n_rows, HALF), I32),
        scratch_types=[pltpu.VMEM((SC_CHUNK, HALF), I32), pltpu.VMEM((TOP_K, SC_CHUNK), I32),
                       pltpu.SemaphoreType.DMA])
    def dispatch(words_hbm, slots_hbm, out_hbm, rows_v, idx_v, sem):
        first = _sc_worker() * per_worker

        @pl.loop(0, per_worker)
        def _(i):
            c = first + i
            pltpu.sync_copy(slots_hbm.at[c], idx_v)
            pltpu.sync_copy(words_hbm.at[pl.ds(c * SC_CHUNK, SC_CHUNK)], rows_v)
            copies = [pltpu.async_copy(rows_v, out_hbm.at[idx_v.at[k]], sem) for k in range(TOP_K)]
            for cp in copies:
                cp.wait()

    return dispatch(words, slots)


def _sc_collect(rows, slots, T):
    n_chunks = slots.shape[0]
    workers = SC_CORES * SC_SUBCORES
    assert T == n_chunks * SC_CHUNK and n_chunks % workers == 0
    per_worker = n_chunks // workers

    @functools.partial(
        pl.kernel, mesh=_sc_mesh(),
        out_type=jax.ShapeDtypeStruct((TOP_K, T, HALF), I32),
        scratch_types=[pltpu.VMEM((2, SC_CHUNK, HALF), I32), pltpu.VMEM((TOP_K, SC_CHUNK), I32),
                       pltpu.SemaphoreType.DMA, pltpu.SemaphoreType.DMA((2,))])
    def collect(rows_hbm, slots_hbm, out_hbm, rows_v, idx_v, sem_in, sem_out):
        first = _sc_worker() * per_worker

        @pl.loop(0, per_worker)
        def _(i):
            c = first + i
            pltpu.sync_copy(slots_hbm.at[c], idx_v)
            writes = []
            for k in range(TOP_K):
                buf = rows_v.at[k % 2]
                if k >= 2:
                    writes[k - 2].wait()
                pltpu.async_copy(rows_hbm.at[idx_v.at[k]], buf, sem_in).wait()
                writes.append(pltpu.async_copy(buf, out_hbm.at[k, pl.ds(c * SC_CHUNK, SC_CHUNK)],
                                               sem_out.at[k % 2]))
            writes[TOP_K - 2].wait()
            writes[TOP_K - 1].wait()

    return collect(rows, slots)


def _expert_kernel(tile_expert_ref, tile_rows_ref, xs_ref, wg_ref, wu_ref, wd_ref, ys_ref,
                   wgu_ref, wdn_ref):
    i = pl.program_id(0)
    e = tile_expert_ref[i]
    n_valid = tile_rows_ref[i]
    e_prev = tile_expert_ref[jnp.maximum(i - 1, 0)]

    @pl.when((i == 0) | (e != e_prev))
    def _():
        wgu_ref[:, :EXPERT_FF] = wg_ref[0].astype(BF16)
        wgu_ref[:, EXPERT_FF:] = wu_ref[0].astype(BF16)
        wdn_ref[...] = wd_ref[0].astype(BF16)

    @pl.when(n_valid > 0)
    def _():
        words = xs_ref[...]
        live = lax.broadcasted_iota(I32, (words.shape[0], 1), 0) < n_valid
        lo, hi = _unpack_rows(jnp.where(live, words, 0))
        a = _dot(lo.astype(BF16), wgu_ref[:HALF, :]) + _dot(hi.astype(BF16), wgu_ref[HALF:, :])
        act = a[:, :EXPERT_FF]
        hid = act * jax.nn.sigmoid(act) * a[:, EXPERT_FF:]
        ys_ref[...] = _pack_rows(_dot(hid.astype(BF16), wdn_ref[...]))

    @pl.when(n_valid == 0)
    def _():
        ys_ref[...] = jnp.zeros_like(ys_ref)


def _experts(xs, tile_expert, tile_rows, w_exp_gate, w_exp_up, w_exp_down):
    n_rows = xs.shape[0]
    R = EXPERT_ROWS
    D = D_MODEL
    assert n_rows % R == 0
    return pl.pallas_call(
        _expert_kernel,
        name="experts",
        grid_spec=pltpu.PrefetchScalarGridSpec(
            num_scalar_prefetch=2, grid=(n_rows // R,),
            in_specs=[pl.BlockSpec((R, HALF), lambda i, te, tr: (i, 0)),
                      pl.BlockSpec((1, D, EXPERT_FF), lambda i, te, tr: (te[i], 0, 0)),
                      pl.BlockSpec((1, D, EXPERT_FF), lambda i, te, tr: (te[i], 0, 0)),
                      pl.BlockSpec((1, EXPERT_FF, D), lambda i, te, tr: (te[i], 0, 0))],
            out_specs=pl.BlockSpec((R, HALF), lambda i, te, tr: (i, 0)),
            scratch_shapes=[pltpu.VMEM((D, 2 * EXPERT_FF), BF16), pltpu.VMEM((EXPERT_FF, D), BF16)]),
        out_shape=jax.ShapeDtypeStruct((n_rows, HALF), I32),
        compiler_params=pltpu.CompilerParams(
            dimension_semantics=("arbitrary",), vmem_limit_bytes=VMEM_LIMIT),
    )(tile_expert, tile_rows, xs, w_exp_gate, w_exp_up, w_exp_down)


def _final_kernel(h_ref, yk_ref, wk_ref, wsgu_ref, wsd_ref, lng_ref, lnb_ref, out_ref, *, alpha):
    h = h_ref[...]
    a = _dot(h.astype(BF16), wsgu_ref[...])
    act = a[:, :SHARED_FF]
    hid = act * jax.nn.sigmoid(act) * a[:, SHARED_FF:]
    shared = _dot(hid.astype(BF16), wsd_ref[...])
    acc_lo = shared[:, :HALF]
    acc_hi = shared[:, HALF:]
    wk = wk_ref[...]
    for k in range(TOP_K):
        lo, hi = _unpack_rows(yk_ref[k])
        g = wk[:, k:k + 1]
        acc_lo = acc_lo + g * lo
        acc_hi = acc_hi + g * hi
    ffn = jnp.concatenate([acc_lo, acc_hi], axis=1)
    out_ref[...] = _layer_norm(alpha * h + ffn, lng_ref[...], lnb_ref[...])


def _final(h2, yk, wk, w_sh_gate, w_sh_up, w_sh_down, ln_g, ln_b, alpha):
    T, D = h2.shape
    tm = FINAL_TILE
    assert T % tm == 0
    w_sgu = jnp.concatenate([w_sh_gate, w_sh_up], axis=-1).astype(BF16)
    w_sd = w_sh_down.astype(BF16)
    return pl.pallas_call(
        functools.partial(_final_kernel, alpha=alpha),
        name="final",
        grid=(T // tm,),
        in_specs=[pl.BlockSpec((tm, D), lambda i: (i, 0)),
                  pl.BlockSpec((TOP_K, tm, HALF), lambda i: (0, i, 0)),
                  pl.BlockSpec((tm, TOP_K), lambda i: (i, 0)),
                  _const_spec((D, 2 * SHARED_FF)), _const_spec((SHARED_FF, D)),
                  _const_spec((1, D)), _const_spec((1, D))],
        out_specs=pl.BlockSpec((tm, D), lambda i: (i, 0)),
        out_shape=jax.ShapeDtypeStruct((T, D), F32),
        compiler_params=pltpu.CompilerParams(
            dimension_semantics=("arbitrary",), vmem_limit_bytes=VMEM_LIMIT),
    )(h2, yk, wk, w_sgu, w_sd, ln_g.reshape(1, D), ln_b.reshape(1, D))


def _moe(h2, words, w_router, router_bias, w_exp_gate, w_exp_up, w_exp_down,
         w_sh_gate, w_sh_up, w_sh_down, ln_g, ln_b, alpha):
    T, D = h2.shape
    R = EXPERT_ROWS
    eid, pos, wk, counts = _router(h2, w_router, router_bias)
    counts = counts.reshape(N_EXPERTS).astype(I32)
    padded = (counts + (R - 1)) // R * R
    ends = jnp.cumsum(padded)
    offsets = ends - padded
    n_rows = T * TOP_K + N_EXPERTS * R
    tile_start = jnp.arange(n_rows // R, dtype=I32) * R
    tile_expert = jnp.minimum(jnp.searchsorted(ends, tile_start, side="right"), N_EXPERTS - 1).astype(I32)
    tile_rows = jnp.clip(counts[tile_expert] - (tile_start - offsets[tile_expert]), 0, R)
    tile_rows = jnp.where(tile_start < ends[-1], tile_rows, 0).astype(I32)
    slots = _slots(offsets.astype(I32), eid, pos)
    slots = slots.reshape(TOP_K, T // SC_CHUNK, SC_CHUNK).transpose(1, 0, 2)
    xs = _sc_dispatch(words, slots, n_rows)
    ys = _experts(xs, tile_expert, tile_rows, w_exp_gate, w_exp_up, w_exp_down)
    yk = _sc_collect(ys, slots, T)
    return _final(h2, yk, wk, w_sh_gate, w_sh_up, w_sh_down, ln_g, ln_b, alpha)


def kernel(x, w_in, w_gate_up, b_gate, gla_norm_g, w_gla_up, w_pool_grp, pool_scale, w_pool_up,
           w_out, ln1_g, ln1_b, w_router, router_bias, w_exp_gate, w_exp_up, w_exp_down,
           w_sh_gate, w_sh_up, w_sh_down, ln2_g, ln2_b):
    B, S, D = x.shape
    depth = w_in.shape[0]
    alpha = (2.0 * depth) ** 0.25
    h = x
    for l in range(depth):
        h, words = _mixer(h, w_in[l], w_gate_up[l], b_gate[l], gla_norm_g[l], w_gla_up[l],
                          w_pool_grp[l], pool_scale[l], w_pool_up[l], w_out[l], ln1_g[l], ln1_b[l],
                          alpha)
        h2 = _moe(h.reshape(B * S, D), words.reshape(B * S, HALF), w_router[l], router_bias[l],
                  w_exp_gate[l], w_exp_up[l], w_exp_down[l], w_sh_gate[l], w_sh_up[l],
                  w_sh_down[l], ln2_g[l], ln2_b[l], alpha)
        h = h2.reshape(B, S, D)
    return h
```

```python
import functools

import jax
import jax.numpy as jnp
from jax import lax
from jax.experimental import pallas as pl
from jax.experimental.pallas import tpu as pltpu
from jax.experimental.pallas import tpu_sc as plsc

F32 = jnp.float32
BF16 = jnp.bfloat16
I32 = jnp.int32
U32 = jnp.uint32

D_MODEL = 1024
GLA_HEADS = 4
GLA_DK = D_MODEL // 2
GLA_DV = D_MODEL
HEAD_K = GLA_DK // GLA_HEADS
HEAD_V = GLA_DV // GLA_HEADS
GATE_RANK = 16
GATE_TEMP = 16.0
POOL_WIDTH = D_MODEL // 2
POOL_GROUPS = 4
POOL_GROUP_DIM = POOL_WIDTH // POOL_GROUPS
POOL_WINDOWS = (2, 4, 8, 16)
POOL_HALO = 16
N_EXPERTS = 64
TOP_K = 8
N_GROUPS = 8
TOP_GROUPS = 4
EXPERTS_PER_GROUP = N_EXPERTS // N_GROUPS
EXPERT_FF = 256
SHARED_FF = 256
ROUTE_SCALE = 2.5
LN_EPS = 1e-5
RMS_EPS = 1e-6
LANES = 128

MIX_TILE = 256
ROUTER_TILE = 512
FINAL_TILE = 512
EXPERT_ROWS = 1024
SC_CORES = 2
SC_SUBCORES = 16
SC_CHUNK = 64
HALF = D_MODEL // 2
VMEM_LIMIT = 56 * 1024 * 1024

_dot = functools.partial(jnp.dot, preferred_element_type=F32)
_NT = (((1,), (1,)), ((), ()))
_TN = (((0,), (0,)), ((), ()))


def _layer_norm(y, g, b):
    mu = jnp.mean(y, axis=-1, keepdims=True)
    yc = y - mu
    var = jnp.mean(yc * yc, axis=-1, keepdims=True)
    return yc * lax.rsqrt(var + LN_EPS) * g + b


def _split3(a):
    hi = a.astype(BF16)
    r1 = a - hi.astype(F32)
    mid = r1.astype(BF16)
    lo = (r1 - mid.astype(F32)).astype(BF16)
    return hi, mid, lo


def _pack_rows(y):
    lo = lax.bitcast_convert_type(y[:, :HALF].astype(BF16).astype(F32), U32)
    hi = lax.bitcast_convert_type(y[:, HALF:].astype(BF16).astype(F32), U32)
    return lax.bitcast_convert_type(hi | (lo >> 16), I32)


def _unpack_rows(w):
    u = lax.bitcast_convert_type(w, U32)
    lo = lax.bitcast_convert_type(u << 16, F32)
    hi = lax.bitcast_convert_type(u & jnp.uint32(0xFFFF0000), F32)
    return lo, hi


def _mixer_kernel(x_ref, wqk_ref, wv_ref, wr_ref, wglr_ref, wgup_ref, bgate_ref, wu_ref,
                  wga_ref, wgb_ref, gnorm_ref, wglaup_ref, wpgrp_ref, pscale_ref, wpup_ref,
                  wout_ref, lng_ref, lnb_ref, out_ref, words_ref, state_ref, ucat_ref,
                  *, alpha, tm):
    j = pl.program_id(1)

    @pl.when(j == 0)
    def _():
        state_ref[...] = jnp.zeros_like(state_ref)
        ucat_ref[0:POOL_HALO, :] = jnp.zeros((POOL_HALO, POOL_WIDTH), F32)

    x = x_ref[0]
    xb = x.astype(BF16)

    qk = _dot(xb, wqk_ref[...])
    v = _dot(xb, wv_ref[...])
    g_lr = _dot(xb, wglr_ref[...])
    z = _dot(g_lr.astype(BF16), wgup_ref[...]) + bgate_ref[...]
    log_decay = (jnp.minimum(z, 0.0) - jnp.log1p(jnp.exp(-jnp.abs(z)))) * (1.0 / GATE_TEMP)

    row = lax.broadcasted_iota(I32, (tm, tm), 0)
    col = lax.broadcasted_iota(I32, (tm, tm), 1)
    causal = row >= col
    tri = causal.astype(BF16)
    g_hi, g_mid, g_lo = _split3(log_decay)
    bcum = _dot(tri, g_hi) + _dot(tri, g_mid) + _dot(tri, g_lo)

    gnorm = gnorm_ref[...]
    heads = []
    for h in range(GLA_HEADS):
        ks = slice(h * HEAD_K, (h + 1) * HEAD_K)
        b = bcum[:, ks]
        b_last = b[tm - 1:tm, :]
        q_h = qk[:, ks] * (HEAD_K ** -0.5)
        k_h = qk[:, GLA_DK + h * HEAD_K:GLA_DK + (h + 1) * HEAD_K]
        v_h = v[:, h * HEAD_V:(h + 1) * HEAD_V].astype(BF16)
        q_in = (q_h * jnp.exp(b)).astype(BF16)
        k_out = (k_h * jnp.exp(-b)).astype(BF16)
        k_end = (k_h * jnp.exp(b_last - b)).astype(BF16)
        scores = lax.dot_general(q_in, k_out, _NT, preferred_element_type=F32)
        scores = jnp.where(causal, scores, 0.0).astype(BF16)
        st = state_ref[h]
        o = _dot(scores, v_h) + lax.dot_general(q_in, st.astype(BF16), _NT,
                                                preferred_element_type=F32)
        state_ref[h] = st * jnp.exp(b_last) + lax.dot_general(
            v_h, k_end, _TN, preferred_element_type=F32)
        ms = jnp.mean(o * o, axis=-1, keepdims=True)
        heads.append(o * lax.rsqrt(ms + RMS_EPS) * gnorm)
    o = jnp.concatenate(heads, axis=1)
    r = _dot(xb, wr_ref[...])
    o = o * (r * jax.nn.sigmoid(r))
    y_gla = _dot(o.astype(BF16), wglaup_ref[...])

    u = _dot(xb, wu_ref[...])
    ucat_ref[POOL_HALO:, :] = u
    pos = j * tm - POOL_HALO + lax.broadcasted_iota(I32, (tm + POOL_HALO, 1), 0)
    mixed = []
    for gi, window in enumerate(POOL_WINDOWS):
        a = ucat_ref[:, gi * POOL_GROUP_DIM:(gi + 1) * POOL_GROUP_DIM]
        s = a
        step = 1
        while step < window:
            s = s + pltpu.roll(s, step, 0)
            step *= 2
        count = jnp.clip(pos + 1, 1, window).astype(F32)
        pooled = (s / count - a)[POOL_HALO:, :]
        mixed.append(_dot(pooled.astype(BF16), wpgrp_ref[gi]))
    ucat_ref[0:POOL_HALO, :] = u[tm - POOL_HALO:, :]
    mixed = jnp.concatenate(mixed, axis=1) * pscale_ref[...]
    y_pool = _dot(mixed.astype(BF16), wpup_ref[...])

    gate_a = _dot(xb, wga_ref[...])
    gate_b = _dot(xb, wgb_ref[...])
    merged = jax.nn.sigmoid(gate_a) * y_gla + jax.nn.sigmoid(gate_b) * y_pool
    mix = _dot(merged.astype(BF16), wout_ref[...])
    y = _layer_norm(alpha * x + mix, lng_ref[...], lnb_ref[...])
    out_ref[0] = y
    words_ref[0] = _pack_rows(y)


def _const_spec(shape):
    zeros = (0,) * len(shape)
    return pl.BlockSpec(shape, lambda *_: zeros)


def _mixer(x, w_in, w_gate_up, b_gate, gla_norm_g, w_gla_up, w_pool_grp, pool_scale,
           w_pool_up, w_out, ln_g, ln_b, alpha):
    B, S, D = x.shape
    tm = MIX_TILE
    assert D == D_MODEL and S % tm == 0
    c = 0
    w_qk = w_in[:, c:c + 2 * GLA_DK].astype(BF16); c += 2 * GLA_DK
    w_v = w_in[:, c:c + GLA_DV].astype(BF16); c += GLA_DV
    w_r = w_in[:, c:c + GLA_DV].astype(BF16); c += GLA_DV
    w_glr = jnp.pad(w_in[:, c:c + GATE_RANK], ((0, 0), (0, LANES - GATE_RANK))).astype(BF16); c += GATE_RANK
    w_u = w_in[:, c:c + POOL_WIDTH].astype(BF16); c += POOL_WIDTH
    w_ga = w_in[:, c:c + D].astype(BF16); c += D
    w_gb = w_in[:, c:c + D].astype(BF16); c += D
    assert c == w_in.shape[1]
    w_gup = jnp.pad(w_gate_up, ((0, LANES - GATE_RANK), (0, 0))).astype(BF16)
    operands = (
        w_qk, w_v, w_r, w_glr, w_gup, b_gate.reshape(1, GLA_DK), w_u, w_ga, w_gb,
        gla_norm_g.reshape(1, HEAD_V), w_gla_up.astype(BF16), w_pool_grp.astype(BF16),
        pool_scale.reshape(1, POOL_WIDTH), w_pool_up.astype(BF16), w_out.astype(BF16),
        ln_g.reshape(1, D), ln_b.reshape(1, D))
    return pl.pallas_call(
        functools.partial(_mixer_kernel, alpha=alpha, tm=tm),
        name="mixer",
        grid=(B, S // tm),
        in_specs=[pl.BlockSpec((1, tm, D), lambda b, j: (b, j, 0))]
        + [_const_spec(w.shape) for w in operands],
        out_specs=[pl.BlockSpec((1, tm, D), lambda b, j: (b, j, 0)),
                   pl.BlockSpec((1, tm, HALF), lambda b, j: (b, j, 0))],
        out_shape=[jax.ShapeDtypeStruct((B, S, D), F32),
                   jax.ShapeDtypeStruct((B, S, HALF), I32)],
        scratch_shapes=[
            pltpu.VMEM((GLA_HEADS, HEAD_V, HEAD_K), F32),
            pltpu.VMEM((tm + POOL_HALO, POOL_WIDTH), F32),
        ],
        compiler_params=pltpu.CompilerParams(
            dimension_semantics=("arbitrary", "arbitrary"), vmem_limit_bytes=VMEM_LIMIT),
    )(x, *operands)


def _rank_desc(vals, n):
    idx = lax.broadcasted_iota(I32, vals.shape, 0)
    rank = jnp.zeros(vals.shape, I32)
    for other in range(n):
        o = vals[other:other + 1, :]
        before = (o > vals) | ((o == vals) & (other < idx))
        rank = rank + before.astype(I32)
    return rank


def _router_kernel(h_ref, wrt_ref, bias_ref, eid_ref, pos_ref, wk_ref, cnt_ref, carry_ref):
    i = pl.program_id(0)

    @pl.when(i == 0)
    def _():
        carry_ref[...] = jnp.zeros_like(carry_ref)

    h = h_ref[...]
    tr = h.shape[0]
    logits = lax.dot_general(wrt_ref[...], h, _NT, preferred_element_type=F32,
                             precision=lax.Precision.HIGHEST)
    scores = jax.nn.sigmoid(logits)
    biased = scores + bias_ref[...]
    grp = biased.reshape(N_GROUPS, EXPERTS_PER_GROUP, tr)
    slot = lax.broadcasted_iota(I32, grp.shape, 1)
    top1 = jnp.max(grp, axis=1, keepdims=True)
    first = jnp.min(jnp.where(grp == top1, slot, EXPERTS_PER_GROUP), axis=1, keepdims=True)
    top2 = jnp.max(jnp.where(slot == first, -jnp.inf, grp), axis=1)
    grp_score = top1[:, 0, :] + top2
    grp_on = _rank_desc(grp_score, N_GROUPS) < TOP_GROUPS
    emask = jnp.broadcast_to(grp_on[:, None, :], grp.shape).reshape(N_EXPERTS, tr)
    masked = jnp.where(emask, biased, -jnp.inf)
    erank = _rank_desc(masked, N_EXPERTS)
    chosen = (erank < TOP_K) & emask
    sel = jnp.where(chosen, scores, 0.0)
    gates = sel / jnp.sum(sel, axis=0, keepdims=True) * ROUTE_SCALE

    t_row = lax.broadcasted_iota(I32, (tr, tr), 0)
    t_col = lax.broadcasted_iota(I32, (tr, tr), 1)
    earlier = (t_row < t_col).astype(BF16)
    pos = carry_ref[...] + _dot(chosen.astype(BF16), earlier)
    carry_ref[...] += jnp.sum(chosen.astype(F32), axis=1, keepdims=True)
    cnt_ref[...] = carry_ref[...]

    eidx = lax.broadcasted_iota(I32, (N_EXPERTS, tr), 0).astype(F32)
    eids, poss, wks = [], [], []
    for k in range(TOP_K):
        m = chosen & (erank == k)
        eids.append(jnp.sum(jnp.where(m, eidx, 0.0), axis=0, keepdims=True))
        poss.append(jnp.sum(jnp.where(m, pos, 0.0), axis=0, keepdims=True))
        wks.append(jnp.sum(jnp.where(m, gates, 0.0), axis=0, keepdims=True))
    eid_ref[...] = jnp.concatenate(eids, axis=0).astype(I32)
    pos_ref[...] = jnp.concatenate(poss, axis=0).astype(I32)
    wk_ref[...] = jnp.concatenate(wks, axis=0).T


def _router(h2, w_router, router_bias):
    T, D = h2.shape
    tr = ROUTER_TILE
    assert T % tr == 0
    return pl.pallas_call(
        _router_kernel,
        name="router",
        grid=(T // tr,),
        in_specs=[pl.BlockSpec((tr, D), lambda i: (i, 0)),
                  _const_spec((N_EXPERTS, D)), _const_spec((N_EXPERTS, 1))],
        out_specs=[pl.BlockSpec((TOP_K, tr), lambda i: (0, i)),
                   pl.BlockSpec((TOP_K, tr), lambda i: (0, i)),
                   pl.BlockSpec((tr, TOP_K), lambda i: (i, 0)),
                   _const_spec((N_EXPERTS, 1))],
        out_shape=[jax.ShapeDtypeStruct((TOP_K, T), I32),
                   jax.ShapeDtypeStruct((TOP_K, T), I32),
                   jax.ShapeDtypeStruct((T, TOP_K), F32),
                   jax.ShapeDtypeStruct((N_EXPERTS, 1), F32)],
        scratch_shapes=[pltpu.VMEM((N_EXPERTS, 1), F32)],
        compiler_params=pltpu.CompilerParams(
            dimension_semantics=("arbitrary",), vmem_limit_bytes=VMEM_LIMIT),
    )(h2, w_router.T, router_bias.reshape(N_EXPERTS, 1))


def _slot_kernel(offs_ref, eid_ref, pos_ref, slot_ref):
    eid = eid_ref[...]
    base = jnp.zeros(eid.shape, I32)
    for e in range(N_EXPERTS):
        base = jnp.where(eid == e, offs_ref[e], base)
    slot_ref[...] = pos_ref[...] + base


def _slots(offsets, eid, pos):
    shape = eid.shape
    return pl.pallas_call(
        _slot_kernel,
        name="slots",
        grid_spec=pltpu.PrefetchScalarGridSpec(
            num_scalar_prefetch=1, grid=(1,),
            in_specs=[pl.BlockSpec(shape, lambda i, offs: (0, 0)),
                      pl.BlockSpec(shape, lambda i, offs: (0, 0))],
            out_specs=pl.BlockSpec(shape, lambda i, offs: (0, 0))),
        out_shape=jax.ShapeDtypeStruct(shape, I32),
        compiler_params=pltpu.CompilerParams(vmem_limit_bytes=VMEM_LIMIT),
    )(offsets, eid, pos)


def _sc_mesh():
    return plsc.VectorSubcoreMesh(core_axis_name="core", subcore_axis_name="subcore",
                                  num_cores=SC_CORES, num_subcores=SC_SUBCORES)


def _sc_worker():
    return lax.axis_index("subcore") * SC_CORES + lax.axis_index("core")


def _sc_dispatch(words, slots, n_rows):
    T = words.shape[0]
    n_chunks = slots.shape[0]
    workers = SC_CORES * SC_SUBCORES
    assert T == n_chunks * SC_CHUNK and n_chunks % workers == 0
    per_worker = n_chunks // workers

    @functools.partial(
        pl.kernel, mesh=_sc_mesh(),
        out_type=jax.ShapeDtypeStruct((n_rows, HALF), I32),
        scratch_types=[pltpu.VMEM((SC_CHUNK, HALF), I32), pltpu.VMEM((TOP_K, SC_CHUNK), I32),
                       pltpu.SemaphoreType.DMA])
    def dispatch(words_hbm, slots_hbm, out_hbm, rows_v, idx_v, sem):
        first = _sc_worker() * per_worker

        @pl.loop(0, per_worker)
        def _(i):
            c = first + i
            pltpu.sync_copy(slots_hbm.at[c], idx_v)
            pltpu.sync_copy(words_hbm.at[pl.ds(c * SC_CHUNK, SC_CHUNK)], rows_v)
            copies = [pltpu.async_copy(rows_v, out_hbm.at[idx_v.at[k]], sem) for k in range(TOP_K)]
            for cp in copies:
                cp.wait()

    return dispatch(words, slots)


def _sc_collect(rows, slots, T):
    n_chunks = slots.shape[0]
    workers = SC_CORES * SC_SUBCORES
    assert T == n_chunks * SC_CHUNK and n_chunks % workers == 0
    per_worker = n_chunks // workers

    @functools.partial(
        pl.kernel, mesh=_sc_mesh(),
        out_type=jax.ShapeDtypeStruct((TOP_K, T, HALF), I32),
        scratch_types=[pltpu.VMEM((2, SC_CHUNK, HALF), I32), pltpu.VMEM((TOP_K, SC_CHUNK), I32),
                       pltpu.SemaphoreType.DMA, pltpu.SemaphoreType.DMA((2,))])
    def collect(rows_hbm, slots_hbm, out_hbm, rows_v, idx_v, sem_in, sem_out):
        first = _sc_worker() * per_worker

        @pl.loop(0, per_worker)
        def _(i):
            c = first + i
            pltpu.sync_copy(slots_hbm.at[c], idx_v)
            writes = []
            for k in range(TOP_K):
                buf = rows_v.at[k % 2]
                if k >= 2:
                    writes[k - 2].wait()
                pltpu.async_copy(rows_hbm.at[idx_v.at[k]], buf, sem_in).wait()
                writes.append(pltpu.async_copy(buf, out_hbm.at[k, pl.ds(c * SC_CHUNK, SC_CHUNK)],
                                               sem_out.at[k % 2]))
            writes[TOP_K - 2].wait()
            writes[TOP_K - 1].wait()

    return collect(rows, slots)


def _expert_kernel(tile_expert_ref, tile_rows_ref, xs_ref, wg_ref, wu_ref, wd_ref, ys_ref,
                   wgu_ref, wdn_ref):
    i = pl.program_id(0)
    e = tile_expert_ref[i]
    n_valid = tile_rows_ref[i]
    e_prev = tile_expert_ref[jnp.maximum(i - 1, 0)]

    @pl.when((i == 0) | (e != e_prev))
    def _():
        wgu_ref[:, :EXPERT_FF] = wg_ref[0].astype(BF16)
        wgu_ref[:, EXPERT_FF:] = wu_ref[0].astype(BF16)
        wdn_ref[...] = wd_ref[0].astype(BF16)

    @pl.when(n_valid > 0)
    def _():
        words = xs_ref[...]
        live = lax.broadcasted_iota(I32, (words.shape[0], 1), 0) < n_valid
        lo, hi = _unpack_rows(jnp.where(live, words, 0))
        a = _dot(lo.astype(BF16), wgu_ref[:HALF, :]) + _dot(hi.astype(BF16), wgu_ref[HALF:, :])
        act = a[:, :EXPERT_FF]
        hid = act * jax.nn.sigmoid(act) * a[:, EXPERT_FF:]
        ys_ref[...] = _pack_rows(_dot(hid.astype(BF16), wdn_ref[...]))

    @pl.when(n_valid == 0)
    def _():
        ys_ref[...] = jnp.zeros_like(ys_ref)


def _experts(xs, tile_expert, tile_rows, w_exp_gate, w_exp_up, w_exp_down):
    n_rows = xs.shape[0]
    R = EXPERT_ROWS
    D = D_MODEL
    assert n_rows % R == 0
    return pl.pallas_call(
        _expert_kernel,
        name="experts",
        grid_spec=pltpu.PrefetchScalarGridSpec(
            num_scalar_prefetch=2, grid=(n_rows // R,),
            in_specs=[pl.BlockSpec((R, HALF), lambda i, te, tr: (i, 0)),
                      pl.BlockSpec((1, D, EXPERT_FF), lambda i, te, tr: (te[i], 0, 0)),
                      pl.BlockSpec((1, D, EXPERT_FF), lambda i, te, tr: (te[i], 0, 0)),
                      pl.BlockSpec((1, EXPERT_FF, D), lambda i, te, tr: (te[i], 0, 0))],
            out_specs=pl.BlockSpec((R, HALF), lambda i, te, tr: (i, 0)),
            scratch_shapes=[pltpu.VMEM((D, 2 * EXPERT_FF), BF16), pltpu.VMEM((EXPERT_FF, D), BF16)]),
        out_shape=jax.ShapeDtypeStruct((n_rows, HALF), I32),
        compiler_params=pltpu.CompilerParams(
            dimension_semantics=("arbitrary",), vmem_limit_bytes=VMEM_LIMIT),
    )(tile_expert, tile_rows, xs, w_exp_gate, w_exp_up, w_exp_down)


def _final_kernel(h_ref, yk_ref, wk_ref, wsgu_ref, wsd_ref, lng_ref, lnb_ref, out_ref, *, alpha):
    h = h_ref[...]
    a = _dot(h.astype(BF16), wsgu_ref[...])
    act = a[:, :SHARED_FF]
    hid = act * jax.nn.sigmoid(act) * a[:, SHARED_FF:]
    shared = _dot(hid.astype(BF16), wsd_ref[...])
    acc_lo = shared[:, :HALF]
    acc_hi = shared[:, HALF:]
    wk = wk_ref[...]
    for k in range(TOP_K):
        lo, hi = _unpack_rows(yk_ref[k])
        g = wk[:, k:k + 1]
        acc_lo = acc_lo + g * lo
        acc_hi = acc_hi + g * hi
    ffn = jnp.concatenate([acc_lo, acc_hi], axis=1)
    out_ref[...] = _layer_norm(alpha * h + ffn, lng_ref[...], lnb_ref[...])


def _final(h2, yk, wk, w_sh_gate, w_sh_up, w_sh_down, ln_g, ln_b, alpha):
    T, D = h2.shape
    tm = FINAL_TILE
    assert T % tm == 0
    w_sgu = jnp.concatenate([w_sh_gate, w_sh_up], axis=-1).astype(BF16)
    w_sd = w_sh_down.astype(BF16)
    return pl.pallas_call(
        functools.partial(_final_kernel, alpha=alpha),
        name="final",
        grid=(T // tm,),
        in_specs=[pl.BlockSpec((tm, D), lambda i: (i, 0)),
                  pl.BlockSpec((TOP_K, tm, HALF), lambda i: (0, i, 0)),
                  pl.BlockSpec((tm, TOP_K), lambda i: (i, 0)),
                  _const_spec((D, 2 * SHARED_FF)), _const_spec((SHARED_FF, D)),
                  _const_spec((1, D)), _const_spec((1, D))],
        out_specs=pl.BlockSpec((tm, D), lambda i: (i, 0)),
        out_shape=jax.ShapeDtypeStruct((T, D), F32),
        compiler_params=pltpu.CompilerParams(
            dimension_semantics=("arbitrary",), vmem_limit_bytes=VMEM_LIMIT),
    )(h2, yk, wk, w_sgu, w_sd, ln_g.reshape(1, D), ln_b.reshape(1, D))


def _moe(h2, words, w_router, router_bias, w_exp_gate, w_exp_up, w_exp_down,
         w_sh_gate, w_sh_up, w_sh_down, ln_g, ln_b, alpha):
    T, D = h2.shape
    R = EXPERT_ROWS
    eid, pos, wk, counts = _router(h2, w_router, router_bias)
    counts = counts.reshape(N_EXPERTS).astype(I32)
    padded = (counts + (R - 1)) // R * R
    ends = jnp.cumsum(padded)
    offsets = ends - padded
    n_rows = T * TOP_K + N_EXPERTS * R
    tile_start = jnp.arange(n_rows // R, dtype=I32) * R
    tile_expert = jnp.sum((tile_start[:, None] >= ends[None, :]).astype(I32), axis=1)
    tile_expert = jnp.minimum(tile_expert, N_EXPERTS - 1)
    tile_rows = jnp.clip(counts[tile_expert] - (tile_start - offsets[tile_expert]), 0, R)
    tile_rows = jnp.where(tile_start < ends[-1], tile_rows, 0).astype(I32)
    slots = _slots(offsets.astype(I32), eid, pos)
    slots = slots.reshape(TOP_K, T // SC_CHUNK, SC_CHUNK).transpose(1, 0, 2)
    xs = _sc_dispatch(words, slots, n_rows)
    ys = _experts(xs, tile_expert, tile_rows, w_exp_gate, w_exp_up, w_exp_down)
    yk = _sc_collect(ys, slots, T)
    return _final(h2, yk, wk, w_sh_gate, w_sh_up, w_sh_down, ln_g, ln_b, alpha)


def kernel(x, w_in, w_gate_up, b_gate, gla_norm_g, w_gla_up, w_pool_grp, pool_scale, w_pool_up,
           w_out, ln1_g, ln1_b, w_router, router_bias, w_exp_gate, w_exp_up, w_exp_down,
           w_sh_gate, w_sh_up, w_sh_down, ln2_g, ln2_b):
    B, S, D = x.shape
    depth = w_in.shape[0]
    alpha = (2.0 * depth) ** 0.25
    h = x
    for l in range(depth):
        h, words = _mixer(h, w_in[l], w_gate_up[l], b_gate[l], gla_norm_g[l], w_gla_up[l],
                          w_pool_grp[l], pool_scale[l], w_pool_up[l], w_out[l], ln1_g[l], ln1_b[l],
                          alpha)
        h2 = _moe(h.reshape(B * S, D), words.reshape(B * S, HALF), w_router[l], router_bias[l],
                  w_exp_gate[l], w_exp_up[l], w_exp_down[l], w_sh_gate[l], w_sh_up[l],
                  w_sh_down[l], ln2_g[l], ln2_b[l], alpha)
        h = h2.reshape(B, S, D)
    return h
```

```python
import functools

import jax
import jax.numpy as jnp
from jax import lax
from jax.experimental import pallas as pl
from jax.experimental.pallas import tpu as pltpu
from jax.experimental.pallas import tpu_sc as plsc

F32 = jnp.float32
BF16 = jnp.bfloat16
I32 = jnp.int32
U32 = jnp.uint32

D_MODEL = 1024
GLA_HEADS = 4
GLA_DK = D_MODEL // 2
GLA_DV = D_MODEL
HEAD_K = GLA_DK // GLA_HEADS
HEAD_V = GLA_DV // GLA_HEADS
GATE_RANK = 16
GATE_TEMP = 16.0
POOL_WIDTH = D_MODEL // 2
POOL_GROUPS = 4
POOL_GROUP_DIM = POOL_WIDTH // POOL_GROUPS
POOL_WINDOWS = (2, 4, 8, 16)
POOL_HALO = 16
N_EXPERTS = 64
TOP_K = 8
N_GROUPS = 8
TOP_GROUPS = 4
EXPERTS_PER_GROUP = N_EXPERTS // N_GROUPS
EXPERT_FF = 256
SHARED_FF = 256
ROUTE_SCALE = 2.5
LN_EPS = 1e-5
RMS_EPS = 1e-6
LANES = 128

MIX_TILE = 256
ROUTER_TILE = 512
FINAL_TILE = 512
EXPERT_ROWS = 512
SC_CORES = 2
SC_SUBCORES = 16
SC_CHUNK = 64
HALF = D_MODEL // 2
VMEM_LIMIT = 56 * 1024 * 1024

_dot = functools.partial(jnp.dot, preferred_element_type=F32)
_NT = (((1,), (1,)), ((), ()))
_TN = (((0,), (0,)), ((), ()))


def _layer_norm(y, g, b):
    mu = jnp.mean(y, axis=-1, keepdims=True)
    yc = y - mu
    var = jnp.mean(yc * yc, axis=-1, keepdims=True)
    return yc * lax.rsqrt(var + LN_EPS) * g + b


def _split3(a):
    hi = a.astype(BF16)
    r1 = a - hi.astype(F32)
    mid = r1.astype(BF16)
    lo = (r1 - mid.astype(F32)).astype(BF16)
    return hi, mid, lo


def _pack_rows(y):
    lo = lax.bitcast_convert_type(y[:, :HALF].astype(BF16).astype(F32), U32)
    hi = lax.bitcast_convert_type(y[:, HALF:].astype(BF16).astype(F32), U32)
    return lax.bitcast_convert_type(hi | (lo >> 16), I32)


def _unpack_rows(w):
    u = lax.bitcast_convert_type(w, U32)
    lo = lax.bitcast_convert_type(u << 16, F32)
    hi = lax.bitcast_convert_type(u & jnp.uint32(0xFFFF0000), F32)
    return lo, hi


def _mixer_kernel(x_ref, wqk_ref, wv_ref, wr_ref, wglr_ref, wgup_ref, bgate_ref, wu_ref,
                  wga_ref, wgb_ref, gnorm_ref, wglaup_ref, wpgrp_ref, pscale_ref, wpup_ref,
                  wout_ref, lng_ref, lnb_ref, out_ref, words_ref, state_ref, ucat_ref,
                  *, alpha, tm):
    j = pl.program_id(1)

    @pl.when(j == 0)
    def _():
        state_ref[...] = jnp.zeros_like(state_ref)
        ucat_ref[0:POOL_HALO, :] = jnp.zeros((POOL_HALO, POOL_WIDTH), F32)

    x = x_ref[0]
    xb = x.astype(BF16)

    qk = _dot(xb, wqk_ref[...])
    v = _dot(xb, wv_ref[...])
    g_lr = _dot(xb, wglr_ref[...])
    z = _dot(g_lr.astype(BF16), wgup_ref[...]) + bgate_ref[...]
    log_decay = (jnp.minimum(z, 0.0) - jnp.log1p(jnp.exp(-jnp.abs(z)))) * (1.0 / GATE_TEMP)

    row = lax.broadcasted_iota(I32, (tm, tm), 0)
    col = lax.broadcasted_iota(I32, (tm, tm), 1)
    causal = row >= col
    tri = causal.astype(BF16)
    g_hi, g_mid, g_lo = _split3(log_decay)
    bcum = _dot(tri, g_hi) + _dot(tri, g_mid) + _dot(tri, g_lo)

    gnorm = gnorm_ref[...]
    heads = []
    for h in range(GLA_HEADS):
        ks = slice(h * HEAD_K, (h + 1) * HEAD_K)
        b = bcum[:, ks]
        b_last = b[tm - 1:tm, :]
        q_h = qk[:, ks] * (HEAD_K ** -0.5)
        k_h = qk[:, GLA_DK + h * HEAD_K:GLA_DK + (h + 1) * HEAD_K]
        v_h = v[:, h * HEAD_V:(h + 1) * HEAD_V].astype(BF16)
        q_in = (q_h * jnp.exp(b)).astype(BF16)
        k_out = (k_h * jnp.exp(-b)).astype(BF16)
        k_end = (k_h * jnp.exp(b_last - b)).astype(BF16)
        scores = lax.dot_general(q_in, k_out, _NT, preferred_element_type=F32)
        scores = jnp.where(causal, scores, 0.0).astype(BF16)
        st = state_ref[h]
        o = _dot(scores, v_h) + lax.dot_general(q_in, st.astype(BF16), _NT,
                                                preferred_element_type=F32)
        state_ref[h] = st * jnp.exp(b_last) + lax.dot_general(
            v_h, k_end, _TN, preferred_element_type=F32)
        ms = jnp.mean(o * o, axis=-1, keepdims=True)
        heads.append(o * lax.rsqrt(ms + RMS_EPS) * gnorm)
    o = jnp.concatenate(heads, axis=1)
    r = _dot(xb, wr_ref[...])
    o = o * (r * jax.nn.sigmoid(r))
    y_gla = _dot(o.astype(BF16), wglaup_ref[...])

    u = _dot(xb, wu_ref[...])
    ucat_ref[POOL_HALO:, :] = u
    pos = j * tm - POOL_HALO + lax.broadcasted_iota(I32, (tm + POOL_HALO, 1), 0)
    mixed = []
    for gi, window in enumerate(POOL_WINDOWS):
        a = ucat_ref[:, gi * POOL_GROUP_DIM:(gi + 1) * POOL_GROUP_DIM]
        s = a
        step = 1
        while step < window:
            s = s + pltpu.roll(s, step, 0)
            step *= 2
        count = jnp.clip(pos + 1, 1, window).astype(F32)
        pooled = (s / count - a)[POOL_HALO:, :]
        mixed.append(_dot(pooled.astype(BF16), wpgrp_ref[gi]))
    ucat_ref[0:POOL_HALO, :] = u[tm - POOL_HALO:, :]
    mixed = jnp.concatenate(mixed, axis=1) * pscale_ref[...]
    y_pool = _dot(mixed.astype(BF16), wpup_ref[...])

    gate_a = _dot(xb, wga_ref[...])
    gate_b = _dot(xb, wgb_ref[...])
    merged = jax.nn.sigmoid(gate_a) * y_gla + jax.nn.sigmoid(gate_b) * y_pool
    mix = _dot(merged.astype(BF16), wout_ref[...])
    y = _layer_norm(alpha * x + mix, lng_ref[...], lnb_ref[...])
    out_ref[0] = y
    words_ref[0] = _pack_rows(y)


def _const_spec(shape):
    zeros = (0,) * len(shape)
    return pl.BlockSpec(shape, lambda *_: zeros)


def _mixer_operands(w_in, w_gate_up, b_gate, gla_norm_g, w_gla_up, w_pool_grp, pool_scale,
                    w_pool_up, w_out, ln_g, ln_b):
    D = D_MODEL
    c = 0
    w_qk = w_in[:, c:c + 2 * GLA_DK].astype(BF16); c += 2 * GLA_DK
    w_v = w_in[:, c:c + GLA_DV].astype(BF16); c += GLA_DV
    w_r = w_in[:, c:c + GLA_DV].astype(BF16); c += GLA_DV
    w_glr = jnp.pad(w_in[:, c:c + GATE_RANK], ((0, 0), (0, LANES - GATE_RANK))).astype(BF16); c += GATE_RANK
    w_u = w_in[:, c:c + POOL_WIDTH].astype(BF16); c += POOL_WIDTH
    w_ga = w_in[:, c:c + D].astype(BF16); c += D
    w_gb = w_in[:, c:c + D].astype(BF16); c += D
    assert c == w_in.shape[1]
    w_gup = jnp.pad(w_gate_up, ((0, LANES - GATE_RANK), (0, 0))).astype(BF16)
    return (
        w_qk, w_v, w_r, w_glr, w_gup, b_gate.reshape(1, GLA_DK), w_u, w_ga, w_gb,
        gla_norm_g.reshape(1, HEAD_V), w_gla_up.astype(BF16), w_pool_grp.astype(BF16),
        pool_scale.reshape(1, POOL_WIDTH), w_pool_up.astype(BF16), w_out.astype(BF16),
        ln_g.reshape(1, D), ln_b.reshape(1, D))


def _mixer(x, operands, alpha, batch0, n_batch):
    _, S, D = x.shape
    tm = MIX_TILE
    assert D == D_MODEL and S % tm == 0
    return pl.pallas_call(
        functools.partial(_mixer_kernel, alpha=alpha, tm=tm),
        name="mixer",
        grid=(n_batch, S // tm),
        in_specs=[pl.BlockSpec((1, tm, D), lambda b, j: (b + batch0, j, 0))]
        + [_const_spec(w.shape) for w in operands],
        out_specs=[pl.BlockSpec((1, tm, D), lambda b, j: (b, j, 0)),
                   pl.BlockSpec((1, tm, HALF), lambda b, j: (b, j, 0))],
        out_shape=[jax.ShapeDtypeStruct((n_batch, S, D), F32),
                   jax.ShapeDtypeStruct((n_batch, S, HALF), I32)],
        scratch_shapes=[
            pltpu.VMEM((GLA_HEADS, HEAD_V, HEAD_K), F32),
            pltpu.VMEM((tm + POOL_HALO, POOL_WIDTH), F32),
        ],
        compiler_params=pltpu.CompilerParams(
            dimension_semantics=("arbitrary", "arbitrary"), vmem_limit_bytes=VMEM_LIMIT),
    )(x, *operands)


def _rank_desc(vals, n):
    idx = lax.broadcasted_iota(I32, vals.shape, 0)
    rank = jnp.zeros(vals.shape, I32)
    for other in range(n):
        o = vals[other:other + 1, :]
        before = (o > vals) | ((o == vals) & (other < idx))
        rank = rank + before.astype(I32)
    return rank


def _router_kernel(h_ref, wrt_ref, bias_ref, eid_ref, pos_ref, wk_ref, cnt_ref, carry_ref):
    i = pl.program_id(0)

    @pl.when(i == 0)
    def _():
        carry_ref[...] = jnp.zeros_like(carry_ref)

    h = h_ref[...]
    tr = h.shape[0]
    logits = lax.dot_general(wrt_ref[...], h, _NT, preferred_element_type=F32,
                             precision=lax.Precision.HIGHEST)
    scores = jax.nn.sigmoid(logits)
    biased = scores + bias_ref[...]
    grp = biased.reshape(N_GROUPS, EXPERTS_PER_GROUP, tr)
    slot = lax.broadcasted_iota(I32, grp.shape, 1)
    top1 = jnp.max(grp, axis=1, keepdims=True)
    first = jnp.min(jnp.where(grp == top1, slot, EXPERTS_PER_GROUP), axis=1, keepdims=True)
    top2 = jnp.max(jnp.where(slot == first, -jnp.inf, grp), axis=1)
    grp_score = top1[:, 0, :] + top2
    grp_on = _rank_desc(grp_score, N_GROUPS) < TOP_GROUPS
    emask = jnp.broadcast_to(grp_on[:, None, :], grp.shape).reshape(N_EXPERTS, tr)
    masked = jnp.where(emask, biased, -jnp.inf)
    erank = _rank_desc(masked, N_EXPERTS)
    chosen = (erank < TOP_K) & emask
    sel = jnp.where(chosen, scores, 0.0)
    gates = sel / jnp.sum(sel, axis=0, keepdims=True) * ROUTE_SCALE

    t_row = lax.broadcasted_iota(I32, (tr, tr), 0)
    t_col = lax.broadcasted_iota(I32, (tr, tr), 1)
    earlier = (t_row < t_col).astype(BF16)
    pos = carry_ref[...] + _dot(chosen.astype(BF16), earlier)
    carry_ref[...] += jnp.sum(chosen.astype(F32), axis=1, keepdims=True)
    cnt_ref[...] = carry_ref[...]

    eidx = lax.broadcasted_iota(I32, (N_EXPERTS, tr), 0).astype(F32)
    eids, poss, wks = [], [], []
    for k in range(TOP_K):
        m = chosen & (erank == k)
        eids.append(jnp.sum(jnp.where(m, eidx, 0.0), axis=0, keepdims=True))
        poss.append(jnp.sum(jnp.where(m, pos, 0.0), axis=0, keepdims=True))
        wks.append(jnp.sum(jnp.where(m, gates, 0.0), axis=0, keepdims=True))
    eid_ref[...] = jnp.concatenate(eids, axis=0).astype(I32)
    pos_ref[...] = jnp.concatenate(poss, axis=0).astype(I32)
    wk_ref[...] = jnp.concatenate(wks, axis=0).T


def _router(h2, w_router, router_bias):
    T, D = h2.shape
    tr = ROUTER_TILE
    assert T % tr == 0
    return pl.pallas_call(
        _router_kernel,
        name="router",
        grid=(T // tr,),
        in_specs=[pl.BlockSpec((tr, D), lambda i: (i, 0)),
                  _const_spec((N_EXPERTS, D)), _const_spec((N_EXPERTS, 1))],
        out_specs=[pl.BlockSpec((TOP_K, tr), lambda i: (0, i)),
                   pl.BlockSpec((TOP_K, tr), lambda i: (0, i)),
                   pl.BlockSpec((tr, TOP_K), lambda i: (i, 0)),
                   _const_spec((N_EXPERTS, 1))],
        out_shape=[jax.ShapeDtypeStruct((TOP_K, T), I32),
                   jax.ShapeDtypeStruct((TOP_K, T), I32),
                   jax.ShapeDtypeStruct((T, TOP_K), F32),
                   jax.ShapeDtypeStruct((N_EXPERTS, 1), F32)],
        scratch_shapes=[pltpu.VMEM((N_EXPERTS, 1), F32)],
        compiler_params=pltpu.CompilerParams(
            dimension_semantics=("arbitrary",), vmem_limit_bytes=VMEM_LIMIT),
    )(h2, w_router.T, router_bias.reshape(N_EXPERTS, 1))


def _slot_kernel(offs_ref, eid_ref, pos_ref, slot_ref):
    eid = eid_ref[...]
    base = jnp.zeros(eid.shape, I32)
    for e in range(N_EXPERTS):
        base = jnp.where(eid == e, offs_ref[e], base)
    slot_ref[...] = pos_ref[...] + base


def _slots(offsets, eid, pos):
    shape = eid.shape
    return pl.pallas_call(
        _slot_kernel,
        name="slots",
        grid_spec=pltpu.PrefetchScalarGridSpec(
            num_scalar_prefetch=1, grid=(1,),
            in_specs=[pl.BlockSpec(shape, lambda i, offs: (0, 0)),
                      pl.BlockSpec(shape, lambda i, offs: (0, 0))],
            out_specs=pl.BlockSpec(shape, lambda i, offs: (0, 0))),
        out_shape=jax.ShapeDtypeStruct(shape, I32),
        compiler_params=pltpu.CompilerParams(vmem_limit_bytes=VMEM_LIMIT),
    )(offsets, eid, pos)


def _sc_mesh():
    return plsc.VectorSubcoreMesh(core_axis_name="core", subcore_axis_name="subcore",
                                  num_cores=SC_CORES, num_subcores=SC_SUBCORES)


def _sc_worker():
    return lax.axis_index("subcore") * SC_CORES + lax.axis_index("core")


def _sc_dispatch(words, slots, n_rows):
    T = words.shape[0]
    n_chunks = slots.shape[0]
    workers = SC_CORES * SC_SUBCORES
    assert T == n_chunks * SC_CHUNK and n_chunks % workers == 0
    per_worker = n_chunks // workers

    @functools.partial(
        pl.kernel, mesh=_sc_mesh(),
        out_type=jax.ShapeDtypeStruct((n_rows, HALF), I32),
        scratch_types=[pltpu.VMEM((SC_CHUNK, HALF), I32), pltpu.VMEM((TOP_K, SC_CHUNK), I32),
                       pltpu.SemaphoreType.DMA])
    def dispatch(words_hbm, slots_hbm, out_hbm, rows_v, idx_v, sem):
        first = _sc_worker() * per_worker

        @pl.loop(0, per_worker)
        def _(i):
            c = first + i
            pltpu.sync_copy(slots_hbm.at[c], idx_v)
            pltpu.sync_copy(words_hbm.at[pl.ds(c * SC_CHUNK, SC_CHUNK)], rows_v)
            copies = [pltpu.async_copy(rows_v, out_hbm.at[idx_v.at[k]], sem) for k in range(TOP_K)]
            for cp in copies:
                cp.wait()

    return dispatch(words, slots)


def _sc_collect(rows, slots, T):
    n_chunks = slots.shape[0]
    workers = SC_CORES * SC_SUBCORES
    assert T == n_chunks * SC_CHUNK and n_chunks % workers == 0
    per_worker = n_chunks // workers

    @functools.partial(
        pl.kernel, mesh=_sc_mesh(),
        out_type=jax.ShapeDtypeStruct((TOP_K, T, HALF), I32),
        scratch_types=[pltpu.VMEM((2, SC_CHUNK, HALF), I32), pltpu.VMEM((TOP_K, SC_CHUNK), I32),
                       pltpu.SemaphoreType.DMA, pltpu.SemaphoreType.DMA((2,))])
    def collect(rows_hbm, slots_hbm, out_hbm, rows_v, idx_v, sem_in, sem_out):
        first = _sc_worker() * per_worker

        @pl.loop(0, per_worker)
        def _(i):
            c = first + i
            pltpu.sync_copy(slots_hbm.at[c], idx_v)
            writes = []
            for k in range(TOP_K):
                buf = rows_v.at[k % 2]
                if k >= 2:
                    writes[k - 2].wait()
                pltpu.async_copy(rows_hbm.at[idx_v.at[k]], buf, sem_in).wait()
                writes.append(pltpu.async_copy(buf, out_hbm.at[k, pl.ds(c * SC_CHUNK, SC_CHUNK)],
                                               sem_out.at[k % 2]))
            writes[TOP_K - 2].wait()
            writes[TOP_K - 1].wait()

    return collect(rows, slots)


def _expert_kernel(tile_expert_ref, tile_rows_ref, xs_ref, wg_ref, wu_ref, wd_ref, ys_ref,
                   wgu_ref, wdn_ref):
    i = pl.program_id(0)
    e = tile_expert_ref[i]
    n_valid = tile_rows_ref[i]
    e_prev = tile_expert_ref[jnp.maximum(i - 1, 0)]

    @pl.when((i == 0) | (e != e_prev))
    def _():
        wgu_ref[:, :EXPERT_FF] = wg_ref[0].astype(BF16)
        wgu_ref[:, EXPERT_FF:] = wu_ref[0].astype(BF16)
        wdn_ref[...] = wd_ref[0].astype(BF16)

    @pl.when(n_valid > 0)
    def _():
        words = xs_ref[...]
        live = lax.broadcasted_iota(I32, (words.shape[0], 1), 0) < n_valid
        lo, hi = _unpack_rows(jnp.where(live, words, 0))
        a = _dot(lo.astype(BF16), wgu_ref[:HALF, :]) + _dot(hi.astype(BF16), wgu_ref[HALF:, :])
        act = a[:, :EXPERT_FF]
        hid = act * jax.nn.sigmoid(act) * a[:, EXPERT_FF:]
        ys_ref[...] = _pack_rows(_dot(hid.astype(BF16), wdn_ref[...]))

    @pl.when(n_valid == 0)
    def _():
        ys_ref[...] = jnp.zeros_like(ys_ref)


def _experts(xs, tile_expert, tile_rows, w_exp_gate, w_exp_up, w_exp_down):
    n_rows = xs.shape[0]
    R = EXPERT_ROWS
    D = D_MODEL
    assert n_rows % R == 0
    return pl.pallas_call(
        _expert_kernel,
        name="experts",
        grid_spec=pltpu.PrefetchScalarGridSpec(
            num_scalar_prefetch=2, grid=(n_rows // R,),
            in_specs=[pl.BlockSpec((R, HALF), lambda i, te, tr: (i, 0)),
                      pl.BlockSpec((1, D, EXPERT_FF), lambda i, te, tr: (te[i], 0, 0)),
                      pl.BlockSpec((1, D, EXPERT_FF), lambda i, te, tr: (te[i], 0, 0)),
                      pl.BlockSpec((1, EXPERT_FF, D), lambda i, te, tr: (te[i], 0, 0))],
            out_specs=pl.BlockSpec((R, HALF), lambda i, te, tr: (i, 0)),
            scratch_shapes=[pltpu.VMEM((D, 2 * EXPERT_FF), BF16), pltpu.VMEM((EXPERT_FF, D), BF16)]),
        out_shape=jax.ShapeDtypeStruct((n_rows, HALF), I32),
        compiler_params=pltpu.CompilerParams(
            dimension_semantics=("arbitrary",), vmem_limit_bytes=VMEM_LIMIT),
    )(tile_expert, tile_rows, xs, w_exp_gate, w_exp_up, w_exp_down)


def _final_kernel(h_ref, yk_ref, wk_ref, wsgu_ref, wsd_ref, lng_ref, lnb_ref, *rest, alpha):
    out_ref = rest[-1]
    h = h_ref[...]
    a = _dot(h.astype(BF16), wsgu_ref[...])
    act = a[:, :SHARED_FF]
    hid = act * jax.nn.sigmoid(act) * a[:, SHARED_FF:]
    shared = _dot(hid.astype(BF16), wsd_ref[...])
    acc_lo = shared[:, :HALF]
    acc_hi = shared[:, HALF:]
    wk = wk_ref[...]
    for k in range(TOP_K):
        lo, hi = _unpack_rows(yk_ref[k])
        g = wk[:, k:k + 1]
        acc_lo = acc_lo + g * lo
        acc_hi = acc_hi + g * hi
    ffn = jnp.concatenate([acc_lo, acc_hi], axis=1)
    out_ref[...] = _layer_norm(alpha * h + ffn, lng_ref[...], lnb_ref[...])


def _final(h2, yk, wk, w_sgu, w_sd, ln_g, ln_b, alpha, out_prev, row0, total_rows):
    T, D = h2.shape
    tm = FINAL_TILE
    assert T % tm == 0 and row0 % tm == 0
    tile0 = row0 // tm
    operands = [h2, yk, wk, w_sgu, w_sd, ln_g.reshape(1, D), ln_b.reshape(1, D)]
    in_specs = [pl.BlockSpec((tm, D), lambda i: (i, 0)),
                pl.BlockSpec((TOP_K, tm, HALF), lambda i: (0, i, 0)),
                pl.BlockSpec((tm, TOP_K), lambda i: (i, 0)),
                _const_spec((D, 2 * SHARED_FF)), _const_spec((SHARED_FF, D)),
                _const_spec((1, D)), _const_spec((1, D))]
    aliases = {}
    if out_prev is not None:
        aliases = {len(operands): 0}
        operands.append(out_prev)
        in_specs.append(pl.BlockSpec(memory_space=pl.ANY))
    return pl.pallas_call(
        functools.partial(_final_kernel, alpha=alpha),
        name="final",
        grid=(T // tm,),
        in_specs=in_specs,
        out_specs=pl.BlockSpec((tm, D), lambda i: (i + tile0, 0)),
        out_shape=jax.ShapeDtypeStruct((total_rows, D), F32),
        input_output_aliases=aliases,
        compiler_params=pltpu.CompilerParams(
            dimension_semantics=("arbitrary",), vmem_limit_bytes=VMEM_LIMIT),
    )(*operands)


def _route_and_dispatch(h2, words, w_router, router_bias):
    T, D = h2.shape
    R = EXPERT_ROWS
    eid, pos, wk, counts = _router(h2, w_router, router_bias)
    counts = counts.reshape(N_EXPERTS).astype(I32)
    padded = (counts + (R - 1)) // R * R
    ends = jnp.cumsum(padded)
    offsets = ends - padded
    n_rows = T * TOP_K + N_EXPERTS * R
    tile_start = jnp.arange(n_rows // R, dtype=I32) * R
    tile_expert = jnp.sum((tile_start[:, None] >= ends[None, :]).astype(I32), axis=1)
    tile_expert = jnp.minimum(tile_expert, N_EXPERTS - 1)
    tile_rows = jnp.clip(counts[tile_expert] - (tile_start - offsets[tile_expert]), 0, R)
    tile_rows = jnp.where(tile_start < ends[-1], tile_rows, 0).astype(I32)
    slots = _slots(offsets.astype(I32), eid, pos)
    slots = slots.reshape(TOP_K, T // SC_CHUNK, SC_CHUNK).transpose(1, 0, 2)
    xs = _sc_dispatch(words, slots, n_rows)
    return xs, slots, tile_expert, tile_rows, wk


def kernel(x, w_in, w_gate_up, b_gate, gla_norm_g, w_gla_up, w_pool_grp, pool_scale, w_pool_up,
           w_out, ln1_g, ln1_b, w_router, router_bias, w_exp_gate, w_exp_up, w_exp_down,
           w_sh_gate, w_sh_up, w_sh_down, ln2_g, ln2_b):
    B, S, D = x.shape
    depth = w_in.shape[0]
    alpha = (2.0 * depth) ** 0.25
    n_groups = 2 if B % 2 == 0 else 1
    gb = B // n_groups
    gt = gb * S
    h = x
    for l in range(depth):
        mix_ops = _mixer_operands(w_in[l], w_gate_up[l], b_gate[l], gla_norm_g[l], w_gla_up[l],
                                  w_pool_grp[l], pool_scale[l], w_pool_up[l], w_out[l],
                                  ln1_g[l], ln1_b[l])
        w_sgu = jnp.concatenate([w_sh_gate[l], w_sh_up[l]], axis=-1).astype(BF16)
        w_sd = w_sh_down[l].astype(BF16)
        staged = []
        for g in range(n_groups):
            hg, words = _mixer(h, mix_ops, alpha, g * gb, gb)
            hg = hg.reshape(gt, D)
            staged.append((hg,) + _route_and_dispatch(hg, words.reshape(gt, HALF),
                                                      w_router[l], router_bias[l]))
        out = None
        for g, (hg, xs, slots, tile_expert, tile_rows, wk) in enumerate(staged):
            ys = _experts(xs, tile_expert, tile_rows, w_exp_gate[l], w_exp_up[l], w_exp_down[l])
            yk = _sc_collect(ys, slots, gt)
            out = _final(hg, yk, wk, w_sgu, w_sd, ln2_g[l], ln2_b[l], alpha, out, g * gt, B * S)
        h = out.reshape(B, S, D)
    return h
```

```python
import functools

import jax
import jax.numpy as jnp
from jax import lax
from jax.experimental import pallas as pl
from jax.experimental.pallas import tpu as pltpu
from jax.experimental.pallas import tpu_sc as plsc

F32 = jnp.float32
BF16 = jnp.bfloat16
I32 = jnp.int32
U32 = jnp.uint32

D_MODEL = 1024
GLA_HEADS = 4
GLA_DK = D_MODEL // 2
GLA_DV = D_MODEL
HEAD_K = GLA_DK // GLA_HEADS
HEAD_V = GLA_DV // GLA_HEADS
GATE_RANK = 16
GATE_TEMP = 16.0
POOL_WIDTH = D_MODEL // 2
POOL_GROUPS = 4
POOL_GROUP_DIM = POOL_WIDTH // POOL_GROUPS
POOL_WINDOWS = (2, 4, 8, 16)
POOL_HALO = 16
N_EXPERTS = 64
TOP_K = 8
N_GROUPS = 8
TOP_GROUPS = 4
EXPERTS_PER_GROUP = N_EXPERTS // N_GROUPS
EXPERT_FF = 256
SHARED_FF = 256
ROUTE_SCALE = 2.5
LN_EPS = 1e-5
RMS_EPS = 1e-6
LANES = 128

MIX_TILE = 256
ROUTER_TILE = 512
FINAL_TILE = 512
EXPERT_ROWS = 1024
TOKEN_GROUPS = 1
SC_CORES = 2
SC_SUBCORES = 16
SC_CHUNK = 64
HALF = D_MODEL // 2
VMEM_LIMIT = 56 * 1024 * 1024

_dot = functools.partial(jnp.dot, preferred_element_type=F32)
_NT = (((1,), (1,)), ((), ()))
_TN = (((0,), (0,)), ((), ()))


def _layer_norm(y, g, b):
    mu = jnp.mean(y, axis=-1, keepdims=True)
    yc = y - mu
    var = jnp.mean(yc * yc, axis=-1, keepdims=True)
    return yc * lax.rsqrt(var + LN_EPS) * g + b


def _split3(a):
    hi = a.astype(BF16)
    r1 = a - hi.astype(F32)
    mid = r1.astype(BF16)
    lo = (r1 - mid.astype(F32)).astype(BF16)
    return hi, mid, lo


def _pack_rows(y):
    lo = lax.bitcast_convert_type(y[:, :HALF].astype(BF16).astype(F32), U32)
    hi = lax.bitcast_convert_type(y[:, HALF:].astype(BF16).astype(F32), U32)
    return lax.bitcast_convert_type(hi | (lo >> 16), I32)


def _unpack_rows(w):
    u = lax.bitcast_convert_type(w, U32)
    lo = lax.bitcast_convert_type(u << 16, F32)
    hi = lax.bitcast_convert_type(u & jnp.uint32(0xFFFF0000), F32)
    return lo, hi


def _mixer_kernel(x_ref, wqk_ref, wv_ref, wr_ref, wglr_ref, wgup_ref, bgate_ref, wu_ref,
                  wga_ref, wgb_ref, gnorm_ref, wglaup_ref, wpgrp_ref, pscale_ref, wpup_ref,
                  wout_ref, lng_ref, lnb_ref, out_ref, words_ref, state_ref, ucat_ref,
                  *, alpha, tm):
    j = pl.program_id(1)

    @pl.when(j == 0)
    def _():
        state_ref[...] = jnp.zeros_like(state_ref)
        ucat_ref[0:POOL_HALO, :] = jnp.zeros((POOL_HALO, POOL_WIDTH), F32)

    x = x_ref[0]
    xb = x.astype(BF16)

    qk = _dot(xb, wqk_ref[...])
    v = _dot(xb, wv_ref[...])
    g_lr = _dot(xb, wglr_ref[...])
    z = _dot(g_lr.astype(BF16), wgup_ref[...]) + bgate_ref[...]
    log_decay = (jnp.minimum(z, 0.0) - jnp.log1p(jnp.exp(-jnp.abs(z)))) * (1.0 / GATE_TEMP)

    row = lax.broadcasted_iota(I32, (tm, tm), 0)
    col = lax.broadcasted_iota(I32, (tm, tm), 1)
    causal = row >= col
    tri = causal.astype(BF16)
    g_hi, g_mid, g_lo = _split3(log_decay)
    bcum = _dot(tri, g_hi) + _dot(tri, g_mid) + _dot(tri, g_lo)

    gnorm = gnorm_ref[...]
    heads = []
    for h in range(GLA_HEADS):
        ks = slice(h * HEAD_K, (h + 1) * HEAD_K)
        b = bcum[:, ks]
        b_last = b[tm - 1:tm, :]
        q_h = qk[:, ks] * (HEAD_K ** -0.5)
        k_h = qk[:, GLA_DK + h * HEAD_K:GLA_DK + (h + 1) * HEAD_K]
        v_h = v[:, h * HEAD_V:(h + 1) * HEAD_V].astype(BF16)
        q_in = (q_h * jnp.exp(b)).astype(BF16)
        k_out = (k_h * jnp.exp(-b)).astype(BF16)
        k_end = (k_h * jnp.exp(b_last - b)).astype(BF16)
        scores = lax.dot_general(q_in, k_out, _NT, preferred_element_type=F32)
        scores = jnp.where(causal, scores, 0.0).astype(BF16)
        st = state_ref[h]
        o = _dot(scores, v_h) + lax.dot_general(q_in, st.astype(BF16), _NT,
                                                preferred_element_type=F32)
        state_ref[h] = st * jnp.exp(b_last) + lax.dot_general(
            v_h, k_end, _TN, preferred_element_type=F32)
        ms = jnp.mean(o * o, axis=-1, keepdims=True)
        heads.append(o * lax.rsqrt(ms + RMS_EPS) * gnorm)
    o = jnp.concatenate(heads, axis=1)
    r = _dot(xb, wr_ref[...])
    o = o * (r * jax.nn.sigmoid(r))
    y_gla = _dot(o.astype(BF16), wglaup_ref[...])

    u = _dot(xb, wu_ref[...])
    ucat_ref[POOL_HALO:, :] = u
    pos = j * tm - POOL_HALO + lax.broadcasted_iota(I32, (tm + POOL_HALO, 1), 0)
    mixed = []
    for gi, window in enumerate(POOL_WINDOWS):
        a = ucat_ref[:, gi * POOL_GROUP_DIM:(gi + 1) * POOL_GROUP_DIM]
        s = a
        step = 1
        while step < window:
            s = s + pltpu.roll(s, step, 0)
            step *= 2
        count = jnp.clip(pos + 1, 1, window).astype(F32)
        pooled = (s / count - a)[POOL_HALO:, :]
        mixed.append(_dot(pooled.astype(BF16), wpgrp_ref[gi]))
    ucat_ref[0:POOL_HALO, :] = u[tm - POOL_HALO:, :]
    mixed = jnp.concatenate(mixed, axis=1) * pscale_ref[...]
    y_pool = _dot(mixed.astype(BF16), wpup_ref[...])

    gate_a = _dot(xb, wga_ref[...])
    gate_b = _dot(xb, wgb_ref[...])
    merged = jax.nn.sigmoid(gate_a) * y_gla + jax.nn.sigmoid(gate_b) * y_pool
    mix = _dot(merged.astype(BF16), wout_ref[...])
    y = _layer_norm(alpha * x + mix, lng_ref[...], lnb_ref[...])
    out_ref[0] = y
    words_ref[0] = _pack_rows(y)


def _const_spec(shape):
    zeros = (0,) * len(shape)
    return pl.BlockSpec(shape, lambda *_: zeros)


def _mixer_operands(w_in, w_gate_up, b_gate, gla_norm_g, w_gla_up, w_pool_grp, pool_scale,
                    w_pool_up, w_out, ln_g, ln_b):
    D = D_MODEL
    c = 0
    w_qk = w_in[:, c:c + 2 * GLA_DK].astype(BF16); c += 2 * GLA_DK
    w_v = w_in[:, c:c + GLA_DV].astype(BF16); c += GLA_DV
    w_r = w_in[:, c:c + GLA_DV].astype(BF16); c += GLA_DV
    w_glr = jnp.pad(w_in[:, c:c + GATE_RANK], ((0, 0), (0, LANES - GATE_RANK))).astype(BF16); c += GATE_RANK
    w_u = w_in[:, c:c + POOL_WIDTH].astype(BF16); c += POOL_WIDTH
    w_ga = w_in[:, c:c + D].astype(BF16); c += D
    w_gb = w_in[:, c:c + D].astype(BF16); c += D
    assert c == w_in.shape[1]
    w_gup = jnp.pad(w_gate_up, ((0, LANES - GATE_RANK), (0, 0))).astype(BF16)
    return (
        w_qk, w_v, w_r, w_glr, w_gup, b_gate.reshape(1, GLA_DK), w_u, w_ga, w_gb,
        gla_norm_g.reshape(1, HEAD_V), w_gla_up.astype(BF16), w_pool_grp.astype(BF16),
        pool_scale.reshape(1, POOL_WIDTH), w_pool_up.astype(BF16), w_out.astype(BF16),
        ln_g.reshape(1, D), ln_b.reshape(1, D))


def _mixer(x, operands, alpha, batch0, n_batch):
    _, S, D = x.shape
    tm = MIX_TILE
    assert D == D_MODEL and S % tm == 0
    return pl.pallas_call(
        functools.partial(_mixer_kernel, alpha=alpha, tm=tm),
        name="mixer",
        grid=(n_batch, S // tm),
        in_specs=[pl.BlockSpec((1, tm, D), lambda b, j: (b + batch0, j, 0))]
        + [_const_spec(w.shape) for w in operands],
        out_specs=[pl.BlockSpec((1, tm, D), lambda b, j: (b, j, 0)),
                   pl.BlockSpec((1, tm, HALF), lambda b, j: (b, j, 0))],
        out_shape=[jax.ShapeDtypeStruct((n_batch, S, D), F32),
                   jax.ShapeDtypeStruct((n_batch, S, HALF), I32)],
        scratch_shapes=[
            pltpu.VMEM((GLA_HEADS, HEAD_V, HEAD_K), F32),
            pltpu.VMEM((tm + POOL_HALO, POOL_WIDTH), F32),
        ],
        compiler_params=pltpu.CompilerParams(
            dimension_semantics=("arbitrary", "arbitrary"), vmem_limit_bytes=VMEM_LIMIT),
    )(x, *operands)


def _rank_desc(vals, n):
    idx = lax.broadcasted_iota(I32, vals.shape, 0)
    rank = jnp.zeros(vals.shape, I32)
    for other in range(n):
        o = vals[other:other + 1, :]
        before = (o > vals) | ((o == vals) & (other < idx))
        rank = rank + before.astype(I32)
    return rank


def _router_kernel(h_ref, wrt_ref, bias_ref, eid_ref, pos_ref, wk_ref, cnt_ref, carry_ref):
    i = pl.program_id(0)

    @pl.when(i == 0)
    def _():
        carry_ref[...] = jnp.zeros_like(carry_ref)

    h = h_ref[...]
    tr = h.shape[0]
    w_hi, w_mid, _ = _split3(wrt_ref[...])
    h_hi, h_mid, _ = _split3(h)
    by_hi = lax.dot_general(jnp.concatenate([w_hi, w_mid], axis=0), h_hi, _NT,
                            preferred_element_type=F32)
    logits = (by_hi[:N_EXPERTS] + by_hi[N_EXPERTS:]
              + lax.dot_general(w_hi, h_mid, _NT, preferred_element_type=F32))
    scores = jax.nn.sigmoid(logits)
    biased = scores + bias_ref[...]
    grp = biased.reshape(N_GROUPS, EXPERTS_PER_GROUP, tr)
    slot = lax.broadcasted_iota(I32, grp.shape, 1)
    top1 = jnp.max(grp, axis=1, keepdims=True)
    first = jnp.min(jnp.where(grp == top1, slot, EXPERTS_PER_GROUP), axis=1, keepdims=True)
    top2 = jnp.max(jnp.where(slot == first, -jnp.inf, grp), axis=1)
    grp_score = top1[:, 0, :] + top2
    grp_on = _rank_desc(grp_score, N_GROUPS) < TOP_GROUPS
    emask = jnp.broadcast_to(grp_on[:, None, :], grp.shape).reshape(N_EXPERTS, tr)
    masked = jnp.where(emask, biased, -jnp.inf)

    eidx = lax.broadcasted_iota(I32, (N_EXPERTS, tr), 0).astype(F32)
    rest = masked
    eids, sels = [], []
    for _ in range(TOP_K):
        top = jnp.max(rest, axis=0, keepdims=True)
        eid = jnp.min(jnp.where(rest == top, eidx, float(N_EXPERTS)), axis=0, keepdims=True)
        hit = eidx == eid
        rest = jnp.where(hit, -jnp.inf, rest)
        eids.append(eid)
        sels.append(jnp.sum(jnp.where(hit, scores, 0.0), axis=0, keepdims=True))
    eid_k = jnp.concatenate(eids, axis=0)
    sel_k = jnp.concatenate(sels, axis=0)
    gate_k = sel_k / jnp.sum(sel_k, axis=0, keepdims=True) * ROUTE_SCALE
    chosen = rest != masked

    t_row = lax.broadcasted_iota(I32, (tr, tr), 0)
    t_col = lax.broadcasted_iota(I32, (tr, tr), 1)
    earlier = (t_row < t_col).astype(BF16)
    pos = carry_ref[...] + _dot(chosen.astype(BF16), earlier)
    carry_ref[...] += jnp.sum(chosen.astype(F32), axis=1, keepdims=True)
    cnt_ref[...] = carry_ref[...]

    poss = [jnp.sum(jnp.where(eidx == eid, pos, 0.0), axis=0, keepdims=True) for eid in eids]
    eid_ref[...] = eid_k.astype(I32)
    pos_ref[...] = jnp.concatenate(poss, axis=0).astype(I32)
    wk_ref[...] = gate_k.T


def _router(h2, w_router, router_bias):
    T, D = h2.shape
    tr = ROUTER_TILE
    assert T % tr == 0
    return pl.pallas_call(
        _router_kernel,
        name="router",
        grid=(T // tr,),
        in_specs=[pl.BlockSpec((tr, D), lambda i: (i, 0)),
                  _const_spec((N_EXPERTS, D)), _const_spec((N_EXPERTS, 1))],
        out_specs=[pl.BlockSpec((TOP_K, tr), lambda i: (0, i)),
                   pl.BlockSpec((TOP_K, tr), lambda i: (0, i)),
                   pl.BlockSpec((tr, TOP_K), lambda i: (i, 0)),
                   _const_spec((N_EXPERTS, 1))],
        out_shape=[jax.ShapeDtypeStruct((TOP_K, T), I32),
                   jax.ShapeDtypeStruct((TOP_K, T), I32),
                   jax.ShapeDtypeStruct((T, TOP_K), F32),
                   jax.ShapeDtypeStruct((N_EXPERTS, 1), F32)],
        scratch_shapes=[pltpu.VMEM((N_EXPERTS, 1), F32)],
        compiler_params=pltpu.CompilerParams(
            dimension_semantics=("arbitrary",), vmem_limit_bytes=VMEM_LIMIT),
    )(h2, w_router.T, router_bias.reshape(N_EXPERTS, 1))


def _slot_kernel(offs_ref, eid_ref, pos_ref, slot_ref):
    eid = eid_ref[...]
    base = jnp.zeros(eid.shape, I32)
    for e in range(N_EXPERTS):
        base = jnp.where(eid == e, offs_ref[e], base)
    slot_ref[...] = pos_ref[...] + base


def _slots(offsets, eid, pos):
    shape = eid.shape
    return pl.pallas_call(
        _slot_kernel,
        name="slots",
        grid_spec=pltpu.PrefetchScalarGridSpec(
            num_scalar_prefetch=1, grid=(1,),
            in_specs=[pl.BlockSpec(shape, lambda i, offs: (0, 0)),
                      pl.BlockSpec(shape, lambda i, offs: (0, 0))],
            out_specs=pl.BlockSpec(shape, lambda i, offs: (0, 0))),
        out_shape=jax.ShapeDtypeStruct(shape, I32),
        compiler_params=pltpu.CompilerParams(vmem_limit_bytes=VMEM_LIMIT),
    )(offsets, eid, pos)


def _sc_mesh():
    return plsc.VectorSubcoreMesh(core_axis_name="core", subcore_axis_name="subcore",
                                  num_cores=SC_CORES, num_subcores=SC_SUBCORES)


def _sc_worker():
    return lax.axis_index("subcore") * SC_CORES + lax.axis_index("core")


def _sc_dispatch(words, slots, n_rows):
    T = words.shape[0]
    n_chunks = slots.shape[0]
    workers = SC_CORES * SC_SUBCORES
    assert T == n_chunks * SC_CHUNK and n_chunks % workers == 0
    per_worker = n_chunks // workers

    @functools.partial(
        pl.kernel, mesh=_sc_mesh(),
        out_type=jax.ShapeDtypeStruct((n_rows, HALF), I32),
        scratch_types=[pltpu.VMEM((SC_CHUNK, HALF), I32), pltpu.VMEM((TOP_K, SC_CHUNK), I32),
                       pltpu.SemaphoreType.DMA])
    def dispatch(words_hbm, slots_hbm, out_hbm, rows_v, idx_v, sem):
        first = _sc_worker() * per_worker

        @pl.loop(0, per_worker)
        def _(i):
            c = first + i
            pltpu.sync_copy(slots_hbm.at[c], idx_v)
            pltpu.sync_copy(words_hbm.at[pl.ds(c * SC_CHUNK, SC_CHUNK)], rows_v)
            copies = [pltpu.async_copy(rows_v, out_hbm.at[idx_v.at[k]], sem) for k in range(TOP_K)]
            for cp in copies:
                cp.wait()

    return dispatch(words, slots)


def _sc_collect(rows, slots, T):
    n_chunks = slots.shape[0]
    workers = SC_CORES * SC_SUBCORES
    assert T == n_chunks * SC_CHUNK and n_chunks % workers == 0
    per_worker = n_chunks // workers

    @functools.partial(
        pl.kernel, mesh=_sc_mesh(),
        out_type=jax.ShapeDtypeStruct((TOP_K, T, HALF), I32),
        scratch_types=[pltpu.VMEM((2, SC_CHUNK, HALF), I32), pltpu.VMEM((TOP_K, SC_CHUNK), I32),
                       pltpu.SemaphoreType.DMA, pltpu.SemaphoreType.DMA((2,))])
    def collect(rows_hbm, slots_hbm, out_hbm, rows_v, idx_v, sem_in, sem_out):
        first = _sc_worker() * per_worker

        @pl.loop(0, per_worker)
        def _(i):
            c = first + i
            pltpu.sync_copy(slots_hbm.at[c], idx_v)
            writes = []
            for k in range(TOP_K):
                buf = rows_v.at[k % 2]
                if k >= 2:
                    writes[k - 2].wait()
                pltpu.async_copy(rows_hbm.at[idx_v.at[k]], buf, sem_in).wait()
                writes.append(pltpu.async_copy(buf, out_hbm.at[k, pl.ds(c * SC_CHUNK, SC_CHUNK)],
                                               sem_out.at[k % 2]))
            writes[TOP_K - 2].wait()
            writes[TOP_K - 1].wait()

    return collect(rows, slots)


def _expert_kernel(tile_expert_ref, tile_rows_ref, xs_ref, wg_ref, wu_ref, wd_ref, ys_ref,
                   wgu_ref, wdn_ref):
    i = pl.program_id(0)
    e = tile_expert_ref[i]
    n_valid = tile_rows_ref[i]
    e_prev = tile_expert_ref[jnp.maximum(i - 1, 0)]

    @pl.when((i == 0) | (e != e_prev))
    def _():
        wgu_ref[:, :EXPERT_FF] = wg_ref[0].astype(BF16)
        wgu_ref[:, EXPERT_FF:] = wu_ref[0].astype(BF16)
        wdn_ref[...] = wd_ref[0].astype(BF16)

    @pl.when(n_valid > 0)
    def _():
        words = xs_ref[...]
        live = lax.broadcasted_iota(I32, (words.shape[0], 1), 0) < n_valid
        lo, hi = _unpack_rows(jnp.where(live, words, 0))
        a = _dot(lo.astype(BF16), wgu_ref[:HALF, :]) + _dot(hi.astype(BF16), wgu_ref[HALF:, :])
        act = a[:, :EXPERT_FF]
        hid = act * jax.nn.sigmoid(act) * a[:, EXPERT_FF:]
        ys_ref[...] = _pack_rows(_dot(hid.astype(BF16), wdn_ref[...]))

    @pl.when(n_valid == 0)
    def _():
        ys_ref[...] = jnp.zeros_like(ys_ref)


def _experts(xs, tile_expert, tile_rows, w_exp_gate, w_exp_up, w_exp_down):
    n_rows = xs.shape[0]
    R = EXPERT_ROWS
    D = D_MODEL
    assert n_rows % R == 0
    return pl.pallas_call(
        _expert_kernel,
        name="experts",
        grid_spec=pltpu.PrefetchScalarGridSpec(
            num_scalar_prefetch=2, grid=(n_rows // R,),
            in_specs=[pl.BlockSpec((R, HALF), lambda i, te, tr: (i, 0)),
                      pl.BlockSpec((1, D, EXPERT_FF), lambda i, te, tr: (te[i], 0, 0)),
                      pl.BlockSpec((1, D, EXPERT_FF), lambda i, te, tr: (te[i], 0, 0)),
                      pl.BlockSpec((1, EXPERT_FF, D), lambda i, te, tr: (te[i], 0, 0))],
            out_specs=pl.BlockSpec((R, HALF), lambda i, te, tr: (i, 0)),
            scratch_shapes=[pltpu.VMEM((D, 2 * EXPERT_FF), BF16), pltpu.VMEM((EXPERT_FF, D), BF16)]),
        out_shape=jax.ShapeDtypeStruct((n_rows, HALF), I32),
        compiler_params=pltpu.CompilerParams(
            dimension_semantics=("arbitrary",), vmem_limit_bytes=VMEM_LIMIT),
    )(tile_expert, tile_rows, xs, w_exp_gate, w_exp_up, w_exp_down)


def _final_kernel(h_ref, yk_ref, wk_ref, wsgu_ref, wsd_ref, lng_ref, lnb_ref, *rest, alpha):
    out_ref = rest[-1]
    h = h_ref[...]
    a = _dot(h.astype(BF16), wsgu_ref[...])
    act = a[:, :SHARED_FF]
    hid = act * jax.nn.sigmoid(act) * a[:, SHARED_FF:]
    shared = _dot(hid.astype(BF16), wsd_ref[...])
    acc_lo = shared[:, :HALF]
    acc_hi = shared[:, HALF:]
    wk = wk_ref[...]
    for k in range(TOP_K):
        lo, hi = _unpack_rows(yk_ref[k])
        g = wk[:, k:k + 1]
        acc_lo = acc_lo + g * lo
        acc_hi = acc_hi + g * hi
    ffn = jnp.concatenate([acc_lo, acc_hi], axis=1)
    out_ref[...] = _layer_norm(alpha * h + ffn, lng_ref[...], lnb_ref[...])


def _final(h2, yk, wk, w_sgu, w_sd, ln_g, ln_b, alpha, out_prev, row0, total_rows):
    T, D = h2.shape
    tm = FINAL_TILE
    assert T % tm == 0 and row0 % tm == 0
    tile0 = row0 // tm
    operands = [h2, yk, wk, w_sgu, w_sd, ln_g.reshape(1, D), ln_b.reshape(1, D)]
    in_specs = [pl.BlockSpec((tm, D), lambda i: (i, 0)),
                pl.BlockSpec((TOP_K, tm, HALF), lambda i: (0, i, 0)),
                pl.BlockSpec((tm, TOP_K), lambda i: (i, 0)),
                _const_spec((D, 2 * SHARED_FF)), _const_spec((SHARED_FF, D)),
                _const_spec((1, D)), _const_spec((1, D))]
    aliases = {}
    if out_prev is not None:
        aliases = {len(operands): 0}
        operands.append(out_prev)
        in_specs.append(pl.BlockSpec(memory_space=pl.ANY))
    return pl.pallas_call(
        functools.partial(_final_kernel, alpha=alpha),
        name="final",
        grid=(T // tm,),
        in_specs=in_specs,
        out_specs=pl.BlockSpec((tm, D), lambda i: (i + tile0, 0)),
        out_shape=jax.ShapeDtypeStruct((total_rows, D), F32),
        input_output_aliases=aliases,
        compiler_params=pltpu.CompilerParams(
            dimension_semantics=("arbitrary",), vmem_limit_bytes=VMEM_LIMIT),
    )(*operands)


def _route_and_dispatch(h2, words, w_router, router_bias):
    T, D = h2.shape
    R = EXPERT_ROWS
    eid, pos, wk, counts = _router(h2, w_router, router_bias)
    counts = counts.reshape(N_EXPERTS).astype(I32)
    padded = (counts + (R - 1)) // R * R
    ends = jnp.cumsum(padded)
    offsets = ends - padded
    n_rows = T * TOP_K + N_EXPERTS * R
    tile_start = jnp.arange(n_rows // R, dtype=I32) * R
    past = (tile_start[:, None] >= ends[None, :]).astype(I32)
    tile_expert = jnp.minimum(jnp.sum(past, axis=1), N_EXPERTS - 1)
    mine = (tile_expert[:, None] == jnp.arange(N_EXPERTS, dtype=I32)[None, :]).astype(I32)
    valid_end = jnp.sum(mine * (offsets + counts)[None, :], axis=1)
    tile_rows = jnp.clip(valid_end - tile_start, 0, R).astype(I32)
    slots = _slots(offsets.astype(I32), eid, pos)
    slots = slots.reshape(TOP_K, T // SC_CHUNK, SC_CHUNK).transpose(1, 0, 2)
    xs = _sc_dispatch(words, slots, n_rows)
    return xs, slots, tile_expert, tile_rows, wk


def kernel(x, w_in, w_gate_up, b_gate, gla_norm_g, w_gla_up, w_pool_grp, pool_scale, w_pool_up,
           w_out, ln1_g, ln1_b, w_router, router_bias, w_exp_gate, w_exp_up, w_exp_down,
           w_sh_gate, w_sh_up, w_sh_down, ln2_g, ln2_b):
    B, S, D = x.shape
    depth = w_in.shape[0]
    alpha = (2.0 * depth) ** 0.25
    n_groups = TOKEN_GROUPS if B % TOKEN_GROUPS == 0 else 1
    gb = B // n_groups
    gt = gb * S
    h = x
    for l in range(depth):
        mix_ops = _mixer_operands(w_in[l], w_gate_up[l], b_gate[l], gla_norm_g[l], w_gla_up[l],
                                  w_pool_grp[l], pool_scale[l], w_pool_up[l], w_out[l],
                                  ln1_g[l], ln1_b[l])
        w_sgu = jnp.concatenate([w_sh_gate[l], w_sh_up[l]], axis=-1).astype(BF16)
        w_sd = w_sh_down[l].astype(BF16)
        staged = []
        for g in range(n_groups):
            hg, words = _mixer(h, mix_ops, alpha, g * gb, gb)
            hg = hg.reshape(gt, D)
            staged.append((hg,) + _route_and_dispatch(hg, words.reshape(gt, HALF),
                                                      w_router[l], router_bias[l]))
        out = None
        for g, (hg, xs, slots, tile_expert, tile_rows, wk) in enumerate(staged):
            ys = _experts(xs, tile_expert, tile_rows, w_exp_gate[l], w_exp_up[l], w_exp_down[l])
            yk = _sc_collect(ys, slots, gt)
            out = _final(hg, yk, wk, w_sgu, w_sd, ln2_g[l], ln2_b[l], alpha, out, g * gt, B * S)
        h = out.reshape(B, S, D)
    return h
```

```python
import functools

import jax
import jax.numpy as jnp
from jax import lax
from jax.experimental import pallas as pl
from jax.experimental.pallas import tpu as pltpu
from jax.experimental.pallas import tpu_sc as plsc

F32 = jnp.float32
BF16 = jnp.bfloat16
I32 = jnp.int32
U32 = jnp.uint32

D_MODEL = 1024
GLA_HEADS = 4
GLA_DK = D_MODEL // 2
GLA_DV = D_MODEL
HEAD_K = GLA_DK // GLA_HEADS
HEAD_V = GLA_DV // GLA_HEADS
GATE_RANK = 16
GATE_TEMP = 16.0
POOL_WIDTH = D_MODEL // 2
POOL_GROUPS = 4
POOL_GROUP_DIM = POOL_WIDTH // POOL_GROUPS
POOL_WINDOWS = (2, 4, 8, 16)
POOL_HALO = 16
N_EXPERTS = 64
TOP_K = 8
N_GROUPS = 8
TOP_GROUPS = 4
EXPERTS_PER_GROUP = N_EXPERTS // N_GROUPS
EXPERT_FF = 256
SHARED_FF = 256
ROUTE_SCALE = 2.5
LN_EPS = 1e-5
RMS_EPS = 1e-6
LANES = 128

GLA_CHUNK = 256
MIX_TILE = 512
ROUTER_TILE = 512
FINAL_TILE = 512
EXPERT_ROWS = 1024
TOKEN_GROUPS = 1
SC_CORES = 2
SC_SUBCORES = 16
SC_CHUNK = 64
HALF = D_MODEL // 2
VMEM_LIMIT = 56 * 1024 * 1024

_dot = functools.partial(jnp.dot, preferred_element_type=F32)
_NT = (((1,), (1,)), ((), ()))
_TN = (((0,), (0,)), ((), ()))


def _layer_norm(y, g, b):
    mu = jnp.mean(y, axis=-1, keepdims=True)
    yc = y - mu
    var = jnp.mean(yc * yc, axis=-1, keepdims=True)
    return yc * lax.rsqrt(var + LN_EPS) * g + b


def _split3(a):
    hi = a.astype(BF16)
    r1 = a - hi.astype(F32)
    mid = r1.astype(BF16)
    lo = (r1 - mid.astype(F32)).astype(BF16)
    return hi, mid, lo


def _pack_rows(y):
    lo = lax.bitcast_convert_type(y[:, :HALF].astype(BF16).astype(F32), U32)
    hi = lax.bitcast_convert_type(y[:, HALF:].astype(BF16).astype(F32), U32)
    return lax.bitcast_convert_type(hi | (lo >> 16), I32)


def _unpack_rows(w):
    u = lax.bitcast_convert_type(w, U32)
    lo = lax.bitcast_convert_type(u << 16, F32)
    hi = lax.bitcast_convert_type(u & jnp.uint32(0xFFFF0000), F32)
    return lo, hi


def _mixer_kernel(x_ref, wqk_ref, wv_ref, wr_ref, wglr_ref, wgup_ref, bgate_ref, wu_ref,
                  wga_ref, wgb_ref, gnorm_ref, wglaup_ref, wpgrp_ref, pscale_ref, wpup_ref,
                  wout_ref, lng_ref, lnb_ref, out_ref, words_ref, state_ref, ucat_ref,
                  *, alpha, tm):
    j = pl.program_id(1)

    @pl.when(j == 0)
    def _():
        state_ref[...] = jnp.zeros_like(state_ref)
        ucat_ref[0:POOL_HALO, :] = jnp.zeros((POOL_HALO, POOL_WIDTH), F32)

    C = GLA_CHUNK
    causal = lax.broadcasted_iota(I32, (C, C), 0) >= lax.broadcasted_iota(I32, (C, C), 1)
    tri = causal.astype(BF16)
    gnorm = gnorm_ref[...]
    states = [state_ref[h] for h in range(GLA_HEADS)]

    for c in range(tm // C):
        rows = pl.ds(c * C, C)
        x = x_ref[0, rows, :]
        xb = x.astype(BF16)

        qk = _dot(xb, wqk_ref[...])
        v = _dot(xb, wv_ref[...])
        g_lr = _dot(xb, wglr_ref[...])
        z = _dot(g_lr.astype(BF16), wgup_ref[...]) + bgate_ref[...]
        u = _dot(xb, wu_ref[...])
        r = _dot(xb, wr_ref[...])
        log_decay = (jnp.minimum(z, 0.0) - jnp.log1p(jnp.exp(-jnp.abs(z)))) * (1.0 / GATE_TEMP)
        g_hi, g_mid, g_lo = _split3(log_decay)
        gate_a = _dot(xb, wga_ref[...])
        bcum = _dot(tri, g_hi) + _dot(tri, g_mid) + _dot(tri, g_lo)

        ucat_ref[POOL_HALO:, :] = u
        pos = j * tm + c * C - POOL_HALO + lax.broadcasted_iota(I32, (C + POOL_HALO, 1), 0)

        def pool_group(gi):
            window = POOL_WINDOWS[gi]
            a = ucat_ref[:, gi * POOL_GROUP_DIM:(gi + 1) * POOL_GROUP_DIM]
            s = a
            step = 1
            while step < window:
                s = s + pltpu.roll(s, step, 0)
                step *= 2
            count = jnp.clip(pos + 1, 1, window).astype(F32)
            pooled = (s / count - a)[POOL_HALO:, :]
            return _dot(pooled.astype(BF16), wpgrp_ref[gi])

        heads, mixed = [], []
        for h in range(GLA_HEADS):
            ks = slice(h * HEAD_K, (h + 1) * HEAD_K)
            b = bcum[:, ks]
            b_last = b[C - 1:C, :]
            q_h = qk[:, ks] * (HEAD_K ** -0.5)
            k_h = qk[:, GLA_DK + h * HEAD_K:GLA_DK + (h + 1) * HEAD_K]
            v_h = v[:, h * HEAD_V:(h + 1) * HEAD_V].astype(BF16)
            q_in = (q_h * jnp.exp(b)).astype(BF16)
            k_out = (k_h * jnp.exp(-b)).astype(BF16)
            k_end = (k_h * jnp.exp(b_last - b)).astype(BF16)
            scores = lax.dot_general(q_in, k_out, _NT, preferred_element_type=F32)
            scores = jnp.where(causal, scores, 0.0).astype(BF16)
            st = states[h]
            o = _dot(scores, v_h) + lax.dot_general(q_in, st.astype(BF16), _NT,
                                                    preferred_element_type=F32)
            states[h] = st * jnp.exp(b_last) + lax.dot_general(
                v_h, k_end, _TN, preferred_element_type=F32)
            ms = jnp.mean(o * o, axis=-1, keepdims=True)
            heads.append(o * lax.rsqrt(ms + RMS_EPS) * gnorm)
            mixed.append(pool_group(h))
            if h == 1:
                gate_b = _dot(xb, wgb_ref[...])
        ucat_ref[0:POOL_HALO, :] = u[C - POOL_HALO:, :]
        mixed = jnp.concatenate(mixed, axis=1) * pscale_ref[...]
        y_pool = _dot(mixed.astype(BF16), wpup_ref[...])
        o = jnp.concatenate(heads, axis=1) * (r * jax.nn.sigmoid(r))
        y_gla = _dot(o.astype(BF16), wglaup_ref[...])

        merged = jax.nn.sigmoid(gate_a) * y_gla + jax.nn.sigmoid(gate_b) * y_pool
        mix = _dot(merged.astype(BF16), wout_ref[...])
        y = _layer_norm(alpha * x + mix, lng_ref[...], lnb_ref[...])
        out_ref[0, rows, :] = y
        words_ref[0, rows, :] = _pack_rows(y)

    for h in range(GLA_HEADS):
        state_ref[h] = states[h]


def _const_spec(shape):
    zeros = (0,) * len(shape)
    return pl.BlockSpec(shape, lambda *_: zeros, pipeline_mode=pl.Buffered(1))


def _mixer_operands(w_in, w_gate_up, b_gate, gla_norm_g, w_gla_up, w_pool_grp, pool_scale,
                    w_pool_up, w_out, ln_g, ln_b):
    D = D_MODEL
    c = 0
    w_qk = w_in[:, c:c + 2 * GLA_DK].astype(BF16); c += 2 * GLA_DK
    w_v = w_in[:, c:c + GLA_DV].astype(BF16); c += GLA_DV
    w_r = w_in[:, c:c + GLA_DV].astype(BF16); c += GLA_DV
    w_glr = jnp.pad(w_in[:, c:c + GATE_RANK], ((0, 0), (0, LANES - GATE_RANK))).astype(BF16); c += GATE_RANK
    w_u = w_in[:, c:c + POOL_WIDTH].astype(BF16); c += POOL_WIDTH
    w_ga = w_in[:, c:c + D].astype(BF16); c += D
    w_gb = w_in[:, c:c + D].astype(BF16); c += D
    assert c == w_in.shape[1]
    w_gup = jnp.pad(w_gate_up, ((0, LANES - GATE_RANK), (0, 0))).astype(BF16)
    return (
        w_qk, w_v, w_r, w_glr, w_gup, b_gate.reshape(1, GLA_DK), w_u, w_ga, w_gb,
        gla_norm_g.reshape(1, HEAD_V), w_gla_up.astype(BF16), w_pool_grp.astype(BF16),
        pool_scale.reshape(1, POOL_WIDTH), w_pool_up.astype(BF16), w_out.astype(BF16),
        ln_g.reshape(1, D), ln_b.reshape(1, D))


def _mixer(x, operands, alpha, batch0, n_batch):
    _, S, D = x.shape
    tm = MIX_TILE
    assert D == D_MODEL and S % tm == 0 and tm % GLA_CHUNK == 0
    return pl.pallas_call(
        functools.partial(_mixer_kernel, alpha=alpha, tm=tm),
        name="mixer",
        grid=(n_batch, S // tm),
        in_specs=[pl.BlockSpec((1, tm, D), lambda b, j: (b + batch0, j, 0))]
        + [_const_spec(w.shape) for w in operands],
        out_specs=[pl.BlockSpec((1, tm, D), lambda b, j: (b, j, 0)),
                   pl.BlockSpec((1, tm, HALF), lambda b, j: (b, j, 0))],
        out_shape=[jax.ShapeDtypeStruct((n_batch, S, D), F32),
                   jax.ShapeDtypeStruct((n_batch, S, HALF), I32)],
        scratch_shapes=[
            pltpu.VMEM((GLA_HEADS, HEAD_V, HEAD_K), F32),
            pltpu.VMEM((GLA_CHUNK + POOL_HALO, POOL_WIDTH), F32),
        ],
        compiler_params=pltpu.CompilerParams(
            dimension_semantics=("arbitrary", "arbitrary"), vmem_limit_bytes=VMEM_LIMIT),
    )(x, *operands)


def _rank_desc(vals, n):
    idx = lax.broadcasted_iota(I32, vals.shape, 0)
    rank = jnp.zeros(vals.shape, I32)
    for other in range(n):
        o = vals[other:other + 1, :]
        before = (o > vals) | ((o == vals) & (other < idx))
        rank = rank + before.astype(I32)
    return rank


def _router_kernel(h_ref, wrt_ref, bias_ref, eid_ref, pos_ref, wk_ref, cnt_ref, carry_ref):
    i = pl.program_id(0)

    @pl.when(i == 0)
    def _():
        carry_ref[...] = jnp.zeros_like(carry_ref)

    h = h_ref[...]
    tr = h.shape[0]
    w_hi, w_mid, _ = _split3(wrt_ref[...])
    h_hi, h_mid, _ = _split3(h)
    by_hi = lax.dot_general(jnp.concatenate([w_hi, w_mid], axis=0), h_hi, _NT,
                            preferred_element_type=F32)
    logits = (by_hi[:N_EXPERTS] + by_hi[N_EXPERTS:]
              + lax.dot_general(w_hi, h_mid, _NT, preferred_element_type=F32))
    scores = jax.nn.sigmoid(logits)
    biased = scores + bias_ref[...]
    grp = biased.reshape(N_GROUPS, EXPERTS_PER_GROUP, tr)
    slot = lax.broadcasted_iota(I32, grp.shape, 1)
    top1 = jnp.max(grp, axis=1, keepdims=True)
    first = jnp.min(jnp.where(grp == top1, slot, EXPERTS_PER_GROUP), axis=1, keepdims=True)
    top2 = jnp.max(jnp.where(slot == first, -jnp.inf, grp), axis=1)
    grp_score = top1[:, 0, :] + top2
    grp_on = _rank_desc(grp_score, N_GROUPS) < TOP_GROUPS
    emask = jnp.broadcast_to(grp_on[:, None, :], grp.shape).reshape(N_EXPERTS, tr)
    masked = jnp.where(emask, biased, -jnp.inf)

    eidx = lax.broadcasted_iota(I32, (N_EXPERTS, tr), 0).astype(F32)
    rest = masked
    eids, sels = [], []
    for _ in range(TOP_K):
        top = jnp.max(rest, axis=0, keepdims=True)
        eid = jnp.min(jnp.where(rest == top, eidx, float(N_EXPERTS)), axis=0, keepdims=True)
        hit = eidx == eid
        rest = jnp.where(hit, -jnp.inf, rest)
        eids.append(eid)
        sels.append(jnp.sum(jnp.where(hit, scores, 0.0), axis=0, keepdims=True))
    eid_k = jnp.concatenate(eids, axis=0)
    sel_k = jnp.concatenate(sels, axis=0)
    gate_k = sel_k / jnp.sum(sel_k, axis=0, keepdims=True) * ROUTE_SCALE
    chosen = rest != masked

    t_row = lax.broadcasted_iota(I32, (tr, tr), 0)
    t_col = lax.broadcasted_iota(I32, (tr, tr), 1)
    earlier = (t_row < t_col).astype(BF16)
    pos = carry_ref[...] + _dot(chosen.astype(BF16), earlier)
    carry_ref[...] += jnp.sum(chosen.astype(F32), axis=1, keepdims=True)
    cnt_ref[...] = carry_ref[...]

    poss = [jnp.sum(jnp.where(eidx == eid, pos, 0.0), axis=0, keepdims=True) for eid in eids]
    eid_ref[...] = eid_k.astype(I32)
    pos_ref[...] = jnp.concatenate(poss, axis=0).astype(I32)
    wk_ref[...] = gate_k.T


def _router(h2, w_router, router_bias):
    T, D = h2.shape
    tr = ROUTER_TILE
    assert T % tr == 0
    return pl.pallas_call(
        _router_kernel,
        name="router",
        grid=(T // tr,),
        in_specs=[pl.BlockSpec((tr, D), lambda i: (i, 0)),
                  _const_spec((N_EXPERTS, D)), _const_spec((N_EXPERTS, 1))],
        out_specs=[pl.BlockSpec((TOP_K, tr), lambda i: (0, i)),
                   pl.BlockSpec((TOP_K, tr), lambda i: (0, i)),
                   pl.BlockSpec((tr, TOP_K), lambda i: (i, 0)),
                   pl.BlockSpec((N_EXPERTS, 1), lambda i: (0, 0))],
        out_shape=[jax.ShapeDtypeStruct((TOP_K, T), I32),
                   jax.ShapeDtypeStruct((TOP_K, T), I32),
                   jax.ShapeDtypeStruct((T, TOP_K), F32),
                   jax.ShapeDtypeStruct((N_EXPERTS, 1), F32)],
        scratch_shapes=[pltpu.VMEM((N_EXPERTS, 1), F32)],
        compiler_params=pltpu.CompilerParams(
            dimension_semantics=("arbitrary",), vmem_limit_bytes=VMEM_LIMIT),
    )(h2, w_router.T, router_bias.reshape(N_EXPERTS, 1))


def _slot_kernel(offs_ref, eid_ref, pos_ref, slot_ref):
    eid = eid_ref[...]
    base = jnp.zeros(eid.shape, I32)
    for e in range(N_EXPERTS):
        base = jnp.where(eid == e, offs_ref[e], base)
    slot_ref[...] = pos_ref[...] + base


def _slots(offsets, eid, pos):
    shape = eid.shape
    return pl.pallas_call(
        _slot_kernel,
        name="slots",
        grid_spec=pltpu.PrefetchScalarGridSpec(
            num_scalar_prefetch=1, grid=(1,),
            in_specs=[pl.BlockSpec(shape, lambda i, offs: (0, 0)),
                      pl.BlockSpec(shape, lambda i, offs: (0, 0))],
            out_specs=pl.BlockSpec(shape, lambda i, offs: (0, 0))),
        out_shape=jax.ShapeDtypeStruct(shape, I32),
        compiler_params=pltpu.CompilerParams(vmem_limit_bytes=VMEM_LIMIT),
    )(offsets, eid, pos)


def _sc_mesh():
    return plsc.VectorSubcoreMesh(core_axis_name="core", subcore_axis_name="subcore",
                                  num_cores=SC_CORES, num_subcores=SC_SUBCORES)


def _sc_worker():
    return lax.axis_index("subcore") * SC_CORES + lax.axis_index("core")


def _sc_dispatch(words, slots, n_rows):
    T = words.shape[0]
    n_chunks = slots.shape[0]
    workers = SC_CORES * SC_SUBCORES
    assert T == n_chunks * SC_CHUNK and n_chunks % workers == 0
    per_worker = n_chunks // workers

    @functools.partial(
        pl.kernel, mesh=_sc_mesh(),
        out_type=jax.ShapeDtypeStruct((n_rows, HALF), I32),
        scratch_types=[pltpu.VMEM((SC_CHUNK, HALF), I32), pltpu.VMEM((TOP_K, SC_CHUNK), I32),
                       pltpu.SemaphoreType.DMA])
    def dispatch(words_hbm, slots_hbm, out_hbm, rows_v, idx_v, sem):
        first = _sc_worker() * per_worker

        @pl.loop(0, per_worker)
        def _(i):
            c = first + i
            pltpu.sync_copy(slots_hbm.at[c], idx_v)
            pltpu.sync_copy(words_hbm.at[pl.ds(c * SC_CHUNK, SC_CHUNK)], rows_v)
            copies = [pltpu.async_copy(rows_v, out_hbm.at[idx_v.at[k]], sem) for k in range(TOP_K)]
            for cp in copies:
                cp.wait()

    return dispatch(words, slots)


def _sc_collect(rows, slots, T):
    n_chunks = slots.shape[0]
    workers = SC_CORES * SC_SUBCORES
    assert T == n_chunks * SC_CHUNK and n_chunks % workers == 0
    per_worker = n_chunks // workers

    @functools.partial(
        pl.kernel, mesh=_sc_mesh(),
        out_type=jax.ShapeDtypeStruct((TOP_K, T, HALF), I32),
        scratch_types=[pltpu.VMEM((2, SC_CHUNK, HALF), I32), pltpu.VMEM((TOP_K, SC_CHUNK), I32),
                       pltpu.SemaphoreType.DMA, pltpu.SemaphoreType.DMA((2,))])
    def collect(rows_hbm, slots_hbm, out_hbm, rows_v, idx_v, sem_in, sem_out):
        first = _sc_worker() * per_worker

        @pl.loop(0, per_worker)
        def _(i):
            c = first + i
            pltpu.sync_copy(slots_hbm.at[c], idx_v)
            writes = []
            for k in range(TOP_K):
                buf = rows_v.at[k % 2]
                if k >= 2:
                    writes[k - 2].wait()
                pltpu.async_copy(rows_hbm.at[idx_v.at[k]], buf, sem_in).wait()
                writes.append(pltpu.async_copy(buf, out_hbm.at[k, pl.ds(c * SC_CHUNK, SC_CHUNK)],
                                               sem_out.at[k % 2]))
            writes[TOP_K - 2].wait()
            writes[TOP_K - 1].wait()

    return collect(rows, slots)


def _expert_kernel(tile_expert_ref, tile_rows_ref, xs_ref, wg_ref, wu_ref, wd_ref, ys_ref,
                   wgu_ref, wdn_ref):
    i = pl.program_id(0)
    e = tile_expert_ref[i]
    n_valid = tile_rows_ref[i]
    e_prev = tile_expert_ref[jnp.maximum(i - 1, 0)]

    @pl.when((i == 0) | (e != e_prev))
    def _():
        wgu_ref[:, :EXPERT_FF] = wg_ref[0].astype(BF16)
        wgu_ref[:, EXPERT_FF:] = wu_ref[0].astype(BF16)
        wdn_ref[...] = wd_ref[0].astype(BF16)

    @pl.when(n_valid > 0)
    def _():
        words = xs_ref[...]
        live = lax.broadcasted_iota(I32, (words.shape[0], 1), 0) < n_valid
        lo, hi = _unpack_rows(jnp.where(live, words, 0))
        a = _dot(lo.astype(BF16), wgu_ref[:HALF, :]) + _dot(hi.astype(BF16), wgu_ref[HALF:, :])
        act = a[:, :EXPERT_FF]
        hid = act * jax.nn.sigmoid(act) * a[:, EXPERT_FF:]
        ys_ref[...] = _pack_rows(_dot(hid.astype(BF16), wdn_ref[...]))

    @pl.when(n_valid == 0)
    def _():
        ys_ref[...] = jnp.zeros_like(ys_ref)


def _experts(xs, tile_expert, tile_rows, w_exp_gate, w_exp_up, w_exp_down):
    n_rows = xs.shape[0]
    R = EXPERT_ROWS
    D = D_MODEL
    assert n_rows % R == 0
    return pl.pallas_call(
        _expert_kernel,
        name="experts",
        grid_spec=pltpu.PrefetchScalarGridSpec(
            num_scalar_prefetch=2, grid=(n_rows // R,),
            in_specs=[pl.BlockSpec((R, HALF), lambda i, te, tr: (i, 0)),
                      pl.BlockSpec((1, D, EXPERT_FF), lambda i, te, tr: (te[i], 0, 0)),
                      pl.BlockSpec((1, D, EXPERT_FF), lambda i, te, tr: (te[i], 0, 0)),
                      pl.BlockSpec((1, EXPERT_FF, D), lambda i, te, tr: (te[i], 0, 0))],
            out_specs=pl.BlockSpec((R, HALF), lambda i, te, tr: (i, 0)),
            scratch_shapes=[pltpu.VMEM((D, 2 * EXPERT_FF), BF16), pltpu.VMEM((EXPERT_FF, D), BF16)]),
        out_shape=jax.ShapeDtypeStruct((n_rows, HALF), I32),
        compiler_params=pltpu.CompilerParams(
            dimension_semantics=("arbitrary",), vmem_limit_bytes=VMEM_LIMIT),
    )(tile_expert, tile_rows, xs, w_exp_gate, w_exp_up, w_exp_down)


def _final_kernel(h_ref, yk_ref, wk_ref, wsgu_ref, wsd_ref, lng_ref, lnb_ref, *rest, alpha):
    out_ref = rest[-1]
    h = h_ref[...]
    a = _dot(h.astype(BF16), wsgu_ref[...])
    act = a[:, :SHARED_FF]
    hid = act * jax.nn.sigmoid(act) * a[:, SHARED_FF:]
    shared = _dot(hid.astype(BF16), wsd_ref[...])
    acc_lo = shared[:, :HALF]
    acc_hi = shared[:, HALF:]
    wk = wk_ref[...]
    for k in range(TOP_K):
        lo, hi = _unpack_rows(yk_ref[k])
        g = wk[:, k:k + 1]
        acc_lo = acc_lo + g * lo
        acc_hi = acc_hi + g * hi
    ffn = jnp.concatenate([acc_lo, acc_hi], axis=1)
    out_ref[...] = _layer_norm(alpha * h + ffn, lng_ref[...], lnb_ref[...])


def _final(h2, yk, wk, w_sgu, w_sd, ln_g, ln_b, alpha, out_prev, row0, total_rows):
    T, D = h2.shape
    tm = FINAL_TILE
    assert T % tm == 0 and row0 % tm == 0
    tile0 = row0 // tm
    operands = [h2, yk, wk, w_sgu, w_sd, ln_g.reshape(1, D), ln_b.reshape(1, D)]
    in_specs = [pl.BlockSpec((tm, D), lambda i: (i, 0)),
                pl.BlockSpec((TOP_K, tm, HALF), lambda i: (0, i, 0)),
                pl.BlockSpec((tm, TOP_K), lambda i: (i, 0)),
                _const_spec((D, 2 * SHARED_FF)), _const_spec((SHARED_FF, D)),
                _const_spec((1, D)), _const_spec((1, D))]
    aliases = {}
    if out_prev is not None:
        aliases = {len(operands): 0}
        operands.append(out_prev)
        in_specs.append(pl.BlockSpec(memory_space=pl.ANY))
    return pl.pallas_call(
        functools.partial(_final_kernel, alpha=alpha),
        name="final",
        grid=(T // tm,),
        in_specs=in_specs,
        out_specs=pl.BlockSpec((tm, D), lambda i: (i + tile0, 0)),
        out_shape=jax.ShapeDtypeStruct((total_rows, D), F32),
        input_output_aliases=aliases,
        compiler_params=pltpu.CompilerParams(
            dimension_semantics=("arbitrary",), vmem_limit_bytes=VMEM_LIMIT),
    )(*operands)


def _route_and_dispatch(h2, words, w_router, router_bias):
    T, D = h2.shape
    R = EXPERT_ROWS
    eid, pos, wk, counts = _router(h2, w_router, router_bias)
    counts = counts.reshape(N_EXPERTS).astype(I32)
    padded = (counts + (R - 1)) // R * R
    ends = jnp.cumsum(padded)
    offsets = ends - padded
    n_rows = T * TOP_K + N_EXPERTS * R
    tile_start = jnp.arange(n_rows // R, dtype=I32) * R
    past = (tile_start[:, None] >= ends[None, :]).astype(I32)
    tile_expert = jnp.minimum(jnp.sum(past, axis=1), N_EXPERTS - 1)
    mine = (tile_expert[:, None] == jnp.arange(N_EXPERTS, dtype=I32)[None, :]).astype(I32)
    valid_end = jnp.sum(mine * (offsets + counts)[None, :], axis=1)
    tile_rows = jnp.clip(valid_end - tile_start, 0, R).astype(I32)
    slots = _slots(offsets.astype(I32), eid, pos)
    slots = slots.reshape(TOP_K, T // SC_CHUNK, SC_CHUNK).transpose(1, 0, 2)
    xs = _sc_dispatch(words, slots, n_rows)
    return xs, slots, tile_expert, tile_rows, wk


def kernel(x, w_in, w_gate_up, b_gate, gla_norm_g, w_gla_up, w_pool_grp, pool_scale, w_pool_up,
           w_out, ln1_g, ln1_b, w_router, router_bias, w_exp_gate, w_exp_up, w_exp_down,
           w_sh_gate, w_sh_up, w_sh_down, ln2_g, ln2_b):
    B, S, D = x.shape
    depth = w_in.shape[0]
    alpha = (2.0 * depth) ** 0.25
    n_groups = TOKEN_GROUPS if B % TOKEN_GROUPS == 0 else 1
    gb = B // n_groups
    gt = gb * S
    h = x
    for l in range(depth):
        mix_ops = _mixer_operands(w_in[l], w_gate_up[l], b_gate[l], gla_norm_g[l], w_gla_up[l],
                                  w_pool_grp[l], pool_scale[l], w_pool_up[l], w_out[l],
                                  ln1_g[l], ln1_b[l])
        w_sgu = jnp.concatenate([w_sh_gate[l], w_sh_up[l]], axis=-1).astype(BF16)
        w_sd = w_sh_down[l].astype(BF16)
        staged = []
        for g in range(n_groups):
            hg, words = _mixer(h, mix_ops, alpha, g * gb, gb)
            hg = hg.reshape(gt, D)
            staged.append((hg,) + _route_and_dispatch(hg, words.reshape(gt, HALF),
                                                      w_router[l], router_bias[l]))
        out = None
        for g, (hg, xs, slots, tile_expert, tile_rows, wk) in enumerate(staged):
            ys = _experts(xs, tile_expert, tile_rows, w_exp_gate[l], w_exp_up[l], w_exp_down[l])
            yk = _sc_collect(ys, slots, gt)
            out = _final(hg, yk, wk, w_sgu, w_sd, ln2_g[l], ln2_b[l], alpha, out, g * gt, B * S)
        h = out.reshape(B, S, D)
    return h
```

```python
import functools

import jax
import jax.numpy as jnp
from jax import lax
from jax.experimental import pallas as pl
from jax.experimental.pallas import tpu as pltpu
from jax.experimental.pallas import tpu_sc as plsc

F32 = jnp.float32
BF16 = jnp.bfloat16
I32 = jnp.int32
U32 = jnp.uint32

D_MODEL = 1024
GLA_HEADS = 4
GLA_DK = D_MODEL // 2
GLA_DV = D_MODEL
HEAD_K = GLA_DK // GLA_HEADS
HEAD_V = GLA_DV // GLA_HEADS
GATE_RANK = 16
GATE_TEMP = 16.0
POOL_WIDTH = D_MODEL // 2
POOL_GROUPS = 4
POOL_GROUP_DIM = POOL_WIDTH // POOL_GROUPS
POOL_WINDOWS = (2, 4, 8, 16)
POOL_HALO = 16
N_EXPERTS = 64
TOP_K = 8
N_GROUPS = 8
TOP_GROUPS = 4
EXPERTS_PER_GROUP = N_EXPERTS // N_GROUPS
EXPERT_FF = 256
SHARED_FF = 256
ROUTE_SCALE = 2.5
LN_EPS = 1e-5
RMS_EPS = 1e-6
LANES = 128

GLA_CHUNK = 256
MIX_TILE = 512
ROUTER_TILE = 512
FINAL_TILE = 512
EXPERT_ROWS = 1024
TOKEN_GROUPS = 1
SC_CORES = 2
SC_SUBCORES = 16
SC_LANES = 16
SC_CHUNK = 64
SC_SUM_CHUNK = 8
HALF = D_MODEL // 2
VMEM_LIMIT = 56 * 1024 * 1024

_dot = functools.partial(jnp.dot, preferred_element_type=F32)
_NT = (((1,), (1,)), ((), ()))
_TN = (((0,), (0,)), ((), ()))


def _layer_norm(y, g, b):
    mu = jnp.mean(y, axis=-1, keepdims=True)
    yc = y - mu
    var = jnp.mean(yc * yc, axis=-1, keepdims=True)
    return yc * lax.rsqrt(var + LN_EPS) * g + b


def _split3(a):
    hi = a.astype(BF16)
    r1 = a - hi.astype(F32)
    mid = r1.astype(BF16)
    lo = (r1 - mid.astype(F32)).astype(BF16)
    return hi, mid, lo


def _pack_rows(y):
    lo = lax.bitcast_convert_type(y[:, :HALF].astype(BF16).astype(F32), U32)
    hi = lax.bitcast_convert_type(y[:, HALF:].astype(BF16).astype(F32), U32)
    return lax.bitcast_convert_type(hi | (lo >> 16), I32)


def _unpack_rows(w):
    u = lax.bitcast_convert_type(w, U32)
    lo = lax.bitcast_convert_type(u << 16, F32)
    hi = lax.bitcast_convert_type(u & jnp.uint32(0xFFFF0000), F32)
    return lo, hi


def _mixer_kernel(x_ref, wqk_ref, wv_ref, wr_ref, wglr_ref, wgup_ref, bgate_ref, wu_ref,
                  wga_ref, wgb_ref, gnorm_ref, wglaup_ref, wpgrp_ref, pscale_ref, wpup_ref,
                  wout_ref, lng_ref, lnb_ref, out_ref, words_ref, state_ref, ucat_ref,
                  *, alpha, tm):
    j = pl.program_id(1)

    @pl.when(j == 0)
    def _():
        state_ref[...] = jnp.zeros_like(state_ref)
        ucat_ref[0:POOL_HALO, :] = jnp.zeros((POOL_HALO, POOL_WIDTH), F32)

    C = GLA_CHUNK
    causal = lax.broadcasted_iota(I32, (C, C), 0) >= lax.broadcasted_iota(I32, (C, C), 1)
    tri = causal.astype(BF16)
    gnorm = gnorm_ref[...]
    states = [state_ref[h] for h in range(GLA_HEADS)]

    for c in range(tm // C):
        rows = pl.ds(c * C, C)
        x = x_ref[0, rows, :]
        xb = x.astype(BF16)

        qk = _dot(xb, wqk_ref[...])
        v = _dot(xb, wv_ref[...])
        g_lr = _dot(xb, wglr_ref[...])
        z = _dot(g_lr.astype(BF16), wgup_ref[...]) + bgate_ref[...]
        u = _dot(xb, wu_ref[...])
        r = _dot(xb, wr_ref[...])
        log_decay = (jnp.minimum(z, 0.0) - jnp.log1p(jnp.exp(-jnp.abs(z)))) * (1.0 / GATE_TEMP)
        g_hi, g_mid, g_lo = _split3(log_decay)
        gate_a = _dot(xb, wga_ref[...])
        bcum = _dot(tri, g_hi) + _dot(tri, g_mid) + _dot(tri, g_lo)

        ucat_ref[POOL_HALO:, :] = u
        pos = j * tm + c * C - POOL_HALO + lax.broadcasted_iota(I32, (C + POOL_HALO, 1), 0)

        def pool_group(gi):
            window = POOL_WINDOWS[gi]
            a = ucat_ref[:, gi * POOL_GROUP_DIM:(gi + 1) * POOL_GROUP_DIM]
            s = a
            step = 1
            while step < window:
                s = s + pltpu.roll(s, step, 0)
                step *= 2
            count = jnp.clip(pos + 1, 1, window).astype(F32)
            pooled = (s / count - a)[POOL_HALO:, :]
            return _dot(pooled.astype(BF16), wpgrp_ref[gi])

        heads, mixed = [], []
        for h in range(GLA_HEADS):
            ks = slice(h * HEAD_K, (h + 1) * HEAD_K)
            b = bcum[:, ks]
            b_last = b[C - 1:C, :]
            q_h = qk[:, ks] * (HEAD_K ** -0.5)
            k_h = qk[:, GLA_DK + h * HEAD_K:GLA_DK + (h + 1) * HEAD_K]
            v_h = v[:, h * HEAD_V:(h + 1) * HEAD_V].astype(BF16)
            q_in = (q_h * jnp.exp(b)).astype(BF16)
            k_out = (k_h * jnp.exp(-b)).astype(BF16)
            k_end = (k_h * jnp.exp(b_last - b)).astype(BF16)
            scores = lax.dot_general(q_in, k_out, _NT, preferred_element_type=F32)
            scores = jnp.where(causal, scores, 0.0).astype(BF16)
            st = states[h]
            o = _dot(scores, v_h) + lax.dot_general(q_in, st.astype(BF16), _NT,
                                                    preferred_element_type=F32)
            states[h] = st * jnp.exp(b_last) + lax.dot_general(
                v_h, k_end, _TN, preferred_element_type=F32)
            ms = jnp.mean(o * o, axis=-1, keepdims=True)
            heads.append(o * lax.rsqrt(ms + RMS_EPS) * gnorm)
            mixed.append(pool_group(h))
            if h == 1:
                gate_b = _dot(xb, wgb_ref[...])
        ucat_ref[0:POOL_HALO, :] = u[C - POOL_HALO:, :]
        mixed = jnp.concatenate(mixed, axis=1) * pscale_ref[...]
        y_pool = _dot(mixed.astype(BF16), wpup_ref[...])
        o = jnp.concatenate(heads, axis=1) * (r * jax.nn.sigmoid(r))
        y_gla = _dot(o.astype(BF16), wglaup_ref[...])

        merged = jax.nn.sigmoid(gate_a) * y_gla + jax.nn.sigmoid(gate_b) * y_pool
        mix = _dot(merged.astype(BF16), wout_ref[...])
        y = _layer_norm(alpha * x + mix, lng_ref[...], lnb_ref[...])
        out_ref[0, rows, :] = y
        words_ref[0, rows, :] = _pack_rows(y)

    for h in range(GLA_HEADS):
        state_ref[h] = states[h]


def _const_spec(shape):
    zeros = (0,) * len(shape)
    return pl.BlockSpec(shape, lambda *_: zeros, pipeline_mode=pl.Buffered(1))


def _mixer_operands(w_in, w_gate_up, b_gate, gla_norm_g, w_gla_up, w_pool_grp, pool_scale,
                    w_pool_up, w_out, ln_g, ln_b):
    D = D_MODEL
    c = 0
    w_qk = w_in[:, c:c + 2 * GLA_DK].astype(BF16); c += 2 * GLA_DK
    w_v = w_in[:, c:c + GLA_DV].astype(BF16); c += GLA_DV
    w_r = w_in[:, c:c + GLA_DV].astype(BF16); c += GLA_DV
    w_glr = jnp.pad(w_in[:, c:c + GATE_RANK], ((0, 0), (0, LANES - GATE_RANK))).astype(BF16); c += GATE_RANK
    w_u = w_in[:, c:c + POOL_WIDTH].astype(BF16); c += POOL_WIDTH
    w_ga = w_in[:, c:c + D].astype(BF16); c += D
    w_gb = w_in[:, c:c + D].astype(BF16); c += D
    assert c == w_in.shape[1]
    w_gup = jnp.pad(w_gate_up, ((0, LANES - GATE_RANK), (0, 0))).astype(BF16)
    return (
        w_qk, w_v, w_r, w_glr, w_gup, b_gate.reshape(1, GLA_DK), w_u, w_ga, w_gb,
        gla_norm_g.reshape(1, HEAD_V), w_gla_up.astype(BF16), w_pool_grp.astype(BF16),
        pool_scale.reshape(1, POOL_WIDTH), w_pool_up.astype(BF16), w_out.astype(BF16),
        ln_g.reshape(1, D), ln_b.reshape(1, D))


def _mixer(x, operands, alpha, batch0, n_batch):
    _, S, D = x.shape
    tm = MIX_TILE
    assert D == D_MODEL and S % tm == 0 and tm % GLA_CHUNK == 0
    return pl.pallas_call(
        functools.partial(_mixer_kernel, alpha=alpha, tm=tm),
        name="mixer",
        grid=(n_batch, S // tm),
        in_specs=[pl.BlockSpec((1, tm, D), lambda b, j: (b + batch0, j, 0))]
        + [_const_spec(w.shape) for w in operands],
        out_specs=[pl.BlockSpec((1, tm, D), lambda b, j: (b, j, 0)),
                   pl.BlockSpec((1, tm, HALF), lambda b, j: (b, j, 0))],
        out_shape=[jax.ShapeDtypeStruct((n_batch, S, D), F32),
                   jax.ShapeDtypeStruct((n_batch, S, HALF), I32)],
        scratch_shapes=[
            pltpu.VMEM((GLA_HEADS, HEAD_V, HEAD_K), F32),
            pltpu.VMEM((GLA_CHUNK + POOL_HALO, POOL_WIDTH), F32),
        ],
        compiler_params=pltpu.CompilerParams(
            dimension_semantics=("arbitrary", "arbitrary"), vmem_limit_bytes=VMEM_LIMIT),
    )(x, *operands)


def _rank_desc(vals, n):
    idx = lax.broadcasted_iota(I32, vals.shape, 0)
    rank = jnp.zeros(vals.shape, I32)
    for other in range(n):
        o = vals[other:other + 1, :]
        before = (o > vals) | ((o == vals) & (other < idx))
        rank = rank + before.astype(I32)
    return rank


def _router_kernel(h_ref, wrt_ref, bias_ref, eid_ref, pos_ref, wk_ref, cnt_ref, carry_ref):
    i = pl.program_id(0)

    @pl.when(i == 0)
    def _():
        carry_ref[...] = jnp.zeros_like(carry_ref)

    h = h_ref[...]
    tr = h.shape[0]
    w_hi, w_mid, _ = _split3(wrt_ref[...])
    h_hi, h_mid, _ = _split3(h)
    by_hi = lax.dot_general(jnp.concatenate([w_hi, w_mid], axis=0), h_hi, _NT,
                            preferred_element_type=F32)
    logits = (by_hi[:N_EXPERTS] + by_hi[N_EXPERTS:]
              + lax.dot_general(w_hi, h_mid, _NT, preferred_element_type=F32))
    scores = jax.nn.sigmoid(logits)
    biased = scores + bias_ref[...]
    grp = biased.reshape(N_GROUPS, EXPERTS_PER_GROUP, tr)
    slot = lax.broadcasted_iota(I32, grp.shape, 1)
    top1 = jnp.max(grp, axis=1, keepdims=True)
    first = jnp.min(jnp.where(grp == top1, slot, EXPERTS_PER_GROUP), axis=1, keepdims=True)
    top2 = jnp.max(jnp.where(slot == first, -jnp.inf, grp), axis=1)
    grp_score = top1[:, 0, :] + top2
    grp_on = _rank_desc(grp_score, N_GROUPS) < TOP_GROUPS
    emask = jnp.broadcast_to(grp_on[:, None, :], grp.shape).reshape(N_EXPERTS, tr)
    masked = jnp.where(emask, biased, -jnp.inf)

    eidx = lax.broadcasted_iota(I32, (N_EXPERTS, tr), 0).astype(F32)
    rest = masked
    eids, sels = [], []
    for _ in range(TOP_K):
        top = jnp.max(rest, axis=0, keepdims=True)
        eid = jnp.min(jnp.where(rest == top, eidx, float(N_EXPERTS)), axis=0, keepdims=True)
        hit = eidx == eid
        rest = jnp.where(hit, -jnp.inf, rest)
        eids.append(eid)
        sels.append(jnp.sum(jnp.where(hit, scores, 0.0), axis=0, keepdims=True))
    eid_k = jnp.concatenate(eids, axis=0)
    sel_k = jnp.concatenate(sels, axis=0)
    gate_k = sel_k / jnp.sum(sel_k, axis=0, keepdims=True) * ROUTE_SCALE
    chosen = rest != masked

    t_row = lax.broadcasted_iota(I32, (tr, tr), 0)
    t_col = lax.broadcasted_iota(I32, (tr, tr), 1)
    earlier = (t_row < t_col).astype(BF16)
    pos = carry_ref[...] + _dot(chosen.astype(BF16), earlier)
    carry_ref[...] += jnp.sum(chosen.astype(F32), axis=1, keepdims=True)
    cnt_ref[...] = carry_ref[...]

    poss = [jnp.sum(jnp.where(eidx == eid, pos, 0.0), axis=0, keepdims=True) for eid in eids]
    eid_ref[...] = eid_k.astype(I32)
    pos_ref[...] = jnp.concatenate(poss, axis=0).astype(I32)
    wk_ref[...] = gate_k


def _router(h2, w_router, router_bias):
    T, D = h2.shape
    tr = ROUTER_TILE
    assert T % tr == 0
    return pl.pallas_call(
        _router_kernel,
        name="router",
        grid=(T // tr,),
        in_specs=[pl.BlockSpec((tr, D), lambda i: (i, 0)),
                  _const_spec((N_EXPERTS, D)), _const_spec((N_EXPERTS, 1))],
        out_specs=[pl.BlockSpec((TOP_K, tr), lambda i: (0, i)),
                   pl.BlockSpec((TOP_K, tr), lambda i: (0, i)),
                   pl.BlockSpec((TOP_K, tr), lambda i: (0, i)),
                   pl.BlockSpec((N_EXPERTS, 1), lambda i: (0, 0))],
        out_shape=[jax.ShapeDtypeStruct((TOP_K, T), I32),
                   jax.ShapeDtypeStruct((TOP_K, T), I32),
                   jax.ShapeDtypeStruct((TOP_K, T), F32),
                   jax.ShapeDtypeStruct((N_EXPERTS, 1), F32)],
        scratch_shapes=[pltpu.VMEM((N_EXPERTS, 1), F32)],
        compiler_params=pltpu.CompilerParams(
            dimension_semantics=("arbitrary",), vmem_limit_bytes=VMEM_LIMIT),
    )(h2, w_router.T, router_bias.reshape(N_EXPERTS, 1))


def _slot_kernel(offs_ref, eid_ref, pos_ref, slot_ref):
    eid = eid_ref[...]
    base = jnp.zeros(eid.shape, I32)
    for e in range(N_EXPERTS):
        base = jnp.where(eid == e, offs_ref[e], base)
    slot_ref[...] = pos_ref[...] + base


def _slots(offsets, eid, pos):
    shape = eid.shape
    return pl.pallas_call(
        _slot_kernel,
        name="slots",
        grid_spec=pltpu.PrefetchScalarGridSpec(
            num_scalar_prefetch=1, grid=(1,),
            in_specs=[pl.BlockSpec(shape, lambda i, offs: (0, 0)),
                      pl.BlockSpec(shape, lambda i, offs: (0, 0))],
            out_specs=pl.BlockSpec(shape, lambda i, offs: (0, 0))),
        out_shape=jax.ShapeDtypeStruct(shape, I32),
        compiler_params=pltpu.CompilerParams(vmem_limit_bytes=VMEM_LIMIT),
    )(offsets, eid, pos)


def _sc_mesh():
    return plsc.VectorSubcoreMesh(core_axis_name="core", subcore_axis_name="subcore",
                                  num_cores=SC_CORES, num_subcores=SC_SUBCORES)


def _sc_worker():
    return lax.axis_index("subcore") * SC_CORES + lax.axis_index("core")


def _sc_dispatch(words, slots, n_rows):
    T = words.shape[0]
    n_chunks = slots.shape[0]
    workers = SC_CORES * SC_SUBCORES
    assert T == n_chunks * SC_CHUNK and n_chunks % workers == 0
    per_worker = n_chunks // workers

    @functools.partial(
        pl.kernel, mesh=_sc_mesh(),
        out_type=jax.ShapeDtypeStruct((n_rows, HALF), I32),
        scratch_types=[pltpu.VMEM((SC_CHUNK, HALF), I32), pltpu.VMEM((TOP_K, SC_CHUNK), I32),
                       pltpu.SemaphoreType.DMA])
    def dispatch(words_hbm, slots_hbm, out_hbm, rows_v, idx_v, sem):
        first = _sc_worker() * per_worker

        @pl.loop(0, per_worker)
        def _(i):
            c = first + i
            pltpu.sync_copy(slots_hbm.at[c], idx_v)
            pltpu.sync_copy(words_hbm.at[pl.ds(c * SC_CHUNK, SC_CHUNK)], rows_v)
            copies = [pltpu.async_copy(rows_v, out_hbm.at[idx_v.at[k]], sem) for k in range(TOP_K)]
            for cp in copies:
                cp.wait()

    return dispatch(words, slots)


def _sc_combine(rows, slots, gates, T):
    CH = SC_SUM_CHUNK
    n_chunks = slots.shape[0]
    workers = SC_CORES * SC_SUBCORES
    assert T == n_chunks * CH and n_chunks % (2 * workers) == 0
    per_worker = n_chunks // workers
    vecs = HALF // SC_LANES

    @functools.partial(
        pl.kernel, mesh=_sc_mesh(),
        compiler_params=pltpu.CompilerParams(needs_layout_passes=False),
        out_type=jax.ShapeDtypeStruct((T, HALF), I32),
        scratch_types=[pltpu.VMEM((2, TOP_K, CH, HALF), I32), pltpu.VMEM((2, CH, HALF), I32),
                       pltpu.VMEM((2, TOP_K, CH), I32), pltpu.VMEM((2, TOP_K, CH), F32),
                       pltpu.SemaphoreType.DMA((2,)), pltpu.SemaphoreType.DMA((2,))])
    def combine(rows_hbm, slots_hbm, gates_hbm, out_hbm, rows_v, sum_v, idx_v, gates_v, sem_in, sem_out):
        first = _sc_worker() * per_worker

        def gathers(b):
            return [pltpu.make_async_copy(rows_hbm.at[idx_v.at[b, k]], rows_v.at[b, k], sem_in.at[b])
                    for k in range(TOP_K)]

        def start_fetch(c, b):
            pltpu.sync_copy(slots_hbm.at[c], idx_v.at[b])
            pltpu.sync_copy(gates_hbm.at[c], gates_v.at[b])
            for cp in gathers(b):
                cp.start()

        def write_out(c, b):
            return pltpu.make_async_copy(sum_v.at[b], out_hbm.at[pl.ds(c * CH, CH)], sem_out.at[b])

        def add_rows(b):
            @pl.loop(0, CH)
            def _(t):
                token = jnp.full((SC_LANES,), t, I32)
                gate = []
                for k in range(TOP_K):
                    g = plsc.load_gather(gates_v.at[b, k], [token])
                    gate.append(plsc.pack(g, g, format=plsc.PackFormat.INTERLEAVED))

                @plsc.parallel_loop(0, vecs, 1, unroll=2)
                def _(n):
                    lanes = pl.ds(n * SC_LANES, SC_LANES)
                    terms = [plsc.bitcast(rows_v[b, k, t, lanes], BF16) * gate[k] for k in range(TOP_K)]
                    while len(terms) > 1:
                        terms = [x + y for x, y in zip(terms[::2], terms[1::2])]
                    sum_v[b, t, lanes] = plsc.bitcast(terms[0], I32)

        start_fetch(first, 0)

        @pl.loop(0, per_worker, step=2)
        def _(i):
            for b in range(2):
                c = first + i + b

                @pl.when(i + b + 1 < per_worker)
                def _():
                    start_fetch(c + 1, 1 - b)

                for cp in gathers(b):
                    cp.wait()

                @pl.when(i >= 2)
                def _():
                    write_out(c - 2, b).wait()

                add_rows(b)
                write_out(c, b).start()

        write_out(first + per_worker - 2, 0).wait()
        write_out(first + per_worker - 1, 1).wait()

    return combine(rows, slots, gates)


def _expert_kernel(tile_expert_ref, tile_rows_ref, xs_ref, wg_ref, wu_ref, wd_ref, ys_ref,
                   wgu_ref, wdn_ref):
    i = pl.program_id(0)
    e = tile_expert_ref[i]
    n_valid = tile_rows_ref[i]
    e_prev = tile_expert_ref[jnp.maximum(i - 1, 0)]

    @pl.when((i == 0) | (e != e_prev))
    def _():
        wgu_ref[:, :EXPERT_FF] = wg_ref[0].astype(BF16)
        wgu_ref[:, EXPERT_FF:] = wu_ref[0].astype(BF16)
        wdn_ref[...] = wd_ref[0].astype(BF16)

    @pl.when(n_valid > 0)
    def _():
        words = xs_ref[...]
        live = lax.broadcasted_iota(I32, (words.shape[0], 1), 0) < n_valid
        lo, hi = _unpack_rows(jnp.where(live, words, 0))
        a = _dot(lo.astype(BF16), wgu_ref[:HALF, :]) + _dot(hi.astype(BF16), wgu_ref[HALF:, :])
        act = a[:, :EXPERT_FF]
        hid = act * jax.nn.sigmoid(act) * a[:, EXPERT_FF:]
        ys_ref[...] = _pack_rows(_dot(hid.astype(BF16), wdn_ref[...]))

    @pl.when(n_valid == 0)
    def _():
        ys_ref[...] = jnp.zeros_like(ys_ref)


def _experts(xs, tile_expert, tile_rows, w_exp_gate, w_exp_up, w_exp_down):
    n_rows = xs.shape[0]
    R = EXPERT_ROWS
    D = D_MODEL
    assert n_rows % R == 0
    return pl.pallas_call(
        _expert_kernel,
        name="experts",
        grid_spec=pltpu.PrefetchScalarGridSpec(
            num_scalar_prefetch=2, grid=(n_rows // R,),
            in_specs=[pl.BlockSpec((R, HALF), lambda i, te, tr: (i, 0)),
                      pl.BlockSpec((1, D, EXPERT_FF), lambda i, te, tr: (te[i], 0, 0)),
                      pl.BlockSpec((1, D, EXPERT_FF), lambda i, te, tr: (te[i], 0, 0)),
                      pl.BlockSpec((1, EXPERT_FF, D), lambda i, te, tr: (te[i], 0, 0))],
            out_specs=pl.BlockSpec((R, HALF), lambda i, te, tr: (i, 0)),
            scratch_shapes=[pltpu.VMEM((D, 2 * EXPERT_FF), BF16), pltpu.VMEM((EXPERT_FF, D), BF16)]),
        out_shape=jax.ShapeDtypeStruct((n_rows, HALF), I32),
        compiler_params=pltpu.CompilerParams(
            dimension_semantics=("arbitrary",), vmem_limit_bytes=VMEM_LIMIT),
    )(tile_expert, tile_rows, xs, w_exp_gate, w_exp_up, w_exp_down)


def _final_kernel(h_ref, routed_ref, wsgu_ref, wsd_ref, lng_ref, lnb_ref, *rest, alpha):
    out_ref = rest[-1]
    h = h_ref[...]
    a = _dot(h.astype(BF16), wsgu_ref[...])
    act = a[:, :SHARED_FF]
    hid = act * jax.nn.sigmoid(act) * a[:, SHARED_FF:]
    shared = _dot(hid.astype(BF16), wsd_ref[...])
    lo, hi = _unpack_rows(routed_ref[...])
    ffn = shared + jnp.concatenate([lo, hi], axis=1)
    out_ref[...] = _layer_norm(alpha * h + ffn, lng_ref[...], lnb_ref[...])


def _final(h2, routed, w_sgu, w_sd, ln_g, ln_b, alpha, out_prev, row0, total_rows):
    T, D = h2.shape
    tm = FINAL_TILE
    assert T % tm == 0 and row0 % tm == 0
    tile0 = row0 // tm
    operands = [h2, routed, w_sgu, w_sd, ln_g.reshape(1, D), ln_b.reshape(1, D)]
    in_specs = [pl.BlockSpec((tm, D), lambda i: (i, 0)),
                pl.BlockSpec((tm, HALF), lambda i: (i, 0)),
                _const_spec((D, 2 * SHARED_FF)), _const_spec((SHARED_FF, D)),
                _const_spec((1, D)), _const_spec((1, D))]
    aliases = {}
    if out_prev is not None:
        aliases = {len(operands): 0}
        operands.append(out_prev)
        in_specs.append(pl.BlockSpec(memory_space=pl.ANY))
    return pl.pallas_call(
        functools.partial(_final_kernel, alpha=alpha),
        name="final",
        grid=(T // tm,),
        in_specs=in_specs,
        out_specs=pl.BlockSpec((tm, D), lambda i: (i + tile0, 0)),
        out_shape=jax.ShapeDtypeStruct((total_rows, D), F32),
        input_output_aliases=aliases,
        compiler_params=pltpu.CompilerParams(
            dimension_semantics=("arbitrary",), vmem_limit_bytes=VMEM_LIMIT),
    )(*operands)


def _route_and_dispatch(h2, words, w_router, router_bias):
    T, D = h2.shape
    R = EXPERT_ROWS
    eid, pos, gates, counts = _router(h2, w_router, router_bias)
    counts = counts.reshape(N_EXPERTS).astype(I32)
    padded = (counts + (R - 1)) // R * R
    ends = jnp.cumsum(padded)
    offsets = ends - padded
    n_rows = T * TOP_K + N_EXPERTS * R
    tile_start = jnp.arange(n_rows // R, dtype=I32) * R
    past = (tile_start[:, None] >= ends[None, :]).astype(I32)
    tile_expert = jnp.minimum(jnp.sum(past, axis=1), N_EXPERTS - 1)
    mine = (tile_expert[:, None] == jnp.arange(N_EXPERTS, dtype=I32)[None, :]).astype(I32)
    valid_end = jnp.sum(mine * (offsets + counts)[None, :], axis=1)
    tile_rows = jnp.clip(valid_end - tile_start, 0, R).astype(I32)
    slots = _slots(offsets.astype(I32), eid, pos)

    def chunked(a, size):
        return a.reshape(TOP_K, T // size, size).transpose(1, 0, 2)

    xs = _sc_dispatch(words, chunked(slots, SC_CHUNK), n_rows)
    return xs, chunked(slots, SC_SUM_CHUNK), chunked(gates, SC_SUM_CHUNK), tile_expert, tile_rows


def kernel(x, w_in, w_gate_up, b_gate, gla_norm_g, w_gla_up, w_pool_grp, pool_scale, w_pool_up,
           w_out, ln1_g, ln1_b, w_router, router_bias, w_exp_gate, w_exp_up, w_exp_down,
           w_sh_gate, w_sh_up, w_sh_down, ln2_g, ln2_b):
    B, S, D = x.shape
    depth = w_in.shape[0]
    alpha = (2.0 * depth) ** 0.25
    n_groups = TOKEN_GROUPS if B % TOKEN_GROUPS == 0 else 1
    gb = B // n_groups
    gt = gb * S
    h = x
    for l in range(depth):
        mix_ops = _mixer_operands(w_in[l], w_gate_up[l], b_gate[l], gla_norm_g[l], w_gla_up[l],
                                  w_pool_grp[l], pool_scale[l], w_pool_up[l], w_out[l],
                                  ln1_g[l], ln1_b[l])
        w_sgu = jnp.concatenate([w_sh_gate[l], w_sh_up[l]], axis=-1).astype(BF16)
        w_sd = w_sh_down[l].astype(BF16)
        staged = []
        for g in range(n_groups):
            hg, words = _mixer(h, mix_ops, alpha, g * gb, gb)
            hg = hg.reshape(gt, D)
            staged.append((hg,) + _route_and_dispatch(hg, words.reshape(gt, HALF),
                                                      w_router[l], router_bias[l]))
        out = None
        for g, (hg, xs, slots, gates, tile_expert, tile_rows) in enumerate(staged):
            ys = _experts(xs, tile_expert, tile_rows, w_exp_gate[l], w_exp_up[l], w_exp_down[l])
            routed = _sc_combine(ys, slots, gates, gt)
            out = _final(hg, routed, w_sgu, w_sd, ln2_g[l], ln2_b[l], alpha, out, g * gt, B * S)
        h = out.reshape(B, S, D)
    return h
```

```python
import functools

import jax
import jax.numpy as jnp
from jax import lax
from jax.experimental import pallas as pl
from jax.experimental.pallas import tpu as pltpu
from jax.experimental.pallas import tpu_sc as plsc

F32 = jnp.float32
BF16 = jnp.bfloat16
I32 = jnp.int32
U32 = jnp.uint32

D_MODEL = 1024
GLA_HEADS = 4
GLA_DK = D_MODEL // 2
GLA_DV = D_MODEL
HEAD_K = GLA_DK // GLA_HEADS
HEAD_V = GLA_DV // GLA_HEADS
GATE_RANK = 16
GATE_TEMP = 16.0
POOL_WIDTH = D_MODEL // 2
POOL_GROUPS = 4
POOL_GROUP_DIM = POOL_WIDTH // POOL_GROUPS
POOL_WINDOWS = (2, 4, 8, 16)
POOL_HALO = 16
N_EXPERTS = 64
TOP_K = 8
N_GROUPS = 8
TOP_GROUPS = 4
EXPERTS_PER_GROUP = N_EXPERTS // N_GROUPS
EXPERT_FF = 256
SHARED_FF = 256
ROUTE_SCALE = 2.5
LN_EPS = 1e-5
RMS_EPS = 1e-6
LANES = 128

GLA_CHUNK = 256
MIX_TILE = 512
ROUTER_TILE = 512
FINAL_TILE = 512
EXPERT_ROWS = 1024
TOKEN_GROUPS = 1
SC_CORES = 2
SC_SUBCORES = 16
SC_LANES = 16
SC_CHUNK = 64
SC_SUM_CHUNK = 8
HALF = D_MODEL // 2
VMEM_LIMIT = 56 * 1024 * 1024

_dot = functools.partial(jnp.dot, preferred_element_type=F32)
_NT = (((1,), (1,)), ((), ()))
_TN = (((0,), (0,)), ((), ()))


def _layer_norm(y, g, b):
    mu = jnp.mean(y, axis=-1, keepdims=True)
    yc = y - mu
    var = jnp.mean(yc * yc, axis=-1, keepdims=True)
    return yc * lax.rsqrt(var + LN_EPS) * g + b


def _split3(a):
    hi = a.astype(BF16)
    r1 = a - hi.astype(F32)
    mid = r1.astype(BF16)
    lo = (r1 - mid.astype(F32)).astype(BF16)
    return hi, mid, lo


def _pack_rows(y):
    lo = lax.bitcast_convert_type(y[:, :HALF].astype(BF16).astype(F32), U32)
    hi = lax.bitcast_convert_type(y[:, HALF:].astype(BF16).astype(F32), U32)
    return lax.bitcast_convert_type(hi | (lo >> 16), I32)


def _unpack_rows(w):
    u = lax.bitcast_convert_type(w, U32)
    lo = lax.bitcast_convert_type(u << 16, F32)
    hi = lax.bitcast_convert_type(u & jnp.uint32(0xFFFF0000), F32)
    return lo, hi


def _mixer_kernel(x_ref, wqk_ref, wv_ref, wr_ref, wglr_ref, wgup_ref, bgate_ref, wu_ref,
                  wga_ref, wgb_ref, gnorm_ref, wglaup_ref, wpgrp_ref, pscale_ref, wpup_ref,
                  wout_ref, lng_ref, lnb_ref, out_ref, words_ref, state_ref, ucat_ref,
                  *, alpha, tm):
    j = pl.program_id(1)

    @pl.when(j == 0)
    def _():
        state_ref[...] = jnp.zeros_like(state_ref)
        ucat_ref[0:POOL_HALO, :] = jnp.zeros((POOL_HALO, POOL_WIDTH), F32)

    C = GLA_CHUNK
    causal = lax.broadcasted_iota(I32, (C, C), 0) >= lax.broadcasted_iota(I32, (C, C), 1)
    tri = causal.astype(BF16)
    gnorm = gnorm_ref[...]
    states = [state_ref[h] for h in range(GLA_HEADS)]

    for c in range(tm // C):
        rows = pl.ds(c * C, C)
        x = x_ref[0, rows, :]
        xb = x.astype(BF16)

        qk = _dot(xb, wqk_ref[...])
        v = _dot(xb, wv_ref[...])
        g_lr = _dot(xb, wglr_ref[...])
        z = _dot(g_lr.astype(BF16), wgup_ref[...]) + bgate_ref[...]
        u = _dot(xb, wu_ref[...])
        r = _dot(xb, wr_ref[...])
        log_decay = (jnp.minimum(z, 0.0) - jnp.log1p(jnp.exp(-jnp.abs(z)))) * (1.0 / GATE_TEMP)
        g_hi, g_mid, g_lo = _split3(log_decay)
        gate_a = _dot(xb, wga_ref[...])
        bcum = _dot(tri, g_hi) + _dot(tri, g_mid) + _dot(tri, g_lo)

        ucat_ref[POOL_HALO:, :] = u
        pos = j * tm + c * C - POOL_HALO + lax.broadcasted_iota(I32, (C + POOL_HALO, 1), 0)

        def pool_group(gi):
            window = POOL_WINDOWS[gi]
            a = ucat_ref[:, gi * POOL_GROUP_DIM:(gi + 1) * POOL_GROUP_DIM]
            s = a
            step = 1
            while step < window:
                s = s + pltpu.roll(s, step, 0)
                step *= 2
            count = jnp.clip(pos + 1, 1, window).astype(F32)
            pooled = (s / count - a)[POOL_HALO:, :]
            return _dot(pooled.astype(BF16), wpgrp_ref[gi])

        heads, mixed = [], []
        for h in range(GLA_HEADS):
            ks = slice(h * HEAD_K, (h + 1) * HEAD_K)
            b = bcum[:, ks]
            b_last = b[C - 1:C, :]
            q_h = qk[:, ks] * (HEAD_K ** -0.5)
            k_h = qk[:, GLA_DK + h * HEAD_K:GLA_DK + (h + 1) * HEAD_K]
            v_h = v[:, h * HEAD_V:(h + 1) * HEAD_V].astype(BF16)
            q_in = (q_h * jnp.exp(b)).astype(BF16)
            k_out = (k_h * jnp.exp(-b)).astype(BF16)
            k_end = (k_h * jnp.exp(b_last - b)).astype(BF16)
            scores = lax.dot_general(q_in, k_out, _NT, preferred_element_type=F32)
            scores = jnp.where(causal, scores, 0.0).astype(BF16)
            st = states[h]
            o = _dot(scores, v_h) + lax.dot_general(q_in, st.astype(BF16), _NT,
                                                    preferred_element_type=F32)
            states[h] = st * jnp.exp(b_last) + lax.dot_general(
                v_h, k_end, _TN, preferred_element_type=F32)
            ms = jnp.mean(o * o, axis=-1, keepdims=True)
            heads.append(o * lax.rsqrt(ms + RMS_EPS) * gnorm)
            mixed.append(pool_group(h))
            if h == 1:
                gate_b = _dot(xb, wgb_ref[...])
        ucat_ref[0:POOL_HALO, :] = u[C - POOL_HALO:, :]
        mixed = jnp.concatenate(mixed, axis=1) * pscale_ref[...]
        y_pool = _dot(mixed.astype(BF16), wpup_ref[...])
        o = jnp.concatenate(heads, axis=1) * (r * jax.nn.sigmoid(r))
        y_gla = _dot(o.astype(BF16), wglaup_ref[...])

        merged = jax.nn.sigmoid(gate_a) * y_gla + jax.nn.sigmoid(gate_b) * y_pool
        mix = _dot(merged.astype(BF16), wout_ref[...])
        y = _layer_norm(alpha * x + mix, lng_ref[...], lnb_ref[...])
        out_ref[0, rows, :] = y
        words_ref[0, rows, :] = _pack_rows(y)

    for h in range(GLA_HEADS):
        state_ref[h] = states[h]


def _const_spec(shape):
    zeros = (0,) * len(shape)
    return pl.BlockSpec(shape, lambda *_: zeros, pipeline_mode=pl.Buffered(1))


def _mixer_operands(w_in, w_gate_up, b_gate, gla_norm_g, w_gla_up, w_pool_grp, pool_scale,
                    w_pool_up, w_out, ln_g, ln_b):
    D = D_MODEL
    c = 0
    w_qk = w_in[:, c:c + 2 * GLA_DK].astype(BF16); c += 2 * GLA_DK
    w_v = w_in[:, c:c + GLA_DV].astype(BF16); c += GLA_DV
    w_r = w_in[:, c:c + GLA_DV].astype(BF16); c += GLA_DV
    w_glr = jnp.pad(w_in[:, c:c + GATE_RANK], ((0, 0), (0, LANES - GATE_RANK))).astype(BF16); c += GATE_RANK
    w_u = w_in[:, c:c + POOL_WIDTH].astype(BF16); c += POOL_WIDTH
    w_ga = w_in[:, c:c + D].astype(BF16); c += D
    w_gb = w_in[:, c:c + D].astype(BF16); c += D
    assert c == w_in.shape[1]
    w_gup = jnp.pad(w_gate_up, ((0, LANES - GATE_RANK), (0, 0))).astype(BF16)
    return (
        w_qk, w_v, w_r, w_glr, w_gup, b_gate.reshape(1, GLA_DK), w_u, w_ga, w_gb,
        gla_norm_g.reshape(1, HEAD_V), w_gla_up.astype(BF16), w_pool_grp.astype(BF16),
        pool_scale.reshape(1, POOL_WIDTH), w_pool_up.astype(BF16), w_out.astype(BF16),
        ln_g.reshape(1, D), ln_b.reshape(1, D))


def _mixer(x, operands, alpha, batch0, n_batch):
    _, S, D = x.shape
    tm = MIX_TILE
    assert D == D_MODEL and S % tm == 0 and tm % GLA_CHUNK == 0
    return pl.pallas_call(
        functools.partial(_mixer_kernel, alpha=alpha, tm=tm),
        name="mixer",
        grid=(n_batch, S // tm),
        in_specs=[pl.BlockSpec((1, tm, D), lambda b, j: (b + batch0, j, 0))]
        + [_const_spec(w.shape) for w in operands],
        out_specs=[pl.BlockSpec((1, tm, D), lambda b, j: (b, j, 0)),
                   pl.BlockSpec((1, tm, HALF), lambda b, j: (b, j, 0))],
        out_shape=[jax.ShapeDtypeStruct((n_batch, S, D), F32),
                   jax.ShapeDtypeStruct((n_batch, S, HALF), I32)],
        scratch_shapes=[
            pltpu.VMEM((GLA_HEADS, HEAD_V, HEAD_K), F32),
            pltpu.VMEM((GLA_CHUNK + POOL_HALO, POOL_WIDTH), F32),
        ],
        compiler_params=pltpu.CompilerParams(
            dimension_semantics=("arbitrary", "arbitrary"), vmem_limit_bytes=VMEM_LIMIT),
    )(x, *operands)


def _rank_desc(vals, n):
    idx = lax.broadcasted_iota(I32, vals.shape, 0)
    rank = jnp.zeros(vals.shape, I32)
    for other in range(n):
        o = vals[other:other + 1, :]
        before = (o > vals) | ((o == vals) & (other < idx))
        rank = rank + before.astype(I32)
    return rank


def _router_kernel(h_ref, wrt_ref, bias_ref, eid_ref, pos_ref, wk_ref, cnt_ref, carry_ref):
    i = pl.program_id(0)

    @pl.when(i == 0)
    def _():
        carry_ref[...] = jnp.zeros_like(carry_ref)

    h = h_ref[...]
    tr = h.shape[0]
    w_hi, w_mid, _ = _split3(wrt_ref[...])
    h_hi, h_mid, _ = _split3(h)
    by_hi = lax.dot_general(jnp.concatenate([w_hi, w_mid], axis=0), h_hi, _NT,
                            preferred_element_type=F32)
    logits = (by_hi[:N_EXPERTS] + by_hi[N_EXPERTS:]
              + lax.dot_general(w_hi, h_mid, _NT, preferred_element_type=F32))
    scores = jax.nn.sigmoid(logits)
    biased = scores + bias_ref[...]
    grp = biased.reshape(N_GROUPS, EXPERTS_PER_GROUP, tr)
    slot = lax.broadcasted_iota(I32, grp.shape, 1)
    top1 = jnp.max(grp, axis=1, keepdims=True)
    first = jnp.min(jnp.where(grp == top1, slot, EXPERTS_PER_GROUP), axis=1, keepdims=True)
    top2 = jnp.max(jnp.where(slot == first, -jnp.inf, grp), axis=1)
    grp_score = top1[:, 0, :] + top2
    grp_on = _rank_desc(grp_score, N_GROUPS) < TOP_GROUPS
    emask = jnp.broadcast_to(grp_on[:, None, :], grp.shape).reshape(N_EXPERTS, tr)
    masked = jnp.where(emask, biased, -jnp.inf)

    eidx = lax.broadcasted_iota(I32, (N_EXPERTS, tr), 0).astype(F32)
    rest = masked
    eids, sels = [], []
    for _ in range(TOP_K):
        top = jnp.max(rest, axis=0, keepdims=True)
        eid = jnp.min(jnp.where(rest == top, eidx, float(N_EXPERTS)), axis=0, keepdims=True)
        hit = eidx == eid
        rest = jnp.where(hit, -jnp.inf, rest)
        eids.append(eid)
        sels.append(jnp.sum(jnp.where(hit, scores, 0.0), axis=0, keepdims=True))
    eid_k = jnp.concatenate(eids, axis=0)
    sel_k = jnp.concatenate(sels, axis=0)
    gate_k = sel_k / jnp.sum(sel_k, axis=0, keepdims=True) * ROUTE_SCALE
    chosen = rest != masked

    t_row = lax.broadcasted_iota(I32, (tr, tr), 0)
    t_col = lax.broadcasted_iota(I32, (tr, tr), 1)
    earlier = (t_row < t_col).astype(BF16)
    pos = carry_ref[...] + _dot(chosen.astype(BF16), earlier)
    carry_ref[...] += jnp.sum(chosen.astype(F32), axis=1, keepdims=True)
    cnt_ref[...] = carry_ref[...]

    poss = [jnp.sum(jnp.where(eidx == eid, pos, 0.0), axis=0, keepdims=True) for eid in eids]
    eid_ref[...] = eid_k.astype(I32)
    pos_ref[...] = jnp.concatenate(poss, axis=0).astype(I32)
    wk_ref[...] = gate_k


def _router(h2, w_router, router_bias):
    T, D = h2.shape
    tr = ROUTER_TILE
    assert T % tr == 0
    return pl.pallas_call(
        _router_kernel,
        name="router",
        grid=(T // tr,),
        in_specs=[pl.BlockSpec((tr, D), lambda i: (i, 0)),
                  _const_spec((N_EXPERTS, D)), _const_spec((N_EXPERTS, 1))],
        out_specs=[pl.BlockSpec((TOP_K, tr), lambda i: (0, i)),
                   pl.BlockSpec((TOP_K, tr), lambda i: (0, i)),
                   pl.BlockSpec((TOP_K, tr), lambda i: (0, i)),
                   pl.BlockSpec((N_EXPERTS, 1), lambda i: (0, 0))],
        out_shape=[jax.ShapeDtypeStruct((TOP_K, T), I32),
                   jax.ShapeDtypeStruct((TOP_K, T), I32),
                   jax.ShapeDtypeStruct((TOP_K, T), F32),
                   jax.ShapeDtypeStruct((N_EXPERTS, 1), F32)],
        scratch_shapes=[pltpu.VMEM((N_EXPERTS, 1), F32)],
        compiler_params=pltpu.CompilerParams(
            dimension_semantics=("arbitrary",), vmem_limit_bytes=VMEM_LIMIT),
    )(h2, w_router.T, router_bias.reshape(N_EXPERTS, 1))


def _slot_kernel(offs_ref, eid_ref, pos_ref, slot_ref):
    eid = eid_ref[...]
    base = jnp.zeros(eid.shape, I32)
    for e in range(N_EXPERTS):
        base = jnp.where(eid == e, offs_ref[e], base)
    slot_ref[...] = pos_ref[...] + base


def _slots(offsets, eid, pos):
    shape = eid.shape
    return pl.pallas_call(
        _slot_kernel,
        name="slots",
        grid_spec=pltpu.PrefetchScalarGridSpec(
            num_scalar_prefetch=1, grid=(1,),
            in_specs=[pl.BlockSpec(shape, lambda i, offs: (0, 0)),
                      pl.BlockSpec(shape, lambda i, offs: (0, 0))],
            out_specs=pl.BlockSpec(shape, lambda i, offs: (0, 0))),
        out_shape=jax.ShapeDtypeStruct(shape, I32),
        compiler_params=pltpu.CompilerParams(vmem_limit_bytes=VMEM_LIMIT),
    )(offsets, eid, pos)


def _sc_mesh():
    return plsc.VectorSubcoreMesh(core_axis_name="core", subcore_axis_name="subcore",
                                  num_cores=SC_CORES, num_subcores=SC_SUBCORES)


def _sc_worker():
    return lax.axis_index("subcore") * SC_CORES + lax.axis_index("core")


def _sc_dispatch(words, slots, n_rows):
    T = words.shape[0]
    n_chunks = slots.shape[0]
    workers = SC_CORES * SC_SUBCORES
    assert T == n_chunks * SC_CHUNK and n_chunks % workers == 0
    per_worker = n_chunks // workers

    @functools.partial(
        pl.kernel, mesh=_sc_mesh(),
        out_type=jax.ShapeDtypeStruct((n_rows, HALF), I32),
        scratch_types=[pltpu.VMEM((SC_CHUNK, HALF), I32), pltpu.VMEM((TOP_K, SC_CHUNK), I32),
                       pltpu.SemaphoreType.DMA])
    def dispatch(words_hbm, slots_hbm, out_hbm, rows_v, idx_v, sem):
        first = _sc_worker() * per_worker

        @pl.loop(0, per_worker)
        def _(i):
            c = first + i
            pltpu.sync_copy(slots_hbm.at[c], idx_v)
            pltpu.sync_copy(words_hbm.at[pl.ds(c * SC_CHUNK, SC_CHUNK)], rows_v)
            copies = [pltpu.async_copy(rows_v, out_hbm.at[idx_v.at[k]], sem) for k in range(TOP_K)]
            for cp in copies:
                cp.wait()

    return dispatch(words, slots)


def _sc_combine(rows, slots, gates, T):
    CH = SC_SUM_CHUNK
    steps = SC_CHUNK // CH
    n_blocks = slots.shape[0]
    workers = SC_CORES * SC_SUBCORES
    assert T == n_blocks * SC_CHUNK and n_blocks % workers == 0 and steps >= 2
    per_worker = n_blocks // workers
    vecs = HALF // SC_LANES

    @functools.partial(
        pl.kernel, mesh=_sc_mesh(),
        compiler_params=pltpu.CompilerParams(needs_layout_passes=False),
        out_type=jax.ShapeDtypeStruct((T, HALF), I32),
        scratch_types=[pltpu.VMEM((2, TOP_K, CH, HALF), I32), pltpu.VMEM((2, CH, HALF), I32),
                       pltpu.VMEM((TOP_K, SC_CHUNK), I32), pltpu.VMEM((TOP_K, SC_CHUNK), F32),
                       pltpu.SemaphoreType.DMA((2,)), pltpu.SemaphoreType.DMA((2,))])
    def combine(rows_hbm, slots_hbm, gates_hbm, out_hbm, rows_v, sum_v, idx_v, gates_v, sem_in, sem_out):
        first = _sc_worker() * per_worker

        def gathers(j, b):
            return [pltpu.make_async_copy(rows_hbm.at[idx_v.at[k, pl.ds(j * CH, CH)]], rows_v.at[b, k],
                                          sem_in.at[b]) for k in range(TOP_K)]

        def write_out(token0, b):
            return pltpu.make_async_copy(sum_v.at[b], out_hbm.at[pl.ds(token0, CH)], sem_out.at[b])

        def add_rows(j, b):
            @pl.loop(0, CH)
            def _(t):
                token = jnp.full((SC_LANES,), j * CH + t, I32)
                gate = []
                for k in range(TOP_K):
                    g = plsc.load_gather(gates_v.at[k], [token])
                    gate.append(plsc.pack(g, g, format=plsc.PackFormat.INTERLEAVED))

                @plsc.parallel_loop(0, vecs, 1, unroll=2)
                def _(n):
                    lanes = pl.ds(n * SC_LANES, SC_LANES)
                    terms = [plsc.bitcast(rows_v[b, k, t, lanes], BF16) * gate[k] for k in range(TOP_K)]
                    while len(terms) > 1:
                        terms = [x + y for x, y in zip(terms[::2], terms[1::2])]
                    sum_v[b, t, lanes] = plsc.bitcast(terms[0], I32)

        @pl.loop(0, per_worker)
        def _(i):
            block = first + i
            token0 = block * SC_CHUNK
            pltpu.sync_copy(slots_hbm.at[block], idx_v)
            pltpu.sync_copy(gates_hbm.at[block], gates_v)
            for cp in gathers(0, 0):
                cp.start()
            for j in range(steps):
                b = j % 2
                if j + 1 < steps:
                    for cp in gathers(j + 1, 1 - b):
                        cp.start()
                for cp in gathers(j, b):
                    cp.wait()
                if j >= 2:
                    write_out(token0 + (j - 2) * CH, b).wait()
                add_rows(j, b)
                write_out(token0 + j * CH, b).start()
            write_out(token0 + (steps - 2) * CH, steps % 2).wait()
            write_out(token0 + (steps - 1) * CH, (steps - 1) % 2).wait()

    return combine(rows, slots, gates)


def _expert_kernel(tile_expert_ref, tile_rows_ref, tile_block_ref, xs_ref, wg_ref, wu_ref, wd_ref,
                   ys_ref, wgu_ref, wdn_ref):
    del tile_block_ref
    i = pl.program_id(0)
    e = tile_expert_ref[i]
    n_valid = tile_rows_ref[i]
    e_prev = tile_expert_ref[jnp.maximum(i - 1, 0)]

    @pl.when((i == 0) | (e != e_prev))
    def _():
        wgu_ref[:, :EXPERT_FF] = wg_ref[0].astype(BF16)
        wgu_ref[:, EXPERT_FF:] = wu_ref[0].astype(BF16)
        wdn_ref[...] = wd_ref[0].astype(BF16)

    @pl.when(n_valid > 0)
    def _():
        words = xs_ref[...]
        live = lax.broadcasted_iota(I32, (words.shape[0], 1), 0) < n_valid
        lo, hi = _unpack_rows(jnp.where(live, words, 0))
        a = _dot(lo.astype(BF16), wgu_ref[:HALF, :]) + _dot(hi.astype(BF16), wgu_ref[HALF:, :])
        act = a[:, :EXPERT_FF]
        hid = act * jax.nn.sigmoid(act) * a[:, EXPERT_FF:]
        ys_ref[...] = _pack_rows(_dot(hid.astype(BF16), wdn_ref[...]))


def _experts(xs, tile_expert, tile_rows, tile_block, w_exp_gate, w_exp_up, w_exp_down):
    n_rows = xs.shape[0]
    R = EXPERT_ROWS
    D = D_MODEL
    assert n_rows % R == 0
    return pl.pallas_call(
        _expert_kernel,
        name="experts",
        grid_spec=pltpu.PrefetchScalarGridSpec(
            num_scalar_prefetch=3, grid=(n_rows // R,),
            in_specs=[pl.BlockSpec((R, HALF), lambda i, te, tr, tb: (tb[i], 0)),
                      pl.BlockSpec((1, D, EXPERT_FF), lambda i, te, tr, tb: (te[i], 0, 0)),
                      pl.BlockSpec((1, D, EXPERT_FF), lambda i, te, tr, tb: (te[i], 0, 0)),
                      pl.BlockSpec((1, EXPERT_FF, D), lambda i, te, tr, tb: (te[i], 0, 0))],
            out_specs=pl.BlockSpec((R, HALF), lambda i, te, tr, tb: (tb[i], 0)),
            scratch_shapes=[pltpu.VMEM((D, 2 * EXPERT_FF), BF16), pltpu.VMEM((EXPERT_FF, D), BF16)]),
        out_shape=jax.ShapeDtypeStruct((n_rows, HALF), I32),
        compiler_params=pltpu.CompilerParams(
            dimension_semantics=("arbitrary",), vmem_limit_bytes=VMEM_LIMIT),
    )(tile_expert, tile_rows, tile_block, xs, w_exp_gate, w_exp_up, w_exp_down)


def _final_kernel(h_ref, routed_ref, wsgu_ref, wsd_ref, lng_ref, lnb_ref, *rest, alpha):
    out_ref = rest[-1]
    h = h_ref[...]
    a = _dot(h.astype(BF16), wsgu_ref[...])
    act = a[:, :SHARED_FF]
    hid = act * jax.nn.sigmoid(act) * a[:, SHARED_FF:]
    shared = _dot(hid.astype(BF16), wsd_ref[...])
    lo, hi = _unpack_rows(routed_ref[...])
    ffn = shared + jnp.concatenate([lo, hi], axis=1)
    out_ref[...] = _layer_norm(alpha * h + ffn, lng_ref[...], lnb_ref[...])


def _final(h2, routed, w_sgu, w_sd, ln_g, ln_b, alpha, out_prev, row0, total_rows):
    T, D = h2.shape
    tm = FINAL_TILE
    assert T % tm == 0 and row0 % tm == 0
    tile0 = row0 // tm
    operands = [h2, routed, w_sgu, w_sd, ln_g.reshape(1, D), ln_b.reshape(1, D)]
    in_specs = [pl.BlockSpec((tm, D), lambda i: (i, 0)),
                pl.BlockSpec((tm, HALF), lambda i: (i, 0)),
                _const_spec((D, 2 * SHARED_FF)), _const_spec((SHARED_FF, D)),
                _const_spec((1, D)), _const_spec((1, D))]
    aliases = {}
    if out_prev is not None:
        aliases = {len(operands): 0}
        operands.append(out_prev)
        in_specs.append(pl.BlockSpec(memory_space=pl.ANY))
    return pl.pallas_call(
        functools.partial(_final_kernel, alpha=alpha),
        name="final",
        grid=(T // tm,),
        in_specs=in_specs,
        out_specs=pl.BlockSpec((tm, D), lambda i: (i + tile0, 0)),
        out_shape=jax.ShapeDtypeStruct((total_rows, D), F32),
        input_output_aliases=aliases,
        compiler_params=pltpu.CompilerParams(
            dimension_semantics=("arbitrary",), vmem_limit_bytes=VMEM_LIMIT),
    )(*operands)


def _route_and_dispatch(h2, words, w_router, router_bias):
    T, D = h2.shape
    R = EXPERT_ROWS
    eid, pos, gates, counts = _router(h2, w_router, router_bias)
    counts = counts.reshape(N_EXPERTS).astype(I32)
    padded = (counts + (R - 1)) // R * R
    ends = jnp.cumsum(padded)
    offsets = ends - padded
    n_rows = T * TOP_K + N_EXPERTS * R
    tile_start = jnp.arange(n_rows // R, dtype=I32) * R
    tile_block = jnp.minimum(tile_start, jnp.maximum(ends[-1] - R, 0)) // R
    past = ((tile_block * R)[:, None] >= ends[None, :]).astype(I32)
    tile_expert = jnp.minimum(jnp.sum(past, axis=1), N_EXPERTS - 1)
    mine = (tile_expert[:, None] == jnp.arange(N_EXPERTS, dtype=I32)[None, :]).astype(I32)
    valid_end = jnp.sum(mine * (offsets + counts)[None, :], axis=1)
    tile_rows = jnp.clip(valid_end - tile_start, 0, R).astype(I32)
    slots = _slots(offsets.astype(I32), eid, pos)

    def chunked(a):
        return a.reshape(TOP_K, T // SC_CHUNK, SC_CHUNK).transpose(1, 0, 2)

    slots = chunked(slots)
    xs = _sc_dispatch(words, slots, n_rows)
    return xs, slots, chunked(gates), (tile_expert, tile_rows, tile_block.astype(I32))


def kernel(x, w_in, w_gate_up, b_gate, gla_norm_g, w_gla_up, w_pool_grp, pool_scale, w_pool_up,
           w_out, ln1_g, ln1_b, w_router, router_bias, w_exp_gate, w_exp_up, w_exp_down,
           w_sh_gate, w_sh_up, w_sh_down, ln2_g, ln2_b):
    B, S, D = x.shape
    depth = w_in.shape[0]
    alpha = (2.0 * depth) ** 0.25
    n_groups = TOKEN_GROUPS if B % TOKEN_GROUPS == 0 else 1
    gb = B // n_groups
    gt = gb * S
    h = x
    for l in range(depth):
        mix_ops = _mixer_operands(w_in[l], w_gate_up[l], b_gate[l], gla_norm_g[l], w_gla_up[l],
                                  w_pool_grp[l], pool_scale[l], w_pool_up[l], w_out[l],
                                  ln1_g[l], ln1_b[l])
        w_sgu = jnp.concatenate([w_sh_gate[l], w_sh_up[l]], axis=-1).astype(BF16)
        w_sd = w_sh_down[l].astype(BF16)
        staged = []
        for g in range(n_groups):
            hg, words = _mixer(h, mix_ops, alpha, g * gb, gb)
            hg = hg.reshape(gt, D)
            staged.append((hg,) + _route_and_dispatch(hg, words.reshape(gt, HALF),
                                                      w_router[l], router_bias[l]))
        out = None
        for g, (hg, xs, slots, gates, tiles) in enumerate(staged):
            ys = _experts(xs, *tiles, w_exp_gate[l], w_exp_up[l], w_exp_down[l])
            routed = _sc_combine(ys, slots, gates, gt)
            out = _final(hg, routed, w_sgu, w_sd, ln2_g[l], ln2_b[l], alpha, out, g * gt, B * S)
        h = out.reshape(B, S, D)
    return h
```

```python
import functools

import jax
import jax.numpy as jnp
from jax import lax
from jax.experimental import pallas as pl
from jax.experimental.pallas import tpu as pltpu
from jax.experimental.pallas import tpu_sc as plsc

F32 = jnp.float32
BF16 = jnp.bfloat16
I32 = jnp.int32
U32 = jnp.uint32

D_MODEL = 1024
GLA_HEADS = 4
GLA_DK = D_MODEL // 2
GLA_DV = D_MODEL
HEAD_K = GLA_DK // GLA_HEADS
HEAD_V = GLA_DV // GLA_HEADS
GATE_RANK = 16
GATE_TEMP = 16.0
POOL_WIDTH = D_MODEL // 2
POOL_GROUPS = 4
POOL_GROUP_DIM = POOL_WIDTH // POOL_GROUPS
POOL_WINDOWS = (2, 4, 8, 16)
POOL_HALO = 16
N_EXPERTS = 64
TOP_K = 8
N_GROUPS = 8
TOP_GROUPS = 4
EXPERTS_PER_GROUP = N_EXPERTS // N_GROUPS
EXPERT_FF = 256
SHARED_FF = 256
ROUTE_SCALE = 2.5
LN_EPS = 1e-5
RMS_EPS = 1e-6
LANES = 128
SUBLANES = 8

GLA_CHUNK = 256
DECAY_LIMIT = 60.0
MIX_TILE = 512
ROUTER_TILE = 512
FINAL_TILE = 512
EXPERT_ROWS = 1024
TOKEN_GROUPS = 1
SC_CORES = 2
SC_SUBCORES = 16
SC_LANES = 16
SC_CHUNK = 64
SC_SUM_CHUNK = 8
HALF = D_MODEL // 2
VMEM_LIMIT = 56 * 1024 * 1024

_dot = functools.partial(jnp.dot, preferred_element_type=F32)
_NT = (((1,), (1,)), ((), ()))
_TN = (((0,), (0,)), ((), ()))


def _layer_norm(y, g, b):
    mu = jnp.mean(y, axis=-1, keepdims=True)
    yc = y - mu
    var = jnp.mean(yc * yc, axis=-1, keepdims=True)
    return yc * lax.rsqrt(var + LN_EPS) * g + b


def _split3(a):
    hi = a.astype(BF16)
    r1 = a - hi.astype(F32)
    mid = r1.astype(BF16)
    lo = (r1 - mid.astype(F32)).astype(BF16)
    return hi, mid, lo


def _pack_rows(y):
    lo = lax.bitcast_convert_type(y[:, :HALF].astype(BF16).astype(F32), U32)
    hi = lax.bitcast_convert_type(y[:, HALF:].astype(BF16).astype(F32), U32)
    return lax.bitcast_convert_type(hi | (lo >> 16), I32)


def _unpack_rows(w):
    u = lax.bitcast_convert_type(w, U32)
    lo = lax.bitcast_convert_type(u << 16, F32)
    hi = lax.bitcast_convert_type(u & jnp.uint32(0xFFFF0000), F32)
    return lo, hi


def _mixer_kernel(x_ref, wqk_ref, wv_ref, wr_ref, wglr_ref, wgup_ref, bgate_ref, wu_ref,
                  wga_ref, wgb_ref, gnorm_ref, wglaup_ref, wpgrp_ref, pscale_ref, wpup_ref,
                  wout_ref, lng_ref, lnb_ref, out_ref, words_ref, state_ref, ucat_ref,
                  qk_s, v_s, b_s, intra_s, *, alpha, tm):
    j = pl.program_id(1)

    @pl.when(j == 0)
    def _():
        state_ref[...] = jnp.zeros_like(state_ref)
        ucat_ref[0:POOL_HALO, :] = jnp.zeros((POOL_HALO, POOL_WIDTH), F32)
        intra_s[...] = jnp.zeros_like(intra_s)

    C = GLA_CHUNK
    causal = lax.broadcasted_iota(I32, (C, C), 0) >= lax.broadcasted_iota(I32, (C, C), 1)
    tri = causal.astype(BF16)
    gnorm = gnorm_ref[...]
    states = [state_ref[h] for h in range(GLA_HEADS)]

    n_chunks = tm // C

    xbs = [x_ref[0, pl.ds(c * C, C), :].astype(BF16) for c in range(n_chunks)]
    decays, pre = [], []
    for xb in xbs:
        g_lr = _dot(xb, wglr_ref[...])
        z = _dot(g_lr.astype(BF16), wgup_ref[...]) + bgate_ref[...]
        log_decay = (jnp.minimum(z, 0.0) - jnp.log1p(jnp.exp(-jnp.abs(z)))) * (1.0 / GATE_TEMP)
        decays.append(_split3(log_decay))
    for xb, (g_hi, g_mid, g_lo) in zip(xbs, decays):
        qk = _dot(xb, wqk_ref[...])
        v = _dot(xb, wv_ref[...])
        u = _dot(xb, wu_ref[...])
        r = _dot(xb, wr_ref[...])
        bcum = _dot(tri, g_hi) + _dot(tri, g_mid) + _dot(tri, g_lo)
        pre.append((qk, v, u, r, bcum))

    steepest = pre[0][4][C - 1:C, :]
    for c in range(1, n_chunks):
        steepest = jnp.minimum(steepest, pre[c][4][C - 1:C, :])
    steep = jnp.max(-steepest) > DECAY_LIMIT

    @pl.when(steep)
    def _():
        row_id = lax.broadcasted_iota(I32, (C, 1), 0)
        for c in range(n_chunks):
            qk, v, _, _, bcum = pre[c]
            qk_s[...] = qk
            v_s[...] = v
            b_s[...] = bcum
            for h in range(GLA_HEADS):
                ks = slice(h * HEAD_K, (h + 1) * HEAD_K)
                kks = slice(GLA_DK + h * HEAD_K, GLA_DK + (h + 1) * HEAD_K)
                vs = slice(h * HEAD_V, (h + 1) * HEAD_V)
                q_h = qk[:, ks] * (HEAD_K ** -0.5)
                b_h = bcum[:, ks]

                def add_rows(group, acc):
                    first = pl.multiple_of(group * SUBLANES, SUBLANES)
                    b_g = b_s[pl.ds(first, SUBLANES), ks]
                    k_g = qk_s[pl.ds(first, SUBLANES), kks]
                    v_g = v_s[pl.ds(first, SUBLANES), vs]
                    for i in range(SUBLANES):
                        decay = jnp.exp(jnp.minimum(b_h - b_g[i:i + 1, :], 0.0))
                        decay = jnp.where(row_id >= first + i, decay, 0.0)
                        score = jnp.sum(q_h * k_g[i:i + 1, :] * decay, axis=1, keepdims=True)
                        acc = acc + score * v_g[i:i + 1, :]
                    return acc

                intra_s[pl.ds(c * C, C), vs] = lax.fori_loop(
                    0, C // SUBLANES, add_rows, jnp.zeros((C, HEAD_V), F32))

    for c in range(n_chunks):
        rows = pl.ds(c * C, C)
        qk, v, u, r, bcum = pre[c]
        x = x_ref[0, rows, :]
        xb = x.astype(BF16)
        gate_a = _dot(xb, wga_ref[...])

        ucat_ref[POOL_HALO:, :] = u
        pos = j * tm + c * C - POOL_HALO + lax.broadcasted_iota(I32, (C + POOL_HALO, 1), 0)

        def pool_group(gi):
            window = POOL_WINDOWS[gi]
            a = ucat_ref[:, gi * POOL_GROUP_DIM:(gi + 1) * POOL_GROUP_DIM]
            s = a
            step = 1
            while step < window:
                s = s + pltpu.roll(s, step, 0)
                step *= 2
            count = jnp.clip(pos + 1, 1, window).astype(F32)
            pooled = (s / count - a)[POOL_HALO:, :]
            return _dot(pooled.astype(BF16), wpgrp_ref[gi])

        intra, inter, mixed = [], [], []
        for h in range(GLA_HEADS):
            ks = slice(h * HEAD_K, (h + 1) * HEAD_K)
            b = bcum[:, ks]
            b_last = b[C - 1:C, :]
            q_h = qk[:, ks] * (HEAD_K ** -0.5)
            k_h = qk[:, GLA_DK + h * HEAD_K:GLA_DK + (h + 1) * HEAD_K]
            v_h = v[:, h * HEAD_V:(h + 1) * HEAD_V].astype(BF16)
            q_in = (q_h * jnp.exp(b)).astype(BF16)
            k_out = (k_h * jnp.exp(-b)).astype(BF16)
            k_end = (k_h * jnp.exp(b_last - b)).astype(BF16)
            scores = lax.dot_general(q_in, k_out, _NT, preferred_element_type=F32)
            scores = jnp.where(causal, scores, 0.0).astype(BF16)
            st = states[h]
            intra.append(_dot(scores, v_h))
            inter.append(lax.dot_general(q_in, st.astype(BF16), _NT, preferred_element_type=F32))
            states[h] = st * jnp.exp(b_last) + lax.dot_general(
                v_h, k_end, _TN, preferred_element_type=F32)
            mixed.append(pool_group(h))
            if h == 1:
                gate_b = _dot(xb, wgb_ref[...])
        ucat_ref[0:POOL_HALO, :] = u[C - POOL_HALO:, :]
        mixed = jnp.concatenate(mixed, axis=1) * pscale_ref[...]
        y_pool = _dot(mixed.astype(BF16), wpup_ref[...])

        heads = []
        for h in range(GLA_HEADS):
            o = jnp.where(steep, intra_s[rows, h * HEAD_V:(h + 1) * HEAD_V], intra[h]) + inter[h]
            ms = jnp.mean(o * o, axis=-1, keepdims=True)
            heads.append(o * lax.rsqrt(ms + RMS_EPS) * gnorm)
        o = jnp.concatenate(heads, axis=1) * (r * jax.nn.sigmoid(r))
        y_gla = _dot(o.astype(BF16), wglaup_ref[...])

        merged = jax.nn.sigmoid(gate_a) * y_gla + jax.nn.sigmoid(gate_b) * y_pool
        mix = _dot(merged.astype(BF16), wout_ref[...])
        y = _layer_norm(alpha * x + mix, lng_ref[...], lnb_ref[...])
        out_ref[0, rows, :] = y
        words_ref[0, rows, :] = _pack_rows(y)

    for h in range(GLA_HEADS):
        state_ref[h] = states[h]


def _const_spec(shape):
    zeros = (0,) * len(shape)
    return pl.BlockSpec(shape, lambda *_: zeros, pipeline_mode=pl.Buffered(1))


def _mixer_operands(w_in, w_gate_up, b_gate, gla_norm_g, w_gla_up, w_pool_grp, pool_scale,
                    w_pool_up, w_out, ln_g, ln_b):
    D = D_MODEL
    c = 0
    w_qk = w_in[:, c:c + 2 * GLA_DK].astype(BF16); c += 2 * GLA_DK
    w_v = w_in[:, c:c + GLA_DV].astype(BF16); c += GLA_DV
    w_r = w_in[:, c:c + GLA_DV].astype(BF16); c += GLA_DV
    w_glr = jnp.pad(w_in[:, c:c + GATE_RANK], ((0, 0), (0, LANES - GATE_RANK))).astype(BF16); c += GATE_RANK
    w_u = w_in[:, c:c + POOL_WIDTH].astype(BF16); c += POOL_WIDTH
    w_ga = w_in[:, c:c + D].astype(BF16); c += D
    w_gb = w_in[:, c:c + D].astype(BF16); c += D
    assert c == w_in.shape[1]
    w_gup = jnp.pad(w_gate_up, ((0, LANES - GATE_RANK), (0, 0))).astype(BF16)
    return (
        w_qk, w_v, w_r, w_glr, w_gup, b_gate.reshape(1, GLA_DK), w_u, w_ga, w_gb,
        gla_norm_g.reshape(1, HEAD_V), w_gla_up.astype(BF16), w_pool_grp.astype(BF16),
        pool_scale.reshape(1, POOL_WIDTH), w_pool_up.astype(BF16), w_out.astype(BF16),
        ln_g.reshape(1, D), ln_b.reshape(1, D))


def _mixer(x, operands, alpha, batch0, n_batch):
    _, S, D = x.shape
    tm = MIX_TILE
    assert D == D_MODEL and S % tm == 0 and tm % GLA_CHUNK == 0
    return pl.pallas_call(
        functools.partial(_mixer_kernel, alpha=alpha, tm=tm),
        name="mixer",
        grid=(n_batch, S // tm),
        in_specs=[pl.BlockSpec((1, tm, D), lambda b, j: (b + batch0, j, 0))]
        + [_const_spec(w.shape) for w in operands],
        out_specs=[pl.BlockSpec((1, tm, D), lambda b, j: (b, j, 0)),
                   pl.BlockSpec((1, tm, HALF), lambda b, j: (b, j, 0))],
        out_shape=[jax.ShapeDtypeStruct((n_batch, S, D), F32),
                   jax.ShapeDtypeStruct((n_batch, S, HALF), I32)],
        scratch_shapes=[
            pltpu.VMEM((GLA_HEADS, HEAD_V, HEAD_K), F32),
            pltpu.VMEM((GLA_CHUNK + POOL_HALO, POOL_WIDTH), F32),
            pltpu.VMEM((GLA_CHUNK, 2 * GLA_DK), F32),
            pltpu.VMEM((GLA_CHUNK, GLA_DV), F32),
            pltpu.VMEM((GLA_CHUNK, GLA_DK), F32),
            pltpu.VMEM((tm, GLA_DV), F32),
        ],
        compiler_params=pltpu.CompilerParams(
            dimension_semantics=("arbitrary", "arbitrary"), vmem_limit_bytes=VMEM_LIMIT),
    )(x, *operands)


def _rank_desc(vals, n):
    idx = lax.broadcasted_iota(I32, vals.shape, 0)
    rank = jnp.zeros(vals.shape, I32)
    for other in range(n):
        o = vals[other:other + 1, :]
        before = (o > vals) | ((o == vals) & (other < idx))
        rank = rank + before.astype(I32)
    return rank


def _router_kernel(h_ref, wrt_ref, bias_ref, eid_ref, pos_ref, wk_ref, cnt_ref, carry_ref):
    i = pl.program_id(0)

    @pl.when(i == 0)
    def _():
        carry_ref[...] = jnp.zeros_like(carry_ref)

    h = h_ref[...]
    tr = h.shape[0]
    w_hi, w_mid, _ = _split3(wrt_ref[...])
    h_hi, h_mid, _ = _split3(h)
    by_hi = lax.dot_general(jnp.concatenate([w_hi, w_mid], axis=0), h_hi, _NT,
                            preferred_element_type=F32)
    logits = (by_hi[:N_EXPERTS] + by_hi[N_EXPERTS:]
              + lax.dot_general(w_hi, h_mid, _NT, preferred_element_type=F32))
    scores = jax.nn.sigmoid(logits)
    biased = scores + bias_ref[...]
    grp = biased.reshape(N_GROUPS, EXPERTS_PER_GROUP, tr)
    slot = lax.broadcasted_iota(I32, grp.shape, 1)
    top1 = jnp.max(grp, axis=1, keepdims=True)
    first = jnp.min(jnp.where(grp == top1, slot, EXPERTS_PER_GROUP), axis=1, keepdims=True)
    top2 = jnp.max(jnp.where(slot == first, -jnp.inf, grp), axis=1)
    grp_score = top1[:, 0, :] + top2
    grp_on = _rank_desc(grp_score, N_GROUPS) < TOP_GROUPS
    emask = jnp.broadcast_to(grp_on[:, None, :], grp.shape).reshape(N_EXPERTS, tr)
    masked = jnp.where(emask, biased, -jnp.inf)

    eidx = lax.broadcasted_iota(I32, (N_EXPERTS, tr), 0).astype(F32)
    rest = masked
    eids, sels = [], []
    for _ in range(TOP_K):
        top = jnp.max(rest, axis=0, keepdims=True)
        eid = jnp.min(jnp.where(rest == top, eidx, float(N_EXPERTS)), axis=0, keepdims=True)
        hit = eidx == eid
        rest = jnp.where(hit, -jnp.inf, rest)
        eids.append(eid)
        sels.append(jnp.sum(jnp.where(hit, scores, 0.0), axis=0, keepdims=True))
    eid_k = jnp.concatenate(eids, axis=0)
    sel_k = jnp.concatenate(sels, axis=0)
    gate_k = sel_k / jnp.sum(sel_k, axis=0, keepdims=True) * ROUTE_SCALE
    chosen = rest != masked

    t_row = lax.broadcasted_iota(I32, (tr, tr), 0)
    t_col = lax.broadcasted_iota(I32, (tr, tr), 1)
    earlier = (t_row < t_col).astype(BF16)
    pos = carry_ref[...] + _dot(chosen.astype(BF16), earlier)
    carry_ref[...] += jnp.sum(chosen.astype(F32), axis=1, keepdims=True)
    cnt_ref[...] = carry_ref[...]

    poss = [jnp.sum(jnp.where(eidx == eid, pos, 0.0), axis=0, keepdims=True) for eid in eids]
    eid_ref[...] = eid_k.astype(I32)
    pos_ref[...] = jnp.concatenate(poss, axis=0).astype(I32)
    wk_ref[...] = gate_k


def _router(h2, w_router, router_bias):
    T, D = h2.shape
    tr = ROUTER_TILE
    assert T % tr == 0
    return pl.pallas_call(
        _router_kernel,
        name="router",
        grid=(T // tr,),
        in_specs=[pl.BlockSpec((tr, D), lambda i: (i, 0)),
                  _const_spec((N_EXPERTS, D)), _const_spec((N_EXPERTS, 1))],
        out_specs=[pl.BlockSpec((TOP_K, tr), lambda i: (0, i)),
                   pl.BlockSpec((TOP_K, tr), lambda i: (0, i)),
                   pl.BlockSpec((TOP_K, tr), lambda i: (0, i)),
                   pl.BlockSpec((N_EXPERTS, 1), lambda i: (0, 0))],
        out_shape=[jax.ShapeDtypeStruct((TOP_K, T), I32),
                   jax.ShapeDtypeStruct((TOP_K, T), I32),
                   jax.ShapeDtypeStruct((TOP_K, T), F32),
                   jax.ShapeDtypeStruct((N_EXPERTS, 1), F32)],
        scratch_shapes=[pltpu.VMEM((N_EXPERTS, 1), F32)],
        compiler_params=pltpu.CompilerParams(
            dimension_semantics=("arbitrary",), vmem_limit_bytes=VMEM_LIMIT),
    )(h2, w_router.T, router_bias.reshape(N_EXPERTS, 1))


def _slot_kernel(offs_ref, eid_ref, pos_ref, slot_ref):
    eid = eid_ref[...]
    base = jnp.zeros(eid.shape, I32)
    for e in range(N_EXPERTS):
        base = jnp.where(eid == e, offs_ref[e], base)
    slot_ref[...] = pos_ref[...] + base


def _slots(offsets, eid, pos):
    shape = eid.shape
    return pl.pallas_call(
        _slot_kernel,
        name="slots",
        grid_spec=pltpu.PrefetchScalarGridSpec(
            num_scalar_prefetch=1, grid=(1,),
            in_specs=[pl.BlockSpec(shape, lambda i, offs: (0, 0)),
                      pl.BlockSpec(shape, lambda i, offs: (0, 0))],
            out_specs=pl.BlockSpec(shape, lambda i, offs: (0, 0))),
        out_shape=jax.ShapeDtypeStruct(shape, I32),
        compiler_params=pltpu.CompilerParams(vmem_limit_bytes=VMEM_LIMIT),
    )(offsets, eid, pos)


def _sc_mesh():
    return plsc.VectorSubcoreMesh(core_axis_name="core", subcore_axis_name="subcore",
                                  num_cores=SC_CORES, num_subcores=SC_SUBCORES)


def _sc_worker():
    return lax.axis_index("subcore") * SC_CORES + lax.axis_index("core")


def _sc_dispatch(words, slots, n_rows):
    T = words.shape[0]
    n_chunks = slots.shape[0]
    workers = SC_CORES * SC_SUBCORES
    assert T == n_chunks * SC_CHUNK and n_chunks % workers == 0
    per_worker = n_chunks // workers

    @functools.partial(
        pl.kernel, mesh=_sc_mesh(),
        out_type=jax.ShapeDtypeStruct((n_rows, HALF), I32),
        scratch_types=[pltpu.VMEM((SC_CHUNK, HALF), I32), pltpu.VMEM((TOP_K, SC_CHUNK), I32),
                       pltpu.SemaphoreType.DMA])
    def dispatch(words_hbm, slots_hbm, out_hbm, rows_v, idx_v, sem):
        first = _sc_worker() * per_worker

        @pl.loop(0, per_worker)
        def _(i):
            c = first + i
            pltpu.sync_copy(slots_hbm.at[c], idx_v)
            pltpu.sync_copy(words_hbm.at[pl.ds(c * SC_CHUNK, SC_CHUNK)], rows_v)
            copies = [pltpu.async_copy(rows_v, out_hbm.at[idx_v.at[k]], sem) for k in range(TOP_K)]
            for cp in copies:
                cp.wait()

    return dispatch(words, slots)


def _sc_combine(rows, slots, gates, T):
    CH = SC_SUM_CHUNK
    steps = SC_CHUNK // CH
    n_blocks = slots.shape[0]
    workers = SC_CORES * SC_SUBCORES
    assert T == n_blocks * SC_CHUNK and n_blocks % workers == 0 and steps >= 2
    per_worker = n_blocks // workers
    vecs = HALF // SC_LANES

    @functools.partial(
        pl.kernel, mesh=_sc_mesh(),
        compiler_params=pltpu.CompilerParams(needs_layout_passes=False),
        out_type=jax.ShapeDtypeStruct((T, HALF), I32),
        scratch_types=[pltpu.VMEM((2, TOP_K, CH, HALF), I32), pltpu.VMEM((2, CH, HALF), I32),
                       pltpu.VMEM((TOP_K, SC_CHUNK), I32), pltpu.VMEM((TOP_K, SC_CHUNK), F32),
                       pltpu.SemaphoreType.DMA((2,)), pltpu.SemaphoreType.DMA((2,))])
    def combine(rows_hbm, slots_hbm, gates_hbm, out_hbm, rows_v, sum_v, idx_v, gates_v, sem_in, sem_out):
        first = _sc_worker() * per_worker

        def gathers(j, b):
            return [pltpu.make_async_copy(rows_hbm.at[idx_v.at[k, pl.ds(j * CH, CH)]], rows_v.at[b, k],
                                          sem_in.at[b]) for k in range(TOP_K)]

        def write_out(token0, b):
            return pltpu.make_async_copy(sum_v.at[b], out_hbm.at[pl.ds(token0, CH)], sem_out.at[b])

        def add_rows(j, b):
            @pl.loop(0, CH)
            def _(t):
                token = jnp.full((SC_LANES,), j * CH + t, I32)
                gate = []
                for k in range(TOP_K):
                    g = plsc.load_gather(gates_v.at[k], [token])
                    gate.append(plsc.pack(g, g, format=plsc.PackFormat.INTERLEAVED))

                @plsc.parallel_loop(0, vecs, 1, unroll=2)
                def _(n):
                    lanes = pl.ds(n * SC_LANES, SC_LANES)
                    terms = [plsc.bitcast(rows_v[b, k, t, lanes], BF16) * gate[k] for k in range(TOP_K)]
                    while len(terms) > 1:
                        terms = [x + y for x, y in zip(terms[::2], terms[1::2])]
                    sum_v[b, t, lanes] = plsc.bitcast(terms[0], I32)

        @pl.loop(0, per_worker)
        def _(i):
            block = first + i
            token0 = block * SC_CHUNK
            pltpu.sync_copy(slots_hbm.at[block], idx_v)
            pltpu.sync_copy(gates_hbm.at[block], gates_v)
            for cp in gathers(0, 0):
                cp.start()
            for j in range(steps):
                b = j % 2
                if j + 1 < steps:
                    for cp in gathers(j + 1, 1 - b):
                        cp.start()
                for cp in gathers(j, b):
                    cp.wait()
                if j >= 2:
                    write_out(token0 + (j - 2) * CH, b).wait()
                add_rows(j, b)
                write_out(token0 + j * CH, b).start()
            write_out(token0 + (steps - 2) * CH, steps % 2).wait()
            write_out(token0 + (steps - 1) * CH, (steps - 1) % 2).wait()

    return combine(rows, slots, gates)


def _expert_kernel(tile_expert_ref, tile_rows_ref, tile_block_ref, xs_ref, wg_ref, wu_ref, wd_ref,
                   ys_ref, wgu_ref, wdn_ref):
    del tile_block_ref
    i = pl.program_id(0)
    e = tile_expert_ref[i]
    n_valid = tile_rows_ref[i]
    e_prev = tile_expert_ref[jnp.maximum(i - 1, 0)]

    @pl.when((i == 0) | (e != e_prev))
    def _():
        wgu_ref[:, :EXPERT_FF] = wg_ref[0].astype(BF16)
        wgu_ref[:, EXPERT_FF:] = wu_ref[0].astype(BF16)
        wdn_ref[...] = wd_ref[0].astype(BF16)

    @pl.when(n_valid > 0)
    def _():
        words = xs_ref[...]
        live = lax.broadcasted_iota(I32, (words.shape[0], 1), 0) < n_valid
        lo, hi = _unpack_rows(jnp.where(live, words, 0))
        a = _dot(lo.astype(BF16), wgu_ref[:HALF, :]) + _dot(hi.astype(BF16), wgu_ref[HALF:, :])
        act = a[:, :EXPERT_FF]
        hid = act * jax.nn.sigmoid(act) * a[:, EXPERT_FF:]
        ys_ref[...] = _pack_rows(_dot(hid.astype(BF16), wdn_ref[...]))


def _experts(xs, tile_expert, tile_rows, tile_block, w_exp_gate, w_exp_up, w_exp_down):
    n_rows = xs.shape[0]
    R = EXPERT_ROWS
    D = D_MODEL
    assert n_rows % R == 0
    return pl.pallas_call(
        _expert_kernel,
        name="experts",
        grid_spec=pltpu.PrefetchScalarGridSpec(
            num_scalar_prefetch=3, grid=(n_rows // R,),
            in_specs=[pl.BlockSpec((R, HALF), lambda i, te, tr, tb: (tb[i], 0)),
                      pl.BlockSpec((1, D, EXPERT_FF), lambda i, te, tr, tb: (te[i], 0, 0)),
                      pl.BlockSpec((1, D, EXPERT_FF), lambda i, te, tr, tb: (te[i], 0, 0)),
                      pl.BlockSpec((1, EXPERT_FF, D), lambda i, te, tr, tb: (te[i], 0, 0))],
            out_specs=pl.BlockSpec((R, HALF), lambda i, te, tr, tb: (tb[i], 0)),
            scratch_shapes=[pltpu.VMEM((D, 2 * EXPERT_FF), BF16), pltpu.VMEM((EXPERT_FF, D), BF16)]),
        out_shape=jax.ShapeDtypeStruct((n_rows, HALF), I32),
        compiler_params=pltpu.CompilerParams(
            dimension_semantics=("arbitrary",), vmem_limit_bytes=VMEM_LIMIT),
    )(tile_expert, tile_rows, tile_block, xs, w_exp_gate, w_exp_up, w_exp_down)


def _final_kernel(h_ref, routed_ref, wsgu_ref, wsd_ref, lng_ref, lnb_ref, *rest, alpha):
    out_ref = rest[-1]
    h = h_ref[...]
    a = _dot(h.astype(BF16), wsgu_ref[...])
    act = a[:, :SHARED_FF]
    hid = act * jax.nn.sigmoid(act) * a[:, SHARED_FF:]
    shared = _dot(hid.astype(BF16), wsd_ref[...])
    lo, hi = _unpack_rows(routed_ref[...])
    ffn = shared + jnp.concatenate([lo, hi], axis=1)
    out_ref[...] = _layer_norm(alpha * h + ffn, lng_ref[...], lnb_ref[...])


def _final(h2, routed, w_sgu, w_sd, ln_g, ln_b, alpha, out_prev, row0, total_rows):
    T, D = h2.shape
    tm = FINAL_TILE
    assert T % tm == 0 and row0 % tm == 0
    tile0 = row0 // tm
    operands = [h2, routed, w_sgu, w_sd, ln_g.reshape(1, D), ln_b.reshape(1, D)]
    in_specs = [pl.BlockSpec((tm, D), lambda i: (i, 0)),
                pl.BlockSpec((tm, HALF), lambda i: (i, 0)),
                _const_spec((D, 2 * SHARED_FF)), _const_spec((SHARED_FF, D)),
                _const_spec((1, D)), _const_spec((1, D))]
    aliases = {}
    if out_prev is not None:
        aliases = {len(operands): 0}
        operands.append(out_prev)
        in_specs.append(pl.BlockSpec(memory_space=pl.ANY))
    return pl.pallas_call(
        functools.partial(_final_kernel, alpha=alpha),
        name="final",
        grid=(T // tm,),
        in_specs=in_specs,
        out_specs=pl.BlockSpec((tm, D), lambda i: (i + tile0, 0)),
        out_shape=jax.ShapeDtypeStruct((total_rows, D), F32),
        input_output_aliases=aliases,
        compiler_params=pltpu.CompilerParams(
            dimension_semantics=("arbitrary",), vmem_limit_bytes=VMEM_LIMIT),
    )(*operands)


def _route_and_dispatch(h2, words, w_router, router_bias):
    T, D = h2.shape
    R = EXPERT_ROWS
    eid, pos, gates, counts = _router(h2, w_router, router_bias)
    counts = counts.reshape(N_EXPERTS).astype(I32)
    padded = (counts + (R - 1)) // R * R
    ends = jnp.cumsum(padded)
    offsets = ends - padded
    n_rows = T * TOP_K + N_EXPERTS * R
    tile_start = jnp.arange(n_rows // R, dtype=I32) * R
    tile_block = jnp.minimum(tile_start, jnp.maximum(ends[-1] - R, 0)) // R
    past = ((tile_block * R)[:, None] >= ends[None, :]).astype(I32)
    tile_expert = jnp.minimum(jnp.sum(past, axis=1), N_EXPERTS - 1)
    mine = (tile_expert[:, None] == jnp.arange(N_EXPERTS, dtype=I32)[None, :]).astype(I32)
    valid_end = jnp.sum(mine * (offsets + counts)[None, :], axis=1)
    tile_rows = jnp.clip(valid_end - tile_start, 0, R).astype(I32)
    slots = _slots(offsets.astype(I32), eid, pos)

    def chunked(a):
        return a.reshape(TOP_K, T // SC_CHUNK, SC_CHUNK).transpose(1, 0, 2)

    slots = chunked(slots)
    xs = _sc_dispatch(words, slots, n_rows)
    return xs, slots, chunked(gates), (tile_expert, tile_rows, tile_block.astype(I32))


def kernel(x, w_in, w_gate_up, b_gate, gla_norm_g, w_gla_up, w_pool_grp, pool_scale, w_pool_up,
           w_out, ln1_g, ln1_b, w_router, router_bias, w_exp_gate, w_exp_up, w_exp_down,
           w_sh_gate, w_sh_up, w_sh_down, ln2_g, ln2_b):
    B, S, D = x.shape
    depth = w_in.shape[0]
    alpha = (2.0 * depth) ** 0.25
    n_groups = TOKEN_GROUPS if B % TOKEN_GROUPS == 0 else 1
    gb = B // n_groups
    gt = gb * S
    h = x
    for l in range(depth):
        mix_ops = _mixer_operands(w_in[l], w_gate_up[l], b_gate[l], gla_norm_g[l], w_gla_up[l],
                                  w_pool_grp[l], pool_scale[l], w_pool_up[l], w_out[l],
                                  ln1_g[l], ln1_b[l])
        w_sgu = jnp.concatenate([w_sh_gate[l], w_sh_up[l]], axis=-1).astype(BF16)
        w_sd = w_sh_down[l].astype(BF16)
        staged = []
        for g in range(n_groups):
            hg, words = _mixer(h, mix_ops, alpha, g * gb, gb)
            hg = hg.reshape(gt, D)
            staged.append((hg,) + _route_and_dispatch(hg, words.reshape(gt, HALF),
                                                      w_router[l], router_bias[l]))
        out = None
        for g, (hg, xs, slots, gates, tiles) in enumerate(staged):
            ys = _experts(xs, *tiles, w_exp_gate[l], w_exp_up[l], w_exp_down[l])
            routed = _sc_combine(ys, slots, gates, gt)
            out = _final(hg, routed, w_sgu, w_sd, ln2_g[l], ln2_b[l], alpha, out, g * gt, B * S)
        h = out.reshape(B, S, D)
    return h
```

```python
import functools

import jax
import jax.numpy as jnp
from jax import lax
from jax.experimental import pallas as pl
from jax.experimental.pallas import tpu as pltpu
from jax.experimental.pallas import tpu_sc as plsc

F32 = jnp.float32
BF16 = jnp.bfloat16
I32 = jnp.int32
U32 = jnp.uint32

D_MODEL = 1024
GLA_HEADS = 4
GLA_DK = D_MODEL // 2
GLA_DV = D_MODEL
HEAD_K = GLA_DK // GLA_HEADS
HEAD_V = GLA_DV // GLA_HEADS
GATE_RANK = 16
GATE_TEMP = 16.0
POOL_WIDTH = D_MODEL // 2
POOL_GROUPS = 4
POOL_GROUP_DIM = POOL_WIDTH // POOL_GROUPS
POOL_WINDOWS = (2, 4, 8, 16)
POOL_HALO = 16
N_EXPERTS = 64
TOP_K = 8
N_GROUPS = 8
TOP_GROUPS = 4
EXPERTS_PER_GROUP = N_EXPERTS // N_GROUPS
EXPERT_FF = 256
SHARED_FF = 256
ROUTE_SCALE = 2.5
LN_EPS = 1e-5
RMS_EPS = 1e-6
LANES = 128
SUBLANES = 8

GLA_CHUNK = 256
DECAY_LIMIT = 60.0
MIX_TILE = 512
ROUTER_TILE = 512
FINAL_TILE = 1024
EXPERT_ROWS = 1024
TOKEN_GROUPS = 1
SC_CORES = 2
SC_SUBCORES = 16
SC_LANES = 16
SC_CHUNK = 64
SC_SUM_CHUNK = 8
HALF = D_MODEL // 2
VMEM_LIMIT = 56 * 1024 * 1024

_dot = functools.partial(jnp.dot, preferred_element_type=F32)
_NT = (((1,), (1,)), ((), ()))
_TN = (((0,), (0,)), ((), ()))


def _layer_norm(y, g, b):
    mu = jnp.mean(y, axis=-1, keepdims=True)
    yc = y - mu
    var = jnp.mean(yc * yc, axis=-1, keepdims=True)
    return yc * lax.rsqrt(var + LN_EPS) * g + b


def _split3(a):
    hi = a.astype(BF16)
    r1 = a - hi.astype(F32)
    mid = r1.astype(BF16)
    lo = (r1 - mid.astype(F32)).astype(BF16)
    return hi, mid, lo


def _pack_rows(y):
    lo = lax.bitcast_convert_type(y[:, :HALF].astype(BF16).astype(F32), U32)
    hi = lax.bitcast_convert_type(y[:, HALF:].astype(BF16).astype(F32), U32)
    return lax.bitcast_convert_type(hi | (lo >> 16), I32)


def _unpack_rows(w):
    u = lax.bitcast_convert_type(w, U32)
    lo = lax.bitcast_convert_type(u << 16, F32)
    hi = lax.bitcast_convert_type(u & jnp.uint32(0xFFFF0000), F32)
    return lo, hi


def _mixer_kernel(x_ref, wqk_ref, wv_ref, wr_ref, wglr_ref, wgup_ref, bgate_ref, wu_ref,
                  wga_ref, wgb_ref, gnorm_ref, wglaup_ref, wpgrp_ref, pscale_ref, wpup_ref,
                  wout_ref, lng_ref, lnb_ref, out_ref, words_ref, state_ref, ucat_ref,
                  qk_s, v_s, b_s, *, alpha, tm):
    j = pl.program_id(1)

    @pl.when(j == 0)
    def _():
        state_ref[...] = jnp.zeros_like(state_ref)
        ucat_ref[0:POOL_HALO, :] = jnp.zeros((POOL_HALO, POOL_WIDTH), F32)

    C = GLA_CHUNK
    causal = lax.broadcasted_iota(I32, (C, C), 0) >= lax.broadcasted_iota(I32, (C, C), 1)
    tri = causal.astype(BF16)
    gnorm = gnorm_ref[...]
    states = [state_ref[h] for h in range(GLA_HEADS)]

    n_chunks = tm // C

    def finish(rows, x, intra, inter, r, gate_a, gate_b, y_pool):
        heads = []
        for h in range(GLA_HEADS):
            o = intra[h] + inter[h]
            ms = jnp.mean(o * o, axis=-1, keepdims=True)
            heads.append(o * lax.rsqrt(ms + RMS_EPS) * gnorm)
        o = jnp.concatenate(heads, axis=1) * (r * jax.nn.sigmoid(r))
        y_gla = _dot(o.astype(BF16), wglaup_ref[...])
        merged = jax.nn.sigmoid(gate_a) * y_gla + jax.nn.sigmoid(gate_b) * y_pool
        mix = _dot(merged.astype(BF16), wout_ref[...])
        y = _layer_norm(alpha * x + mix, lng_ref[...], lnb_ref[...])
        out_ref[0, rows, :] = y
        words_ref[0, rows, :] = _pack_rows(y)

    kept = []
    for c in range(n_chunks):
        rows = pl.ds(c * C, C)
        x = x_ref[0, rows, :]
        xb = x.astype(BF16)

        qk = _dot(xb, wqk_ref[...])
        v = _dot(xb, wv_ref[...])
        g_lr = _dot(xb, wglr_ref[...])
        z = _dot(g_lr.astype(BF16), wgup_ref[...]) + bgate_ref[...]
        u = _dot(xb, wu_ref[...])
        r = _dot(xb, wr_ref[...])
        log_decay = (jnp.minimum(z, 0.0) - jnp.log1p(jnp.exp(-jnp.abs(z)))) * (1.0 / GATE_TEMP)
        g_hi, g_mid, _ = _split3(log_decay)
        gate_a = _dot(xb, wga_ref[...])
        bcum = _dot(tri, g_hi) + _dot(tri, g_mid)

        ucat_ref[POOL_HALO:, :] = u
        pos = j * tm + c * C - POOL_HALO + lax.broadcasted_iota(I32, (C + POOL_HALO, 1), 0)

        def pool_group(gi):
            window = POOL_WINDOWS[gi]
            a = ucat_ref[:, gi * POOL_GROUP_DIM:(gi + 1) * POOL_GROUP_DIM]
            s = a
            step = 1
            while step < window:
                s = s + pltpu.roll(s, step, 0)
                step *= 2
            count = jnp.clip(pos + 1, 1, window).astype(F32)
            pooled = (s / count - a)[POOL_HALO:, :]
            return _dot(pooled.astype(BF16), wpgrp_ref[gi])

        intra, inter, mixed = [], [], []
        for h in range(GLA_HEADS):
            ks = slice(h * HEAD_K, (h + 1) * HEAD_K)
            b = bcum[:, ks]
            b_last = b[C - 1:C, :]
            q_h = qk[:, ks] * (HEAD_K ** -0.5)
            k_h = qk[:, GLA_DK + h * HEAD_K:GLA_DK + (h + 1) * HEAD_K]
            v_h = v[:, h * HEAD_V:(h + 1) * HEAD_V].astype(BF16)
            q_in = (q_h * jnp.exp(b)).astype(BF16)
            k_out = (k_h * jnp.exp(-b)).astype(BF16)
            k_end = (k_h * jnp.exp(b_last - b)).astype(BF16)
            scores = lax.dot_general(q_in, k_out, _NT, preferred_element_type=F32)
            scores = jnp.where(causal, scores, 0.0).astype(BF16)
            st = states[h]
            intra.append(_dot(scores, v_h))
            inter.append(lax.dot_general(q_in, st.astype(BF16), _NT, preferred_element_type=F32))
            states[h] = st * jnp.exp(b_last) + lax.dot_general(
                v_h, k_end, _TN, preferred_element_type=F32)
            mixed.append(pool_group(h))
            if h == 1:
                gate_b = _dot(xb, wgb_ref[...])
        ucat_ref[0:POOL_HALO, :] = u[C - POOL_HALO:, :]
        mixed = jnp.concatenate(mixed, axis=1) * pscale_ref[...]
        y_pool = _dot(mixed.astype(BF16), wpup_ref[...])
        finish(rows, x, intra, inter, r, gate_a, gate_b, y_pool)
        kept.append((qk, v, bcum, inter, y_pool))

    for h in range(GLA_HEADS):
        state_ref[h] = states[h]

    steepest = kept[0][2][C - 1:C, :]
    for c in range(1, n_chunks):
        steepest = jnp.minimum(steepest, kept[c][2][C - 1:C, :])

    @pl.when(jnp.max(-steepest) > DECAY_LIMIT)
    def _():
        row_id = lax.broadcasted_iota(I32, (C, 1), 0)
        for c in range(n_chunks):
            rows = pl.ds(c * C, C)
            qk, v, bcum, inter, y_pool = kept[c]
            qk_s[...] = qk
            v_s[...] = v
            b_s[...] = bcum
            intra = []
            for h in range(GLA_HEADS):
                ks = slice(h * HEAD_K, (h + 1) * HEAD_K)
                kks = slice(GLA_DK + h * HEAD_K, GLA_DK + (h + 1) * HEAD_K)
                vs = slice(h * HEAD_V, (h + 1) * HEAD_V)
                q_h = qk[:, ks] * (HEAD_K ** -0.5)
                b_h = bcum[:, ks]

                def add_rows(group, acc):
                    first = pl.multiple_of(group * SUBLANES, SUBLANES)
                    b_g = b_s[pl.ds(first, SUBLANES), ks]
                    k_g = qk_s[pl.ds(first, SUBLANES), kks]
                    v_g = v_s[pl.ds(first, SUBLANES), vs]
                    for i in range(SUBLANES):
                        decay = jnp.exp(jnp.minimum(b_h - b_g[i:i + 1, :], 0.0))
                        decay = jnp.where(row_id >= first + i, decay, 0.0)
                        score = jnp.sum(q_h * k_g[i:i + 1, :] * decay, axis=1, keepdims=True)
                        acc = acc + score * v_g[i:i + 1, :]
                    return acc

                intra.append(lax.fori_loop(0, C // SUBLANES, add_rows, jnp.zeros((C, HEAD_V), F32)))
            x = x_ref[0, rows, :]
            xb = x.astype(BF16)
            finish(rows, x, intra, inter, _dot(xb, wr_ref[...]), _dot(xb, wga_ref[...]),
                   _dot(xb, wgb_ref[...]), y_pool)


def _const_spec(shape):
    zeros = (0,) * len(shape)
    return pl.BlockSpec(shape, lambda *_: zeros, pipeline_mode=pl.Buffered(1))


def _mixer_operands(w_in, w_gate_up, b_gate, gla_norm_g, w_gla_up, w_pool_grp, pool_scale,
                    w_pool_up, w_out, ln_g, ln_b):
    D = D_MODEL
    c = 0
    w_qk = w_in[:, c:c + 2 * GLA_DK].astype(BF16); c += 2 * GLA_DK
    w_v = w_in[:, c:c + GLA_DV].astype(BF16); c += GLA_DV
    w_r = w_in[:, c:c + GLA_DV].astype(BF16); c += GLA_DV
    w_glr = jnp.pad(w_in[:, c:c + GATE_RANK], ((0, 0), (0, LANES - GATE_RANK))).astype(BF16); c += GATE_RANK
    w_u = w_in[:, c:c + POOL_WIDTH].astype(BF16); c += POOL_WIDTH
    w_ga = w_in[:, c:c + D].astype(BF16); c += D
    w_gb = w_in[:, c:c + D].astype(BF16); c += D
    assert c == w_in.shape[1]
    w_gup = jnp.pad(w_gate_up, ((0, LANES - GATE_RANK), (0, 0))).astype(BF16)
    return (
        w_qk, w_v, w_r, w_glr, w_gup, b_gate.reshape(1, GLA_DK), w_u, w_ga, w_gb,
        gla_norm_g.reshape(1, HEAD_V), w_gla_up.astype(BF16), w_pool_grp.astype(BF16),
        pool_scale.reshape(1, POOL_WIDTH), w_pool_up.astype(BF16), w_out.astype(BF16),
        ln_g.reshape(1, D), ln_b.reshape(1, D))


def _mixer(x, operands, alpha, batch0, n_batch):
    _, S, D = x.shape
    tm = MIX_TILE
    assert D == D_MODEL and S % tm == 0 and tm % GLA_CHUNK == 0
    return pl.pallas_call(
        functools.partial(_mixer_kernel, alpha=alpha, tm=tm),
        name="mixer",
        grid=(n_batch, S // tm),
        in_specs=[pl.BlockSpec((1, tm, D), lambda b, j: (b + batch0, j, 0))]
        + [_const_spec(w.shape) for w in operands],
        out_specs=[pl.BlockSpec((1, tm, D), lambda b, j: (b, j, 0)),
                   pl.BlockSpec((1, tm, HALF), lambda b, j: (b, j, 0))],
        out_shape=[jax.ShapeDtypeStruct((n_batch, S, D), F32),
                   jax.ShapeDtypeStruct((n_batch, S, HALF), I32)],
        scratch_shapes=[
            pltpu.VMEM((GLA_HEADS, HEAD_V, HEAD_K), F32),
            pltpu.VMEM((GLA_CHUNK + POOL_HALO, POOL_WIDTH), F32),
            pltpu.VMEM((GLA_CHUNK, 2 * GLA_DK), F32),
            pltpu.VMEM((GLA_CHUNK, GLA_DV), F32),
            pltpu.VMEM((GLA_CHUNK, GLA_DK), F32),
        ],
        compiler_params=pltpu.CompilerParams(
            dimension_semantics=("arbitrary", "arbitrary"), vmem_limit_bytes=VMEM_LIMIT),
    )(x, *operands)


def _rank_desc(vals, n):
    idx = lax.broadcasted_iota(I32, vals.shape, 0)
    rank = jnp.zeros(vals.shape, I32)
    for other in range(n):
        o = vals[other:other + 1, :]
        before = (o > vals) | ((o == vals) & (other < idx))
        rank = rank + before.astype(I32)
    return rank


def _router_kernel(h_ref, wrt_ref, bias_ref, eid_ref, pos_ref, wk_ref, cnt_ref, carry_ref):
    i = pl.program_id(0)

    @pl.when(i == 0)
    def _():
        carry_ref[...] = jnp.zeros_like(carry_ref)

    h = h_ref[...]
    tr = h.shape[0]
    w_hi, w_mid, _ = _split3(wrt_ref[...])
    h_hi, h_mid, _ = _split3(h)
    by_hi = lax.dot_general(jnp.concatenate([w_hi, w_mid], axis=0), h_hi, _NT,
                            preferred_element_type=F32)
    logits = (by_hi[:N_EXPERTS] + by_hi[N_EXPERTS:]
              + lax.dot_general(w_hi, h_mid, _NT, preferred_element_type=F32))
    scores = jax.nn.sigmoid(logits)
    biased = scores + bias_ref[...]
    grp = biased.reshape(N_GROUPS, EXPERTS_PER_GROUP, tr)
    slot = lax.broadcasted_iota(I32, grp.shape, 1)
    top1 = jnp.max(grp, axis=1, keepdims=True)
    first = jnp.min(jnp.where(grp == top1, slot, EXPERTS_PER_GROUP), axis=1, keepdims=True)
    top2 = jnp.max(jnp.where(slot == first, -jnp.inf, grp), axis=1)
    grp_score = top1[:, 0, :] + top2
    grp_on = _rank_desc(grp_score, N_GROUPS) < TOP_GROUPS
    emask = jnp.broadcast_to(grp_on[:, None, :], grp.shape).reshape(N_EXPERTS, tr)
    masked = jnp.where(emask, biased, -jnp.inf)

    eidx = lax.broadcasted_iota(I32, (N_EXPERTS, tr), 0).astype(F32)
    rest = masked
    eids, sels = [], []
    for _ in range(TOP_K):
        top = jnp.max(rest, axis=0, keepdims=True)
        eid = jnp.min(jnp.where(rest == top, eidx, float(N_EXPERTS)), axis=0, keepdims=True)
        hit = eidx == eid
        rest = jnp.where(hit, -jnp.inf, rest)
        eids.append(eid)
        sels.append(jnp.sum(jnp.where(hit, scores, 0.0), axis=0, keepdims=True))
    eid_k = jnp.concatenate(eids, axis=0)
    sel_k = jnp.concatenate(sels, axis=0)
    gate_k = sel_k / jnp.sum(sel_k, axis=0, keepdims=True) * ROUTE_SCALE
    chosen = rest != masked

    t_row = lax.broadcasted_iota(I32, (tr, tr), 0)
    t_col = lax.broadcasted_iota(I32, (tr, tr), 1)
    earlier = (t_row < t_col).astype(BF16)
    pos = carry_ref[...] + _dot(chosen.astype(BF16), earlier)
    carry_ref[...] += jnp.sum(chosen.astype(F32), axis=1, keepdims=True)
    cnt_ref[...] = carry_ref[...]

    poss = [jnp.sum(jnp.where(eidx == eid, pos, 0.0), axis=0, keepdims=True) for eid in eids]
    eid_ref[...] = eid_k.astype(I32)
    pos_ref[...] = jnp.concatenate(poss, axis=0).astype(I32)
    wk_ref[...] = gate_k


def _router(h2, w_router, router_bias):
    T, D = h2.shape
    tr = ROUTER_TILE
    assert T % tr == 0
    return pl.pallas_call(
        _router_kernel,
        name="router",
        grid=(T // tr,),
        in_specs=[pl.BlockSpec((tr, D), lambda i: (i, 0)),
                  _const_spec((N_EXPERTS, D)), _const_spec((N_EXPERTS, 1))],
        out_specs=[pl.BlockSpec((TOP_K, tr), lambda i: (0, i)),
                   pl.BlockSpec((TOP_K, tr), lambda i: (0, i)),
                   pl.BlockSpec((TOP_K, tr), lambda i: (0, i)),
                   pl.BlockSpec((N_EXPERTS, 1), lambda i: (0, 0))],
        out_shape=[jax.ShapeDtypeStruct((TOP_K, T), I32),
                   jax.ShapeDtypeStruct((TOP_K, T), I32),
                   jax.ShapeDtypeStruct((TOP_K, T), F32),
                   jax.ShapeDtypeStruct((N_EXPERTS, 1), F32)],
        scratch_shapes=[pltpu.VMEM((N_EXPERTS, 1), F32)],
        compiler_params=pltpu.CompilerParams(
            dimension_semantics=("arbitrary",), vmem_limit_bytes=VMEM_LIMIT),
    )(h2, w_router.T, router_bias.reshape(N_EXPERTS, 1))


def _slot_kernel(offs_ref, eid_ref, pos_ref, slot_ref):
    eid = eid_ref[...]
    base = jnp.zeros(eid.shape, I32)
    for e in range(N_EXPERTS):
        base = jnp.where(eid == e, offs_ref[e], base)
    slot_ref[...] = pos_ref[...] + base


def _slots(offsets, eid, pos):
    shape = eid.shape
    return pl.pallas_call(
        _slot_kernel,
        name="slots",
        grid_spec=pltpu.PrefetchScalarGridSpec(
            num_scalar_prefetch=1, grid=(1,),
            in_specs=[pl.BlockSpec(shape, lambda i, offs: (0, 0)),
                      pl.BlockSpec(shape, lambda i, offs: (0, 0))],
            out_specs=pl.BlockSpec(shape, lambda i, offs: (0, 0))),
        out_shape=jax.ShapeDtypeStruct(shape, I32),
        compiler_params=pltpu.CompilerParams(vmem_limit_bytes=VMEM_LIMIT),
    )(offsets, eid, pos)


def _sc_mesh():
    return plsc.VectorSubcoreMesh(core_axis_name="core", subcore_axis_name="subcore",
                                  num_cores=SC_CORES, num_subcores=SC_SUBCORES)


def _sc_worker():
    return lax.axis_index("subcore") * SC_CORES + lax.axis_index("core")


def _sc_dispatch(words, slots, n_rows):
    T = words.shape[0]
    n_chunks = slots.shape[0]
    workers = SC_CORES * SC_SUBCORES
    assert T == n_chunks * SC_CHUNK and n_chunks % workers == 0
    per_worker = n_chunks // workers

    @functools.partial(
        pl.kernel, mesh=_sc_mesh(),
        out_type=jax.ShapeDtypeStruct((n_rows, HALF), I32),
        scratch_types=[pltpu.VMEM((SC_CHUNK, HALF), I32), pltpu.VMEM((TOP_K, SC_CHUNK), I32),
                       pltpu.SemaphoreType.DMA])
    def dispatch(words_hbm, slots_hbm, out_hbm, rows_v, idx_v, sem):
        first = _sc_worker() * per_worker

        @pl.loop(0, per_worker)
        def _(i):
            c = first + i
            pltpu.sync_copy(slots_hbm.at[c], idx_v)
            pltpu.sync_copy(words_hbm.at[pl.ds(c * SC_CHUNK, SC_CHUNK)], rows_v)
            copies = [pltpu.async_copy(rows_v, out_hbm.at[idx_v.at[k]], sem) for k in range(TOP_K)]
            for cp in copies:
                cp.wait()

    return dispatch(words, slots)


def _sc_combine(rows, slots, gates, T):
    CH = SC_SUM_CHUNK
    steps = SC_CHUNK // CH
    n_blocks = slots.shape[0]
    workers = SC_CORES * SC_SUBCORES
    assert T == n_blocks * SC_CHUNK and n_blocks % workers == 0 and steps >= 2
    per_worker = n_blocks // workers
    vecs = HALF // SC_LANES

    @functools.partial(
        pl.kernel, mesh=_sc_mesh(),
        compiler_params=pltpu.CompilerParams(needs_layout_passes=False),
        out_type=jax.ShapeDtypeStruct((T, HALF), I32),
        scratch_types=[pltpu.VMEM((2, TOP_K, CH, HALF), I32), pltpu.VMEM((2, CH, HALF), I32),
                       pltpu.VMEM((TOP_K, SC_CHUNK), I32), pltpu.VMEM((TOP_K, SC_CHUNK), F32),
                       pltpu.SemaphoreType.DMA((2,)), pltpu.SemaphoreType.DMA((2,))])
    def combine(rows_hbm, slots_hbm, gates_hbm, out_hbm, rows_v, sum_v, idx_v, gates_v, sem_in, sem_out):
        first = _sc_worker() * per_worker

        def gathers(j, b):
            return [pltpu.make_async_copy(rows_hbm.at[idx_v.at[k, pl.ds(j * CH, CH)]], rows_v.at[b, k],
                                          sem_in.at[b]) for k in range(TOP_K)]

        def write_out(token0, b):
            return pltpu.make_async_copy(sum_v.at[b], out_hbm.at[pl.ds(token0, CH)], sem_out.at[b])

        def add_rows(j, b):
            @pl.loop(0, CH)
            def _(t):
                token = jnp.full((SC_LANES,), j * CH + t, I32)
                gate = []
                for k in range(TOP_K):
                    g = plsc.load_gather(gates_v.at[k], [token])
                    gate.append(plsc.pack(g, g, format=plsc.PackFormat.INTERLEAVED))

                @plsc.parallel_loop(0, vecs, 1, unroll=2)
                def _(n):
                    lanes = pl.ds(n * SC_LANES, SC_LANES)
                    terms = [plsc.bitcast(rows_v[b, k, t, lanes], BF16) * gate[k] for k in range(TOP_K)]
                    while len(terms) > 1:
                        terms = [x + y for x, y in zip(terms[::2], terms[1::2])]
                    sum_v[b, t, lanes] = plsc.bitcast(terms[0], I32)

        @pl.loop(0, per_worker)
        def _(i):
            block = first + i
            token0 = block * SC_CHUNK
            pltpu.sync_copy(slots_hbm.at[block], idx_v)
            pltpu.sync_copy(gates_hbm.at[block], gates_v)
            for cp in gathers(0, 0):
                cp.start()
            for j in range(steps):
                b = j % 2
                if j + 1 < steps:
                    for cp in gathers(j + 1, 1 - b):
                        cp.start()
                for cp in gathers(j, b):
                    cp.wait()
                if j >= 2:
                    write_out(token0 + (j - 2) * CH, b).wait()
                add_rows(j, b)
                write_out(token0 + j * CH, b).start()
            write_out(token0 + (steps - 2) * CH, steps % 2).wait()
            write_out(token0 + (steps - 1) * CH, (steps - 1) % 2).wait()

    return combine(rows, slots, gates)


def _expert_kernel(tile_expert_ref, tile_rows_ref, tile_block_ref, xs_ref, wg_ref, wu_ref, wd_ref,
                   ys_ref, wgu_ref, wdn_ref):
    del tile_block_ref
    i = pl.program_id(0)
    e = tile_expert_ref[i]
    n_valid = tile_rows_ref[i]
    e_prev = tile_expert_ref[jnp.maximum(i - 1, 0)]

    @pl.when((i == 0) | (e != e_prev))
    def _():
        wgu_ref[:, :EXPERT_FF] = wg_ref[0].astype(BF16)
        wgu_ref[:, EXPERT_FF:] = wu_ref[0].astype(BF16)
        wdn_ref[...] = wd_ref[0].astype(BF16)

    @pl.when(n_valid > 0)
    def _():
        words = xs_ref[...]
        live = lax.broadcasted_iota(I32, (words.shape[0], 1), 0) < n_valid
        lo, hi = _unpack_rows(jnp.where(live, words, 0))
        a = _dot(lo.astype(BF16), wgu_ref[:HALF, :]) + _dot(hi.astype(BF16), wgu_ref[HALF:, :])
        act = a[:, :EXPERT_FF]
        hid = act * jax.nn.sigmoid(act) * a[:, EXPERT_FF:]
        ys_ref[...] = _pack_rows(_dot(hid.astype(BF16), wdn_ref[...]))


def _experts(xs, tile_expert, tile_rows, tile_block, w_exp_gate, w_exp_up, w_exp_down):
    n_rows = xs.shape[0]
    R = EXPERT_ROWS
    D = D_MODEL
    assert n_rows % R == 0
    return pl.pallas_call(
        _expert_kernel,
        name="experts",
        grid_spec=pltpu.PrefetchScalarGridSpec(
            num_scalar_prefetch=3, grid=(n_rows // R,),
            in_specs=[pl.BlockSpec((R, HALF), lambda i, te, tr, tb: (tb[i], 0)),
                      pl.BlockSpec((1, D, EXPERT_FF), lambda i, te, tr, tb: (te[i], 0, 0)),
                      pl.BlockSpec((1, D, EXPERT_FF), lambda i, te, tr, tb: (te[i], 0, 0)),
                      pl.BlockSpec((1, EXPERT_FF, D), lambda i, te, tr, tb: (te[i], 0, 0))],
            out_specs=pl.BlockSpec((R, HALF), lambda i, te, tr, tb: (tb[i], 0)),
            scratch_shapes=[pltpu.VMEM((D, 2 * EXPERT_FF), BF16), pltpu.VMEM((EXPERT_FF, D), BF16)]),
        out_shape=jax.ShapeDtypeStruct((n_rows, HALF), I32),
        compiler_params=pltpu.CompilerParams(
            dimension_semantics=("arbitrary",), vmem_limit_bytes=VMEM_LIMIT),
    )(tile_expert, tile_rows, tile_block, xs, w_exp_gate, w_exp_up, w_exp_down)


def _final_kernel(h_ref, routed_ref, wsgu_ref, wsd_ref, lng_ref, lnb_ref, *rest, alpha):
    out_ref = rest[-1]
    h = h_ref[...]
    a = _dot(h.astype(BF16), wsgu_ref[...])
    act = a[:, :SHARED_FF]
    hid = act * jax.nn.sigmoid(act) * a[:, SHARED_FF:]
    shared = _dot(hid.astype(BF16), wsd_ref[...])
    lo, hi = _unpack_rows(routed_ref[...])
    ffn = shared + jnp.concatenate([lo, hi], axis=1)
    out_ref[...] = _layer_norm(alpha * h + ffn, lng_ref[...], lnb_ref[...])


def _final(h2, routed, w_sgu, w_sd, ln_g, ln_b, alpha, out_prev, row0, total_rows):
    T, D = h2.shape
    tm = FINAL_TILE
    assert T % tm == 0 and row0 % tm == 0
    tile0 = row0 // tm
    operands = [h2, routed, w_sgu, w_sd, ln_g.reshape(1, D), ln_b.reshape(1, D)]
    in_specs = [pl.BlockSpec((tm, D), lambda i: (i, 0)),
                pl.BlockSpec((tm, HALF), lambda i: (i, 0)),
                _const_spec((D, 2 * SHARED_FF)), _const_spec((SHARED_FF, D)),
                _const_spec((1, D)), _const_spec((1, D))]
    aliases = {}
    if out_prev is not None:
        aliases = {len(operands): 0}
        operands.append(out_prev)
        in_specs.append(pl.BlockSpec(memory_space=pl.ANY))
    return pl.pallas_call(
        functools.partial(_final_kernel, alpha=alpha),
        name="final",
        grid=(T // tm,),
        in_specs=in_specs,
        out_specs=pl.BlockSpec((tm, D), lambda i: (i + tile0, 0)),
        out_shape=jax.ShapeDtypeStruct((total_rows, D), F32),
        input_output_aliases=aliases,
        compiler_params=pltpu.CompilerParams(
            dimension_semantics=("arbitrary",), vmem_limit_bytes=VMEM_LIMIT),
    )(*operands)


def _route_and_dispatch(h2, words, w_router, router_bias):
    T, D = h2.shape
    R = EXPERT_ROWS
    eid, pos, gates, counts = _router(h2, w_router, router_bias)
    counts = counts.reshape(N_EXPERTS).astype(I32)
    padded = (counts + (R - 1)) // R * R
    ends = jnp.cumsum(padded)
    offsets = ends - padded
    n_rows = T * TOP_K + N_EXPERTS * R
    tile_start = jnp.arange(n_rows // R, dtype=I32) * R
    tile_block = jnp.minimum(tile_start, jnp.maximum(ends[-1] - R, 0)) // R
    past = ((tile_block * R)[:, None] >= ends[None, :]).astype(I32)
    tile_expert = jnp.minimum(jnp.sum(past, axis=1), N_EXPERTS - 1)
    mine = (tile_expert[:, None] == jnp.arange(N_EXPERTS, dtype=I32)[None, :]).astype(I32)
    valid_end = jnp.sum(mine * (offsets + counts)[None, :], axis=1)
    tile_rows = jnp.clip(valid_end - tile_start, 0, R).astype(I32)
    slots = _slots(offsets.astype(I32), eid, pos)

    def chunked(a):
        return a.reshape(TOP_K, T // SC_CHUNK, SC_CHUNK).transpose(1, 0, 2)

    slots = chunked(slots)
    xs = _sc_dispatch(words, slots, n_rows)
    return xs, slots, chunked(gates), (tile_expert, tile_rows, tile_block.astype(I32))


def kernel(x, w_in, w_gate_up, b_gate, gla_norm_g, w_gla_up, w_pool_grp, pool_scale, w_pool_up,
           w_out, ln1_g, ln1_b, w_router, router_bias, w_exp_gate, w_exp_up, w_exp_down,
           w_sh_gate, w_sh_up, w_sh_down, ln2_g, ln2_b):
    B, S, D = x.shape
    depth = w_in.shape[0]
    alpha = (2.0 * depth) ** 0.25
    n_groups = TOKEN_GROUPS if B % TOKEN_GROUPS == 0 else 1
    gb = B // n_groups
    gt = gb * S
    h = x
    for l in range(depth):
        mix_ops = _mixer_operands(w_in[l], w_gate_up[l], b_gate[l], gla_norm_g[l], w_gla_up[l],
                                  w_pool_grp[l], pool_scale[l], w_pool_up[l], w_out[l],
                                  ln1_g[l], ln1_b[l])
        w_sgu = jnp.concatenate([w_sh_gate[l], w_sh_up[l]], axis=-1).astype(BF16)
        w_sd = w_sh_down[l].astype(BF16)
        staged = []
        for g in range(n_groups):
            hg, words = _mixer(h, mix_ops, alpha, g * gb, gb)
            hg = hg.reshape(gt, D)
            staged.append((hg,) + _route_and_dispatch(hg, words.reshape(gt, HALF),
                                                      w_router[l], router_bias[l]))
        out = None
        for g, (hg, xs, slots, gates, tiles) in enumerate(staged):
            ys = _experts(xs, *tiles, w_exp_gate[l], w_exp_up[l], w_exp_down[l])
            routed = _sc_combine(ys, slots, gates, gt)
            out = _final(hg, routed, w_sgu, w_sd, ln2_g[l], ln2_b[l], alpha, out, g * gt, B * S)
        h = out.reshape(B, S, D)
    return h
```

```python
import functools

import jax
import jax.numpy as jnp
from jax import lax
from jax.experimental import pallas as pl
from jax.experimental.pallas import tpu as pltpu
from jax.experimental.pallas import tpu_sc as plsc

F32 = jnp.float32
BF16 = jnp.bfloat16
I32 = jnp.int32
U32 = jnp.uint32

D_MODEL = 1024
GLA_HEADS = 4
GLA_DK = D_MODEL // 2
GLA_DV = D_MODEL
HEAD_K = GLA_DK // GLA_HEADS
HEAD_V = GLA_DV // GLA_HEADS
GATE_RANK = 16
GATE_TEMP = 16.0
POOL_WIDTH = D_MODEL // 2
POOL_GROUPS = 4
POOL_GROUP_DIM = POOL_WIDTH // POOL_GROUPS
POOL_WINDOWS = (2, 4, 8, 16)
POOL_HALO = 16
N_EXPERTS = 64
TOP_K = 8
N_GROUPS = 8
TOP_GROUPS = 4
EXPERTS_PER_GROUP = N_EXPERTS // N_GROUPS
EXPERT_FF = 256
SHARED_FF = 256
ROUTE_SCALE = 2.5
LN_EPS = 1e-5
RMS_EPS = 1e-6
LANES = 128
SUBLANES = 8

GLA_CHUNK = 256
DECAY_LIMIT = 60.0
MIX_TILE = 512
ROUTER_TILE = 1024
FINAL_TILE = 1024
EXPERT_ROWS = 1024
TOKEN_GROUPS = 1
SC_CORES = 2
SC_SUBCORES = 16
SC_LANES = 16
SC_CHUNK = 64
SC_SUM_CHUNK = 8
HALF = D_MODEL // 2
VMEM_LIMIT = 56 * 1024 * 1024

_dot = functools.partial(jnp.dot, preferred_element_type=F32)
_NT = (((1,), (1,)), ((), ()))
_TN = (((0,), (0,)), ((), ()))


def _layer_norm(y, g, b):
    mu = jnp.mean(y, axis=-1, keepdims=True)
    yc = y - mu
    var = jnp.mean(yc * yc, axis=-1, keepdims=True)
    return yc * lax.rsqrt(var + LN_EPS) * g + b


def _split3(a):
    hi = a.astype(BF16)
    r1 = a - hi.astype(F32)
    mid = r1.astype(BF16)
    lo = (r1 - mid.astype(F32)).astype(BF16)
    return hi, mid, lo


def _pack_rows(y):
    lo = lax.bitcast_convert_type(y[:, :HALF].astype(BF16).astype(F32), U32)
    hi = lax.bitcast_convert_type(y[:, HALF:].astype(BF16).astype(F32), U32)
    return lax.bitcast_convert_type(hi | (lo >> 16), I32)


def _unpack_rows(w):
    u = lax.bitcast_convert_type(w, U32)
    lo = lax.bitcast_convert_type(u << 16, F32)
    hi = lax.bitcast_convert_type(u & jnp.uint32(0xFFFF0000), F32)
    return lo, hi


def _mixer_kernel(x_ref, wqk_ref, wv_ref, wr_ref, wglr_ref, wgup_ref, bgate_ref, wu_ref,
                  wga_ref, wgb_ref, gnorm_ref, wglaup_ref, wpgrp_ref, pscale_ref, wpup_ref,
                  wout_ref, lng_ref, lnb_ref, out_ref, words_ref, state_ref, ucat_ref,
                  qk_s, v_s, b_s, *, alpha, tm):
    j = pl.program_id(1)

    @pl.when(j == 0)
    def _():
        state_ref[...] = jnp.zeros_like(state_ref)
        ucat_ref[0:POOL_HALO, :] = jnp.zeros((POOL_HALO, POOL_WIDTH), F32)

    C = GLA_CHUNK
    causal = lax.broadcasted_iota(I32, (C, C), 0) >= lax.broadcasted_iota(I32, (C, C), 1)
    tri = causal.astype(BF16)
    gnorm = gnorm_ref[...]
    states = [state_ref[h] for h in range(GLA_HEADS)]

    n_chunks = tm // C

    def finish(rows, x, intra, inter, r, gate_a, gate_b, y_pool):
        heads = []
        for h in range(GLA_HEADS):
            o = intra[h] + inter[h]
            ms = jnp.mean(o * o, axis=-1, keepdims=True)
            heads.append(o * lax.rsqrt(ms + RMS_EPS) * gnorm)
        o = jnp.concatenate(heads, axis=1) * (r * jax.nn.sigmoid(r))
        y_gla = _dot(o.astype(BF16), wglaup_ref[...])
        merged = jax.nn.sigmoid(gate_a) * y_gla + jax.nn.sigmoid(gate_b) * y_pool
        mix = _dot(merged.astype(BF16), wout_ref[...])
        y = _layer_norm(alpha * x + mix, lng_ref[...], lnb_ref[...])
        out_ref[0, rows, :] = y
        words_ref[0, rows, :] = _pack_rows(y)

    kept = []
    for c in range(n_chunks):
        rows = pl.ds(c * C, C)
        x = x_ref[0, rows, :]
        xb = x.astype(BF16)

        qk = _dot(xb, wqk_ref[...])
        v = _dot(xb, wv_ref[...])
        g_lr = _dot(xb, wglr_ref[...])
        z = _dot(g_lr.astype(BF16), wgup_ref[...]) + bgate_ref[...]
        u = _dot(xb, wu_ref[...])
        r = _dot(xb, wr_ref[...])
        log_decay = (jnp.minimum(z, 0.0) - jnp.log1p(jnp.exp(-jnp.abs(z)))) * (1.0 / GATE_TEMP)
        g_hi, g_mid, _ = _split3(log_decay)
        gate_a = _dot(xb, wga_ref[...])
        bcum = _dot(tri, g_hi) + _dot(tri, g_mid)

        ucat_ref[POOL_HALO:, :] = u
        pos = j * tm + c * C - POOL_HALO + lax.broadcasted_iota(I32, (C + POOL_HALO, 1), 0)

        def pool_group(gi):
            window = POOL_WINDOWS[gi]
            a = ucat_ref[:, gi * POOL_GROUP_DIM:(gi + 1) * POOL_GROUP_DIM]
            s = a
            step = 1
            while step < window:
                s = s + pltpu.roll(s, step, 0)
                step *= 2
            count = jnp.clip(pos + 1, 1, window).astype(F32)
            pooled = (s / count - a)[POOL_HALO:, :]
            return _dot(pooled.astype(BF16), wpgrp_ref[gi])

        intra, inter, mixed = [], [], []
        for h in range(GLA_HEADS):
            ks = slice(h * HEAD_K, (h + 1) * HEAD_K)
            b = bcum[:, ks]
            b_last = b[C - 1:C, :]
            q_h = qk[:, ks] * (HEAD_K ** -0.5)
            k_h = qk[:, GLA_DK + h * HEAD_K:GLA_DK + (h + 1) * HEAD_K]
            v_h = v[:, h * HEAD_V:(h + 1) * HEAD_V].astype(BF16)
            q_in = (q_h * jnp.exp(b)).astype(BF16)
            k_out = (k_h * jnp.exp(-b)).astype(BF16)
            k_end = (k_h * jnp.exp(b_last - b)).astype(BF16)
            scores = lax.dot_general(q_in, k_out, _NT, preferred_element_type=F32)
            scores = jnp.where(causal, scores, 0.0).astype(BF16)
            st = states[h]
            intra.append(_dot(scores, v_h))
            inter.append(lax.dot_general(q_in, st.astype(BF16), _NT, preferred_element_type=F32))
            states[h] = st * jnp.exp(b_last) + lax.dot_general(
                v_h, k_end, _TN, preferred_element_type=F32)
            mixed.append(pool_group(h))
            if h == 1:
                gate_b = _dot(xb, wgb_ref[...])
        ucat_ref[0:POOL_HALO, :] = u[C - POOL_HALO:, :]
        mixed = jnp.concatenate(mixed, axis=1) * pscale_ref[...]
        y_pool = _dot(mixed.astype(BF16), wpup_ref[...])
        finish(rows, x, intra, inter, r, gate_a, gate_b, y_pool)
        kept.append((qk, v, bcum, inter, y_pool))

    for h in range(GLA_HEADS):
        state_ref[h] = states[h]

    steepest = kept[0][2][C - 1:C, :]
    for c in range(1, n_chunks):
        steepest = jnp.minimum(steepest, kept[c][2][C - 1:C, :])

    @pl.when(jnp.max(-steepest) > DECAY_LIMIT)
    def _():
        row_id = lax.broadcasted_iota(I32, (C, 1), 0)
        for c in range(n_chunks):
            rows = pl.ds(c * C, C)
            qk, v, bcum, inter, y_pool = kept[c]
            qk_s[...] = qk
            v_s[...] = v
            b_s[...] = bcum
            intra = []
            for h in range(GLA_HEADS):
                ks = slice(h * HEAD_K, (h + 1) * HEAD_K)
                kks = slice(GLA_DK + h * HEAD_K, GLA_DK + (h + 1) * HEAD_K)
                vs = slice(h * HEAD_V, (h + 1) * HEAD_V)
                q_h = qk[:, ks] * (HEAD_K ** -0.5)
                b_h = bcum[:, ks]

                def add_rows(group, acc):
                    first = pl.multiple_of(group * SUBLANES, SUBLANES)
                    b_g = b_s[pl.ds(first, SUBLANES), ks]
                    k_g = qk_s[pl.ds(first, SUBLANES), kks]
                    v_g = v_s[pl.ds(first, SUBLANES), vs]
                    for i in range(SUBLANES):
                        decay = jnp.exp(jnp.minimum(b_h - b_g[i:i + 1, :], 0.0))
                        decay = jnp.where(row_id >= first + i, decay, 0.0)
                        score = jnp.sum(q_h * k_g[i:i + 1, :] * decay, axis=1, keepdims=True)
                        acc = acc + score * v_g[i:i + 1, :]
                    return acc

                intra.append(lax.fori_loop(0, C // SUBLANES, add_rows, jnp.zeros((C, HEAD_V), F32)))
            x = x_ref[0, rows, :]
            xb = x.astype(BF16)
            finish(rows, x, intra, inter, _dot(xb, wr_ref[...]), _dot(xb, wga_ref[...]),
                   _dot(xb, wgb_ref[...]), y_pool)


def _const_spec(shape):
    zeros = (0,) * len(shape)
    return pl.BlockSpec(shape, lambda *_: zeros, pipeline_mode=pl.Buffered(1))


def _mixer_operands(w_in, w_gate_up, b_gate, gla_norm_g, w_gla_up, w_pool_grp, pool_scale,
                    w_pool_up, w_out, ln_g, ln_b):
    D = D_MODEL
    c = 0
    w_qk = w_in[:, c:c + 2 * GLA_DK].astype(BF16); c += 2 * GLA_DK
    w_v = w_in[:, c:c + GLA_DV].astype(BF16); c += GLA_DV
    w_r = w_in[:, c:c + GLA_DV].astype(BF16); c += GLA_DV
    w_glr = jnp.pad(w_in[:, c:c + GATE_RANK], ((0, 0), (0, LANES - GATE_RANK))).astype(BF16); c += GATE_RANK
    w_u = w_in[:, c:c + POOL_WIDTH].astype(BF16); c += POOL_WIDTH
    w_ga = w_in[:, c:c + D].astype(BF16); c += D
    w_gb = w_in[:, c:c + D].astype(BF16); c += D
    assert c == w_in.shape[1]
    w_gup = jnp.pad(w_gate_up, ((0, LANES - GATE_RANK), (0, 0))).astype(BF16)
    return (
        w_qk, w_v, w_r, w_glr, w_gup, b_gate.reshape(1, GLA_DK), w_u, w_ga, w_gb,
        gla_norm_g.reshape(1, HEAD_V), w_gla_up.astype(BF16), w_pool_grp.astype(BF16),
        pool_scale.reshape(1, POOL_WIDTH), w_pool_up.astype(BF16), w_out.astype(BF16),
        ln_g.reshape(1, D), ln_b.reshape(1, D))


def _mixer(x, operands, alpha, batch0, n_batch):
    _, S, D = x.shape
    tm = MIX_TILE
    assert D == D_MODEL and S % tm == 0 and tm % GLA_CHUNK == 0
    return pl.pallas_call(
        functools.partial(_mixer_kernel, alpha=alpha, tm=tm),
        name="mixer",
        grid=(n_batch, S // tm),
        in_specs=[pl.BlockSpec((1, tm, D), lambda b, j: (b + batch0, j, 0))]
        + [_const_spec(w.shape) for w in operands],
        out_specs=[pl.BlockSpec((1, tm, D), lambda b, j: (b, j, 0)),
                   pl.BlockSpec((1, tm, HALF), lambda b, j: (b, j, 0))],
        out_shape=[jax.ShapeDtypeStruct((n_batch, S, D), F32),
                   jax.ShapeDtypeStruct((n_batch, S, HALF), I32)],
        scratch_shapes=[
            pltpu.VMEM((GLA_HEADS, HEAD_V, HEAD_K), F32),
            pltpu.VMEM((GLA_CHUNK + POOL_HALO, POOL_WIDTH), F32),
            pltpu.VMEM((GLA_CHUNK, 2 * GLA_DK), F32),
            pltpu.VMEM((GLA_CHUNK, GLA_DV), F32),
            pltpu.VMEM((GLA_CHUNK, GLA_DK), F32),
        ],
        compiler_params=pltpu.CompilerParams(
            dimension_semantics=("arbitrary", "arbitrary"), vmem_limit_bytes=VMEM_LIMIT),
    )(x, *operands)


def _rank_desc(vals, n):
    idx = lax.broadcasted_iota(I32, vals.shape, 0)
    rank = jnp.zeros(vals.shape, I32)
    for other in range(n):
        o = vals[other:other + 1, :]
        before = (o > vals) | ((o == vals) & (other < idx))
        rank = rank + before.astype(I32)
    return rank


def _router_kernel(h_ref, wrt_ref, bias_ref, eid_ref, pos_ref, wk_ref, cnt_ref, carry_ref):
    i = pl.program_id(0)

    @pl.when(i == 0)
    def _():
        carry_ref[...] = jnp.zeros_like(carry_ref)

    h = h_ref[...]
    tr = h.shape[0]
    w_hi, w_mid, _ = _split3(wrt_ref[...])
    h_hi, h_mid, _ = _split3(h)
    by_hi = lax.dot_general(jnp.concatenate([w_hi, w_mid], axis=0), h_hi, _NT,
                            preferred_element_type=F32)
    logits = (by_hi[:N_EXPERTS] + by_hi[N_EXPERTS:]
              + lax.dot_general(w_hi, h_mid, _NT, preferred_element_type=F32))
    scores = jax.nn.sigmoid(logits)
    biased = scores + bias_ref[...]
    grp = biased.reshape(N_GROUPS, EXPERTS_PER_GROUP, tr)
    slot = lax.broadcasted_iota(I32, grp.shape, 1)
    top1 = jnp.max(grp, axis=1, keepdims=True)
    first = jnp.min(jnp.where(grp == top1, slot, EXPERTS_PER_GROUP), axis=1, keepdims=True)
    top2 = jnp.max(jnp.where(slot == first, -jnp.inf, grp), axis=1)
    grp_score = top1[:, 0, :] + top2
    grp_on = _rank_desc(grp_score, N_GROUPS) < TOP_GROUPS
    emask = jnp.broadcast_to(grp_on[:, None, :], grp.shape).reshape(N_EXPERTS, tr)
    masked = jnp.where(emask, biased, -jnp.inf)

    eidx = lax.broadcasted_iota(I32, (N_EXPERTS, tr), 0).astype(F32)
    rest = masked
    eids, sels = [], []
    for _ in range(TOP_K):
        top = jnp.max(rest, axis=0, keepdims=True)
        eid = jnp.min(jnp.where(rest == top, eidx, float(N_EXPERTS)), axis=0, keepdims=True)
        hit = eidx == eid
        rest = jnp.where(hit, -jnp.inf, rest)
        eids.append(eid)
        sels.append(jnp.sum(jnp.where(hit, scores, 0.0), axis=0, keepdims=True))
    eid_k = jnp.concatenate(eids, axis=0)
    sel_k = jnp.concatenate(sels, axis=0)
    gate_k = sel_k / jnp.sum(sel_k, axis=0, keepdims=True) * ROUTE_SCALE
    chosen = rest != masked

    t_row = lax.broadcasted_iota(I32, (tr, tr), 0)
    t_col = lax.broadcasted_iota(I32, (tr, tr), 1)
    earlier = (t_row < t_col).astype(BF16)
    pos = carry_ref[...] + _dot(chosen.astype(BF16), earlier)
    carry_ref[...] += jnp.sum(chosen.astype(F32), axis=1, keepdims=True)
    cnt_ref[...] = carry_ref[...]

    poss = [jnp.sum(jnp.where(eidx == eid, pos, 0.0), axis=0, keepdims=True) for eid in eids]
    eid_ref[...] = eid_k.astype(I32)
    pos_ref[...] = jnp.concatenate(poss, axis=0).astype(I32)
    wk_ref[...] = gate_k


def _router(h2, w_router, router_bias):
    T, D = h2.shape
    tr = ROUTER_TILE
    assert T % tr == 0
    return pl.pallas_call(
        _router_kernel,
        name="router",
        grid=(T // tr,),
        in_specs=[pl.BlockSpec((tr, D), lambda i: (i, 0)),
                  _const_spec((N_EXPERTS, D)), _const_spec((N_EXPERTS, 1))],
        out_specs=[pl.BlockSpec((TOP_K, tr), lambda i: (0, i)),
                   pl.BlockSpec((TOP_K, tr), lambda i: (0, i)),
                   pl.BlockSpec((TOP_K, tr), lambda i: (0, i)),
                   pl.BlockSpec((N_EXPERTS, 1), lambda i: (0, 0))],
        out_shape=[jax.ShapeDtypeStruct((TOP_K, T), I32),
                   jax.ShapeDtypeStruct((TOP_K, T), I32),
                   jax.ShapeDtypeStruct((TOP_K, T), F32),
                   jax.ShapeDtypeStruct((N_EXPERTS, 1), F32)],
        scratch_shapes=[pltpu.VMEM((N_EXPERTS, 1), F32)],
        compiler_params=pltpu.CompilerParams(
            dimension_semantics=("arbitrary",), vmem_limit_bytes=VMEM_LIMIT),
    )(h2, w_router.T, router_bias.reshape(N_EXPERTS, 1))


def _slot_kernel(offs_ref, eid_ref, pos_ref, slot_ref):
    eid = eid_ref[...]
    base = jnp.zeros(eid.shape, I32)
    for e in range(N_EXPERTS):
        base = jnp.where(eid == e, offs_ref[e], base)
    slot_ref[...] = pos_ref[...] + base


def _slots(offsets, eid, pos):
    shape = eid.shape
    return pl.pallas_call(
        _slot_kernel,
        name="slots",
        grid_spec=pltpu.PrefetchScalarGridSpec(
            num_scalar_prefetch=1, grid=(1,),
            in_specs=[pl.BlockSpec(shape, lambda i, offs: (0, 0)),
                      pl.BlockSpec(shape, lambda i, offs: (0, 0))],
            out_specs=pl.BlockSpec(shape, lambda i, offs: (0, 0))),
        out_shape=jax.ShapeDtypeStruct(shape, I32),
        compiler_params=pltpu.CompilerParams(vmem_limit_bytes=VMEM_LIMIT),
    )(offsets, eid, pos)


def _sc_mesh():
    return plsc.VectorSubcoreMesh(core_axis_name="core", subcore_axis_name="subcore",
                                  num_cores=SC_CORES, num_subcores=SC_SUBCORES)


def _sc_worker():
    return lax.axis_index("subcore") * SC_CORES + lax.axis_index("core")


def _sc_dispatch(words, slots, n_rows):
    T = words.shape[0]
    n_chunks = slots.shape[0]
    workers = SC_CORES * SC_SUBCORES
    assert T == n_chunks * SC_CHUNK and n_chunks % workers == 0
    per_worker = n_chunks // workers

    @functools.partial(
        pl.kernel, mesh=_sc_mesh(),
        out_type=jax.ShapeDtypeStruct((n_rows, HALF), I32),
        scratch_types=[pltpu.VMEM((SC_CHUNK, HALF), I32), pltpu.VMEM((TOP_K, SC_CHUNK), I32),
                       pltpu.SemaphoreType.DMA])
    def dispatch(words_hbm, slots_hbm, out_hbm, rows_v, idx_v, sem):
        first = _sc_worker() * per_worker

        @pl.loop(0, per_worker)
        def _(i):
            c = first + i
            pltpu.sync_copy(slots_hbm.at[c], idx_v)
            pltpu.sync_copy(words_hbm.at[pl.ds(c * SC_CHUNK, SC_CHUNK)], rows_v)
            copies = [pltpu.async_copy(rows_v, out_hbm.at[idx_v.at[k]], sem) for k in range(TOP_K)]
            for cp in copies:
                cp.wait()

    return dispatch(words, slots)


def _sc_combine(rows, slots, gates, T):
    CH = SC_SUM_CHUNK
    steps = SC_CHUNK // CH
    n_blocks = slots.shape[0]
    workers = SC_CORES * SC_SUBCORES
    assert T == n_blocks * SC_CHUNK and n_blocks % workers == 0 and steps >= 2
    per_worker = n_blocks // workers
    vecs = HALF // SC_LANES

    @functools.partial(
        pl.kernel, mesh=_sc_mesh(),
        compiler_params=pltpu.CompilerParams(needs_layout_passes=False),
        out_type=jax.ShapeDtypeStruct((T, HALF), I32),
        scratch_types=[pltpu.VMEM((2, TOP_K, CH, HALF), I32), pltpu.VMEM((2, CH, HALF), I32),
                       pltpu.VMEM((TOP_K, SC_CHUNK), I32), pltpu.VMEM((TOP_K, SC_CHUNK), F32),
                       pltpu.SemaphoreType.DMA((2,)), pltpu.SemaphoreType.DMA((2,))])
    def combine(rows_hbm, slots_hbm, gates_hbm, out_hbm, rows_v, sum_v, idx_v, gates_v, sem_in, sem_out):
        first = _sc_worker() * per_worker

        def gathers(j, b):
            return [pltpu.make_async_copy(rows_hbm.at[idx_v.at[k, pl.ds(j * CH, CH)]], rows_v.at[b, k],
                                          sem_in.at[b]) for k in range(TOP_K)]

        def write_out(token0, b):
            return pltpu.make_async_copy(sum_v.at[b], out_hbm.at[pl.ds(token0, CH)], sem_out.at[b])

        def add_rows(j, b):
            @pl.loop(0, CH)
            def _(t):
                token = jnp.full((SC_LANES,), j * CH + t, I32)
                gate = []
                for k in range(TOP_K):
                    g = plsc.load_gather(gates_v.at[k], [token])
                    gate.append(plsc.pack(g, g, format=plsc.PackFormat.INTERLEAVED))

                @plsc.parallel_loop(0, vecs, 1, unroll=2)
                def _(n):
                    lanes = pl.ds(n * SC_LANES, SC_LANES)
                    terms = [plsc.bitcast(rows_v[b, k, t, lanes], BF16) * gate[k] for k in range(TOP_K)]
                    while len(terms) > 1:
                        terms = [x + y for x, y in zip(terms[::2], terms[1::2])]
                    sum_v[b, t, lanes] = plsc.bitcast(terms[0], I32)

        @pl.loop(0, per_worker)
        def _(i):
            block = first + i
            token0 = block * SC_CHUNK
            pltpu.sync_copy(slots_hbm.at[block], idx_v)
            pltpu.sync_copy(gates_hbm.at[block], gates_v)
            for cp in gathers(0, 0):
                cp.start()
            for j in range(steps):
                b = j % 2
                if j + 1 < steps:
                    for cp in gathers(j + 1, 1 - b):
                        cp.start()
                for cp in gathers(j, b):
                    cp.wait()
                if j >= 2:
                    write_out(token0 + (j - 2) * CH, b).wait()
                add_rows(j, b)
                write_out(token0 + j * CH, b).start()
            write_out(token0 + (steps - 2) * CH, steps % 2).wait()
            write_out(token0 + (steps - 1) * CH, (steps - 1) % 2).wait()

    return combine(rows, slots, gates)


def _expert_kernel(tile_expert_ref, tile_rows_ref, tile_block_ref, xs_ref, wg_ref, wu_ref, wd_ref,
                   ys_ref, wgu_ref, wdn_ref):
    del tile_block_ref
    i = pl.program_id(0)
    e = tile_expert_ref[i]
    n_valid = tile_rows_ref[i]
    e_prev = tile_expert_ref[jnp.maximum(i - 1, 0)]

    @pl.when((i == 0) | (e != e_prev))
    def _():
        wgu_ref[:, :EXPERT_FF] = wg_ref[0].astype(BF16)
        wgu_ref[:, EXPERT_FF:] = wu_ref[0].astype(BF16)
        wdn_ref[...] = wd_ref[0].astype(BF16)

    @pl.when(n_valid > 0)
    def _():
        words = xs_ref[...]
        live = lax.broadcasted_iota(I32, (words.shape[0], 1), 0) < n_valid
        lo, hi = _unpack_rows(jnp.where(live, words, 0))
        a = _dot(lo.astype(BF16), wgu_ref[:HALF, :]) + _dot(hi.astype(BF16), wgu_ref[HALF:, :])
        act = a[:, :EXPERT_FF]
        hid = act * jax.nn.sigmoid(act) * a[:, EXPERT_FF:]
        ys_ref[...] = _pack_rows(_dot(hid.astype(BF16), wdn_ref[...]))


def _experts(xs, tile_expert, tile_rows, tile_block, w_exp_gate, w_exp_up, w_exp_down):
    n_rows = xs.shape[0]
    R = EXPERT_ROWS
    D = D_MODEL
    assert n_rows % R == 0
    return pl.pallas_call(
        _expert_kernel,
        name="experts",
        grid_spec=pltpu.PrefetchScalarGridSpec(
            num_scalar_prefetch=3, grid=(n_rows // R,),
            in_specs=[pl.BlockSpec((R, HALF), lambda i, te, tr, tb: (tb[i], 0)),
                      pl.BlockSpec((1, D, EXPERT_FF), lambda i, te, tr, tb: (te[i], 0, 0)),
                      pl.BlockSpec((1, D, EXPERT_FF), lambda i, te, tr, tb: (te[i], 0, 0)),
                      pl.BlockSpec((1, EXPERT_FF, D), lambda i, te, tr, tb: (te[i], 0, 0))],
            out_specs=pl.BlockSpec((R, HALF), lambda i, te, tr, tb: (tb[i], 0)),
            scratch_shapes=[pltpu.VMEM((D, 2 * EXPERT_FF), BF16), pltpu.VMEM((EXPERT_FF, D), BF16)]),
        out_shape=jax.ShapeDtypeStruct((n_rows, HALF), I32),
        compiler_params=pltpu.CompilerParams(
            dimension_semantics=("arbitrary",), vmem_limit_bytes=VMEM_LIMIT),
    )(tile_expert, tile_rows, tile_block, xs, w_exp_gate, w_exp_up, w_exp_down)


def _final_kernel(h_ref, routed_ref, wsgu_ref, wsd_ref, lng_ref, lnb_ref, *rest, alpha):
    out_ref = rest[-1]
    h = h_ref[...]
    a = _dot(h.astype(BF16), wsgu_ref[...])
    act = a[:, :SHARED_FF]
    hid = act * jax.nn.sigmoid(act) * a[:, SHARED_FF:]
    shared = _dot(hid.astype(BF16), wsd_ref[...])
    lo, hi = _unpack_rows(routed_ref[...])
    ffn = shared + jnp.concatenate([lo, hi], axis=1)
    out_ref[...] = _layer_norm(alpha * h + ffn, lng_ref[...], lnb_ref[...])


def _final(h2, routed, w_sgu, w_sd, ln_g, ln_b, alpha, out_prev, row0, total_rows):
    T, D = h2.shape
    tm = FINAL_TILE
    assert T % tm == 0 and row0 % tm == 0
    tile0 = row0 // tm
    operands = [h2, routed, w_sgu, w_sd, ln_g.reshape(1, D), ln_b.reshape(1, D)]
    in_specs = [pl.BlockSpec((tm, D), lambda i: (i, 0)),
                pl.BlockSpec((tm, HALF), lambda i: (i, 0)),
                _const_spec((D, 2 * SHARED_FF)), _const_spec((SHARED_FF, D)),
                _const_spec((1, D)), _const_spec((1, D))]
    aliases = {}
    if out_prev is not None:
        aliases = {len(operands): 0}
        operands.append(out_prev)
        in_specs.append(pl.BlockSpec(memory_space=pl.ANY))
    return pl.pallas_call(
        functools.partial(_final_kernel, alpha=alpha),
        name="final",
        grid=(T // tm,),
        in_specs=in_specs,
        out_specs=pl.BlockSpec((tm, D), lambda i: (i + tile0, 0)),
        out_shape=jax.ShapeDtypeStruct((total_rows, D), F32),
        input_output_aliases=aliases,
        compiler_params=pltpu.CompilerParams(
            dimension_semantics=("arbitrary",), vmem_limit_bytes=VMEM_LIMIT),
    )(*operands)


def _route_and_dispatch(h2, words, w_router, router_bias):
    T, D = h2.shape
    R = EXPERT_ROWS
    eid, pos, gates, counts = _router(h2, w_router, router_bias)
    counts = counts.reshape(N_EXPERTS).astype(I32)
    padded = (counts + (R - 1)) // R * R
    ends = jnp.cumsum(padded)
    offsets = ends - padded
    n_rows = T * TOP_K + N_EXPERTS * R
    tile_start = jnp.arange(n_rows // R, dtype=I32) * R
    tile_block = jnp.minimum(tile_start, jnp.maximum(ends[-1] - R, 0)) // R
    past = ((tile_block * R)[:, None] >= ends[None, :]).astype(I32)
    tile_expert = jnp.minimum(jnp.sum(past, axis=1), N_EXPERTS - 1)
    mine = (tile_expert[:, None] == jnp.arange(N_EXPERTS, dtype=I32)[None, :]).astype(I32)
    valid_end = jnp.sum(mine * (offsets + counts)[None, :], axis=1)
    tile_rows = jnp.clip(valid_end - tile_start, 0, R).astype(I32)
    slots = _slots(offsets.astype(I32), eid, pos)

    def chunked(a):
        return a.reshape(TOP_K, T // SC_CHUNK, SC_CHUNK).transpose(1, 0, 2)

    slots = chunked(slots)
    xs = _sc_dispatch(words, slots, n_rows)
    return xs, slots, chunked(gates), (tile_expert, tile_rows, tile_block.astype(I32))


def kernel(x, w_in, w_gate_up, b_gate, gla_norm_g, w_gla_up, w_pool_grp, pool_scale, w_pool_up,
           w_out, ln1_g, ln1_b, w_router, router_bias, w_exp_gate, w_exp_up, w_exp_down,
           w_sh_gate, w_sh_up, w_sh_down, ln2_g, ln2_b):
    B, S, D = x.shape
    depth = w_in.shape[0]
    alpha = (2.0 * depth) ** 0.25
    n_groups = TOKEN_GROUPS if B % TOKEN_GROUPS == 0 else 1
    gb = B // n_groups
    gt = gb * S
    h = x
    for l in range(depth):
        mix_ops = _mixer_operands(w_in[l], w_gate_up[l], b_gate[l], gla_norm_g[l], w_gla_up[l],
                                  w_pool_grp[l], pool_scale[l], w_pool_up[l], w_out[l],
                                  ln1_g[l], ln1_b[l])
        w_sgu = jnp.concatenate([w_sh_gate[l], w_sh_up[l]], axis=-1).astype(BF16)
        w_sd = w_sh_down[l].astype(BF16)
        staged = []
        for g in range(n_groups):
            hg, words = _mixer(h, mix_ops, alpha, g * gb, gb)
            hg = hg.reshape(gt, D)
            staged.append((hg,) + _route_and_dispatch(hg, words.reshape(gt, HALF),
                                                      w_router[l], router_bias[l]))
        out = None
        for g, (hg, xs, slots, gates, tiles) in enumerate(staged):
            ys = _experts(xs, *tiles, w_exp_gate[l].astype(BF16), w_exp_up[l].astype(BF16),
                          w_exp_down[l].astype(BF16))
            routed = _sc_combine(ys, slots, gates, gt)
            out = _final(hg, routed, w_sgu, w_sd, ln2_g[l], ln2_b[l], alpha, out, g * gt, B * S)
        h = out.reshape(B, S, D)
    return h
```

```python
import functools

import jax
import jax.numpy as jnp
from jax import lax
from jax.experimental import pallas as pl
from jax.experimental.pallas import tpu as pltpu
from jax.experimental.pallas import tpu_sc as plsc

F32 = jnp.float32
BF16 = jnp.bfloat16
I32 = jnp.int32
U32 = jnp.uint32

D_MODEL = 1024
GLA_HEADS = 4
GLA_DK = D_MODEL // 2
GLA_DV = D_MODEL
HEAD_K = GLA_DK // GLA_HEADS
HEAD_V = GLA_DV // GLA_HEADS
GATE_RANK = 16
GATE_TEMP = 16.0
POOL_WIDTH = D_MODEL // 2
POOL_GROUPS = 4
POOL_GROUP_DIM = POOL_WIDTH // POOL_GROUPS
POOL_WINDOWS = (2, 4, 8, 16)
POOL_HALO = 16
N_EXPERTS = 64
TOP_K = 8
N_GROUPS = 8
TOP_GROUPS = 4
EXPERTS_PER_GROUP = N_EXPERTS // N_GROUPS
EXPERT_FF = 256
SHARED_FF = 256
ROUTE_SCALE = 2.5
LN_EPS = 1e-5
RMS_EPS = 1e-6
LANES = 128
SUBLANES = 8

GLA_CHUNK = 256
DECAY_LIMIT = 60.0
MIX_TILE = 512
ROUTER_TILE = 1024
FINAL_TILE = 1024
EXPERT_ROWS = 1024
TOKEN_GROUPS = 1
SC_CORES = 2
SC_SUBCORES = 16
SC_LANES = 16
SC_CHUNK = 128
SC_SUM_CHUNK = 8
HALF = D_MODEL // 2
VMEM_LIMIT = 56 * 1024 * 1024

_dot = functools.partial(jnp.dot, preferred_element_type=F32)
_NT = (((1,), (1,)), ((), ()))
_TN = (((0,), (0,)), ((), ()))


def _layer_norm(y, g, b):
    mu = jnp.mean(y, axis=-1, keepdims=True)
    yc = y - mu
    var = jnp.mean(yc * yc, axis=-1, keepdims=True)
    return yc * lax.rsqrt(var + LN_EPS) * g + b


def _split3(a):
    hi = a.astype(BF16)
    r1 = a - hi.astype(F32)
    mid = r1.astype(BF16)
    lo = (r1 - mid.astype(F32)).astype(BF16)
    return hi, mid, lo


def _pack_rows(y):
    lo = lax.bitcast_convert_type(y[:, :HALF].astype(BF16).astype(F32), U32)
    hi = lax.bitcast_convert_type(y[:, HALF:].astype(BF16).astype(F32), U32)
    return lax.bitcast_convert_type(hi | (lo >> 16), I32)


def _unpack_rows(w):
    u = lax.bitcast_convert_type(w, U32)
    lo = lax.bitcast_convert_type(u << 16, F32)
    hi = lax.bitcast_convert_type(u & jnp.uint32(0xFFFF0000), F32)
    return lo, hi


def _mixer_kernel(x_ref, wqk_ref, wv_ref, wr_ref, wglr_ref, wgup_ref, bgate_ref, wu_ref,
                  wga_ref, wgb_ref, gnorm_ref, wglaup_ref, wpgrp_ref, pscale_ref, wpup_ref,
                  wout_ref, lng_ref, lnb_ref, out_ref, words_ref, state_ref, ucat_ref,
                  qk_s, v_s, b_s, *, alpha, tm):
    j = pl.program_id(1)

    @pl.when(j == 0)
    def _():
        state_ref[...] = jnp.zeros_like(state_ref)
        ucat_ref[0:POOL_HALO, :] = jnp.zeros((POOL_HALO, POOL_WIDTH), F32)

    C = GLA_CHUNK
    causal = lax.broadcasted_iota(I32, (C, C), 0) >= lax.broadcasted_iota(I32, (C, C), 1)
    tri = causal.astype(BF16)
    gnorm = gnorm_ref[...]
    states = [state_ref[h] for h in range(GLA_HEADS)]

    n_chunks = tm // C

    def finish(rows, x, intra, inter, r, gate_a, gate_b, y_pool):
        heads = []
        for h in range(GLA_HEADS):
            o = intra[h] + inter[h]
            ms = jnp.mean(o * o, axis=-1, keepdims=True)
            heads.append(o * lax.rsqrt(ms + RMS_EPS) * gnorm)
        o = jnp.concatenate(heads, axis=1) * (r * jax.nn.sigmoid(r))
        y_gla = _dot(o.astype(BF16), wglaup_ref[...])
        merged = jax.nn.sigmoid(gate_a) * y_gla + jax.nn.sigmoid(gate_b) * y_pool
        mix = _dot(merged.astype(BF16), wout_ref[...])
        y = _layer_norm(alpha * x + mix, lng_ref[...], lnb_ref[...])
        out_ref[0, rows, :] = y
        words_ref[0, rows, :] = _pack_rows(y)

    kept = []
    for c in range(n_chunks):
        rows = pl.ds(c * C, C)
        x = x_ref[0, rows, :]
        xb = x.astype(BF16)

        qk = _dot(xb, wqk_ref[...])
        v = _dot(xb, wv_ref[...])
        g_lr = _dot(xb, wglr_ref[...])
        z = _dot(g_lr.astype(BF16), wgup_ref[...]) + bgate_ref[...]
        u = _dot(xb, wu_ref[...])
        r = _dot(xb, wr_ref[...])
        log_decay = (jnp.minimum(z, 0.0) - jnp.log1p(jnp.exp(-jnp.abs(z)))) * (1.0 / GATE_TEMP)
        g_hi, g_mid, _ = _split3(log_decay)
        gate_a = _dot(xb, wga_ref[...])
        bcum = _dot(tri, g_hi) + _dot(tri, g_mid)

        ucat_ref[POOL_HALO:, :] = u
        pos = j * tm + c * C - POOL_HALO + lax.broadcasted_iota(I32, (C + POOL_HALO, 1), 0)

        def pool_group(gi):
            window = POOL_WINDOWS[gi]
            a = ucat_ref[:, gi * POOL_GROUP_DIM:(gi + 1) * POOL_GROUP_DIM]
            s = a
            step = 1
            while step < window:
                s = s + pltpu.roll(s, step, 0)
                step *= 2
            count = jnp.clip(pos + 1, 1, window).astype(F32)
            pooled = (s / count - a)[POOL_HALO:, :]
            return _dot(pooled.astype(BF16), wpgrp_ref[gi])

        intra, inter, mixed = [], [], []
        for h in range(GLA_HEADS):
            ks = slice(h * HEAD_K, (h + 1) * HEAD_K)
            b = bcum[:, ks]
            b_last = b[C - 1:C, :]
            q_h = qk[:, ks] * (HEAD_K ** -0.5)
            k_h = qk[:, GLA_DK + h * HEAD_K:GLA_DK + (h + 1) * HEAD_K]
            v_h = v[:, h * HEAD_V:(h + 1) * HEAD_V].astype(BF16)
            q_in = (q_h * jnp.exp(b)).astype(BF16)
            k_out = (k_h * jnp.exp(-b)).astype(BF16)
            k_end = (k_h * jnp.exp(b_last - b)).astype(BF16)
            scores = lax.dot_general(q_in, k_out, _NT, preferred_element_type=F32)
            scores = jnp.where(causal, scores, 0.0).astype(BF16)
            st = states[h]
            intra.append(_dot(scores, v_h))
            inter.append(lax.dot_general(q_in, st.astype(BF16), _NT, preferred_element_type=F32))
            states[h] = st * jnp.exp(b_last) + lax.dot_general(
                v_h, k_end, _TN, preferred_element_type=F32)
            mixed.append(pool_group(h))
            if h == 1:
                gate_b = _dot(xb, wgb_ref[...])
        ucat_ref[0:POOL_HALO, :] = u[C - POOL_HALO:, :]
        mixed = jnp.concatenate(mixed, axis=1) * pscale_ref[...]
        y_pool = _dot(mixed.astype(BF16), wpup_ref[...])
        finish(rows, x, intra, inter, r, gate_a, gate_b, y_pool)
        kept.append((qk, v, bcum, inter, y_pool))

    for h in range(GLA_HEADS):
        state_ref[h] = states[h]

    steepest = kept[0][2][C - 1:C, :]
    for c in range(1, n_chunks):
        steepest = jnp.minimum(steepest, kept[c][2][C - 1:C, :])

    @pl.when(jnp.max(-steepest) > DECAY_LIMIT)
    def _():
        row_id = lax.broadcasted_iota(I32, (C, 1), 0)
        for c in range(n_chunks):
            rows = pl.ds(c * C, C)
            qk, v, bcum, inter, y_pool = kept[c]
            qk_s[...] = qk
            v_s[...] = v
            b_s[...] = bcum
            intra = []
            for h in range(GLA_HEADS):
                ks = slice(h * HEAD_K, (h + 1) * HEAD_K)
                kks = slice(GLA_DK + h * HEAD_K, GLA_DK + (h + 1) * HEAD_K)
                vs = slice(h * HEAD_V, (h + 1) * HEAD_V)
                q_h = qk[:, ks] * (HEAD_K ** -0.5)
                b_h = bcum[:, ks]

                def add_rows(group, acc):
                    first = pl.multiple_of(group * SUBLANES, SUBLANES)
                    b_g = b_s[pl.ds(first, SUBLANES), ks]
                    k_g = qk_s[pl.ds(first, SUBLANES), kks]
                    v_g = v_s[pl.ds(first, SUBLANES), vs]
                    for i in range(SUBLANES):
                        decay = jnp.exp(jnp.minimum(b_h - b_g[i:i + 1, :], 0.0))
                        decay = jnp.where(row_id >= first + i, decay, 0.0)
                        score = jnp.sum(q_h * k_g[i:i + 1, :] * decay, axis=1, keepdims=True)
                        acc = acc + score * v_g[i:i + 1, :]
                    return acc

                intra.append(lax.fori_loop(0, C // SUBLANES, add_rows, jnp.zeros((C, HEAD_V), F32)))
            x = x_ref[0, rows, :]
            xb = x.astype(BF16)
            finish(rows, x, intra, inter, _dot(xb, wr_ref[...]), _dot(xb, wga_ref[...]),
                   _dot(xb, wgb_ref[...]), y_pool)


def _const_spec(shape):
    zeros = (0,) * len(shape)
    return pl.BlockSpec(shape, lambda *_: zeros, pipeline_mode=pl.Buffered(1))


def _mixer_operands(w_in, w_gate_up, b_gate, gla_norm_g, w_gla_up, w_pool_grp, pool_scale,
                    w_pool_up, w_out, ln_g, ln_b):
    D = D_MODEL
    c = 0
    w_qk = w_in[:, c:c + 2 * GLA_DK].astype(BF16); c += 2 * GLA_DK
    w_v = w_in[:, c:c + GLA_DV].astype(BF16); c += GLA_DV
    w_r = w_in[:, c:c + GLA_DV].astype(BF16); c += GLA_DV
    w_glr = jnp.pad(w_in[:, c:c + GATE_RANK], ((0, 0), (0, LANES - GATE_RANK))).astype(BF16); c += GATE_RANK
    w_u = w_in[:, c:c + POOL_WIDTH].astype(BF16); c += POOL_WIDTH
    w_ga = w_in[:, c:c + D].astype(BF16); c += D
    w_gb = w_in[:, c:c + D].astype(BF16); c += D
    assert c == w_in.shape[1]
    w_gup = jnp.pad(w_gate_up, ((0, LANES - GATE_RANK), (0, 0))).astype(BF16)
    return (
        w_qk, w_v, w_r, w_glr, w_gup, b_gate.reshape(1, GLA_DK), w_u, w_ga, w_gb,
        gla_norm_g.reshape(1, HEAD_V), w_gla_up.astype(BF16), w_pool_grp.astype(BF16),
        pool_scale.reshape(1, POOL_WIDTH), w_pool_up.astype(BF16), w_out.astype(BF16),
        ln_g.reshape(1, D), ln_b.reshape(1, D))


def _mixer(x, operands, alpha, batch0, n_batch):
    _, S, D = x.shape
    tm = MIX_TILE
    assert D == D_MODEL and S % tm == 0 and tm % GLA_CHUNK == 0
    return pl.pallas_call(
        functools.partial(_mixer_kernel, alpha=alpha, tm=tm),
        name="mixer",
        grid=(n_batch, S // tm),
        in_specs=[pl.BlockSpec((1, tm, D), lambda b, j: (b + batch0, j, 0))]
        + [_const_spec(w.shape) for w in operands],
        out_specs=[pl.BlockSpec((1, tm, D), lambda b, j: (b, j, 0)),
                   pl.BlockSpec((1, tm, HALF), lambda b, j: (b, j, 0))],
        out_shape=[jax.ShapeDtypeStruct((n_batch, S, D), F32),
                   jax.ShapeDtypeStruct((n_batch, S, HALF), I32)],
        scratch_shapes=[
            pltpu.VMEM((GLA_HEADS, HEAD_V, HEAD_K), F32),
            pltpu.VMEM((GLA_CHUNK + POOL_HALO, POOL_WIDTH), F32),
            pltpu.VMEM((GLA_CHUNK, 2 * GLA_DK), F32),
            pltpu.VMEM((GLA_CHUNK, GLA_DV), F32),
            pltpu.VMEM((GLA_CHUNK, GLA_DK), F32),
        ],
        compiler_params=pltpu.CompilerParams(
            dimension_semantics=("arbitrary", "arbitrary"), vmem_limit_bytes=VMEM_LIMIT),
    )(x, *operands)


def _rank_desc(vals, n):
    idx = lax.broadcasted_iota(I32, vals.shape, 0)
    rank = jnp.zeros(vals.shape, I32)
    for other in range(n):
        o = vals[other:other + 1, :]
        before = (o > vals) | ((o == vals) & (other < idx))
        rank = rank + before.astype(I32)
    return rank


def _router_kernel(h_ref, wrt_ref, bias_ref, eid_ref, pos_ref, wk_ref, cnt_ref, carry_ref):
    i = pl.program_id(0)

    @pl.when(i == 0)
    def _():
        carry_ref[...] = jnp.zeros_like(carry_ref)

    h = h_ref[...]
    tr = h.shape[0]
    w_hi, w_mid, _ = _split3(wrt_ref[...])
    h_hi, h_mid, _ = _split3(h)
    by_hi = lax.dot_general(jnp.concatenate([w_hi, w_mid], axis=0), h_hi, _NT,
                            preferred_element_type=F32)
    logits = (by_hi[:N_EXPERTS] + by_hi[N_EXPERTS:]
              + lax.dot_general(w_hi, h_mid, _NT, preferred_element_type=F32))
    scores = jax.nn.sigmoid(logits)
    biased = scores + bias_ref[...]
    grp = biased.reshape(N_GROUPS, EXPERTS_PER_GROUP, tr)
    slot = lax.broadcasted_iota(I32, grp.shape, 1)
    top1 = jnp.max(grp, axis=1, keepdims=True)
    first = jnp.min(jnp.where(grp == top1, slot, EXPERTS_PER_GROUP), axis=1, keepdims=True)
    top2 = jnp.max(jnp.where(slot == first, -jnp.inf, grp), axis=1)
    grp_score = top1[:, 0, :] + top2
    grp_on = _rank_desc(grp_score, N_GROUPS) < TOP_GROUPS
    emask = jnp.broadcast_to(grp_on[:, None, :], grp.shape).reshape(N_EXPERTS, tr)
    masked = jnp.where(emask, biased, -jnp.inf)

    eidx = lax.broadcasted_iota(I32, (N_EXPERTS, tr), 0).astype(F32)
    rest = masked
    eids, sels = [], []
    for _ in range(TOP_K):
        top = jnp.max(rest, axis=0, keepdims=True)
        eid = jnp.min(jnp.where(rest == top, eidx, float(N_EXPERTS)), axis=0, keepdims=True)
        hit = eidx == eid
        rest = jnp.where(hit, -jnp.inf, rest)
        eids.append(eid)
        sels.append(jnp.sum(jnp.where(hit, scores, 0.0), axis=0, keepdims=True))
    eid_k = jnp.concatenate(eids, axis=0)
    sel_k = jnp.concatenate(sels, axis=0)
    gate_k = sel_k / jnp.sum(sel_k, axis=0, keepdims=True) * ROUTE_SCALE
    chosen = rest != masked

    t_row = lax.broadcasted_iota(I32, (tr, tr), 0)
    t_col = lax.broadcasted_iota(I32, (tr, tr), 1)
    earlier = (t_row < t_col).astype(BF16)
    pos = carry_ref[...] + _dot(chosen.astype(BF16), earlier)
    carry_ref[...] += jnp.sum(chosen.astype(F32), axis=1, keepdims=True)
    cnt_ref[...] = carry_ref[...]

    poss = [jnp.sum(jnp.where(eidx == eid, pos, 0.0), axis=0, keepdims=True) for eid in eids]
    eid_ref[...] = eid_k.astype(I32)
    pos_ref[...] = jnp.concatenate(poss, axis=0).astype(I32)
    wk_ref[...] = gate_k


def _router(h2, w_router, router_bias):
    T, D = h2.shape
    tr = ROUTER_TILE
    assert T % tr == 0
    return pl.pallas_call(
        _router_kernel,
        name="router",
        grid=(T // tr,),
        in_specs=[pl.BlockSpec((tr, D), lambda i: (i, 0)),
                  _const_spec((N_EXPERTS, D)), _const_spec((N_EXPERTS, 1))],
        out_specs=[pl.BlockSpec((TOP_K, tr), lambda i: (0, i)),
                   pl.BlockSpec((TOP_K, tr), lambda i: (0, i)),
                   pl.BlockSpec((TOP_K, tr), lambda i: (0, i)),
                   pl.BlockSpec((N_EXPERTS, 1), lambda i: (0, 0))],
        out_shape=[jax.ShapeDtypeStruct((TOP_K, T), I32),
                   jax.ShapeDtypeStruct((TOP_K, T), I32),
                   jax.ShapeDtypeStruct((TOP_K, T), F32),
                   jax.ShapeDtypeStruct((N_EXPERTS, 1), F32)],
        scratch_shapes=[pltpu.VMEM((N_EXPERTS, 1), F32)],
        compiler_params=pltpu.CompilerParams(
            dimension_semantics=("arbitrary",), vmem_limit_bytes=VMEM_LIMIT),
    )(h2, w_router.T, router_bias.reshape(N_EXPERTS, 1))


def _slot_kernel(offs_ref, eid_ref, pos_ref, slot_ref):
    eid = eid_ref[...]
    base = jnp.zeros(eid.shape, I32)
    for e in range(N_EXPERTS):
        base = jnp.where(eid == e, offs_ref[e], base)
    slot_ref[...] = pos_ref[...] + base


def _slots(offsets, eid, pos):
    shape = eid.shape
    return pl.pallas_call(
        _slot_kernel,
        name="slots",
        grid_spec=pltpu.PrefetchScalarGridSpec(
            num_scalar_prefetch=1, grid=(1,),
            in_specs=[pl.BlockSpec(shape, lambda i, offs: (0, 0)),
                      pl.BlockSpec(shape, lambda i, offs: (0, 0))],
            out_specs=pl.BlockSpec(shape, lambda i, offs: (0, 0))),
        out_shape=jax.ShapeDtypeStruct(shape, I32),
        compiler_params=pltpu.CompilerParams(vmem_limit_bytes=VMEM_LIMIT),
    )(offsets, eid, pos)


def _sc_mesh():
    return plsc.VectorSubcoreMesh(core_axis_name="core", subcore_axis_name="subcore",
                                  num_cores=SC_CORES, num_subcores=SC_SUBCORES)


def _sc_worker():
    return lax.axis_index("subcore") * SC_CORES + lax.axis_index("core")


def _sc_dispatch(words, slots, n_rows):
    T = words.shape[0]
    n_chunks = slots.shape[0]
    workers = SC_CORES * SC_SUBCORES
    assert T == n_chunks * SC_CHUNK and n_chunks % workers == 0
    per_worker = n_chunks // workers

    @functools.partial(
        pl.kernel, mesh=_sc_mesh(),
        out_type=jax.ShapeDtypeStruct((n_rows, HALF), I32),
        scratch_types=[pltpu.VMEM((SC_CHUNK, HALF), I32), pltpu.VMEM((TOP_K, SC_CHUNK), I32),
                       pltpu.SemaphoreType.DMA])
    def dispatch(words_hbm, slots_hbm, out_hbm, rows_v, idx_v, sem):
        first = _sc_worker() * per_worker

        @pl.loop(0, per_worker)
        def _(i):
            c = first + i
            pltpu.sync_copy(slots_hbm.at[c], idx_v)
            pltpu.sync_copy(words_hbm.at[pl.ds(c * SC_CHUNK, SC_CHUNK)], rows_v)
            copies = [pltpu.async_copy(rows_v, out_hbm.at[idx_v.at[k]], sem) for k in range(TOP_K)]
            for cp in copies:
                cp.wait()

    return dispatch(words, slots)


def _sc_combine(rows, slots, gates, T):
    CH = SC_SUM_CHUNK
    steps = SC_CHUNK // CH
    n_blocks = slots.shape[0]
    workers = SC_CORES * SC_SUBCORES
    assert T == n_blocks * SC_CHUNK and n_blocks % workers == 0 and steps >= 2
    per_worker = n_blocks // workers
    vecs = HALF // SC_LANES

    @functools.partial(
        pl.kernel, mesh=_sc_mesh(),
        compiler_params=pltpu.CompilerParams(needs_layout_passes=False),
        out_type=jax.ShapeDtypeStruct((T, HALF), I32),
        scratch_types=[pltpu.VMEM((2, TOP_K, CH, HALF), I32), pltpu.VMEM((2, CH, HALF), I32),
                       pltpu.VMEM((TOP_K, SC_CHUNK), I32), pltpu.VMEM((TOP_K, SC_CHUNK), F32),
                       pltpu.SemaphoreType.DMA((2,)), pltpu.SemaphoreType.DMA((2,))])
    def combine(rows_hbm, slots_hbm, gates_hbm, out_hbm, rows_v, sum_v, idx_v, gates_v, sem_in, sem_out):
        first = _sc_worker() * per_worker

        def gathers(j, b):
            return [pltpu.make_async_copy(rows_hbm.at[idx_v.at[k, pl.ds(j * CH, CH)]], rows_v.at[b, k],
                                          sem_in.at[b]) for k in range(TOP_K)]

        def write_out(token0, b):
            return pltpu.make_async_copy(sum_v.at[b], out_hbm.at[pl.ds(token0, CH)], sem_out.at[b])

        def add_rows(j, b):
            @pl.loop(0, CH)
            def _(t):
                token = jnp.full((SC_LANES,), j * CH + t, I32)
                gate = []
                for k in range(TOP_K):
                    g = plsc.load_gather(gates_v.at[k], [token])
                    gate.append(plsc.pack(g, g, format=plsc.PackFormat.INTERLEAVED))

                @plsc.parallel_loop(0, vecs, 1, unroll=2)
                def _(n):
                    lanes = pl.ds(n * SC_LANES, SC_LANES)
                    terms = [plsc.bitcast(rows_v[b, k, t, lanes], BF16) * gate[k] for k in range(TOP_K)]
                    while len(terms) > 1:
                        terms = [x + y for x, y in zip(terms[::2], terms[1::2])]
                    sum_v[b, t, lanes] = plsc.bitcast(terms[0], I32)

        @pl.loop(0, per_worker)
        def _(i):
            block = first + i
            token0 = block * SC_CHUNK
            pltpu.sync_copy(slots_hbm.at[block], idx_v)
            pltpu.sync_copy(gates_hbm.at[block], gates_v)
            for cp in gathers(0, 0):
                cp.start()
            for j in range(steps):
                b = j % 2
                if j + 1 < steps:
                    for cp in gathers(j + 1, 1 - b):
                        cp.start()
                for cp in gathers(j, b):
                    cp.wait()
                if j >= 2:
                    write_out(token0 + (j - 2) * CH, b).wait()
                add_rows(j, b)
                write_out(token0 + j * CH, b).start()
            write_out(token0 + (steps - 2) * CH, steps % 2).wait()
            write_out(token0 + (steps - 1) * CH, (steps - 1) % 2).wait()

    return combine(rows, slots, gates)


def _expert_kernel(tile_expert_ref, tile_rows_ref, tile_block_ref, xs_ref, wg_ref, wu_ref, wd_ref,
                   ys_ref, wgu_ref, wdn_ref):
    del tile_block_ref
    i = pl.program_id(0)
    e = tile_expert_ref[i]
    n_valid = tile_rows_ref[i]
    e_prev = tile_expert_ref[jnp.maximum(i - 1, 0)]

    @pl.when((i == 0) | (e != e_prev))
    def _():
        wgu_ref[:, :EXPERT_FF] = wg_ref[0].astype(BF16)
        wgu_ref[:, EXPERT_FF:] = wu_ref[0].astype(BF16)
        wdn_ref[...] = wd_ref[0].astype(BF16)

    @pl.when(n_valid > 0)
    def _():
        words = xs_ref[...]
        live = lax.broadcasted_iota(I32, (words.shape[0], 1), 0) < n_valid
        lo, hi = _unpack_rows(jnp.where(live, words, 0))
        a = _dot(lo.astype(BF16), wgu_ref[:HALF, :]) + _dot(hi.astype(BF16), wgu_ref[HALF:, :])
        act = a[:, :EXPERT_FF]
        hid = act * jax.nn.sigmoid(act) * a[:, EXPERT_FF:]
        ys_ref[...] = _pack_rows(_dot(hid.astype(BF16), wdn_ref[...]))


def _experts(xs, tile_expert, tile_rows, tile_block, w_exp_gate, w_exp_up, w_exp_down):
    n_rows = xs.shape[0]
    R = EXPERT_ROWS
    D = D_MODEL
    assert n_rows % R == 0
    return pl.pallas_call(
        _expert_kernel,
        name="experts",
        grid_spec=pltpu.PrefetchScalarGridSpec(
            num_scalar_prefetch=3, grid=(n_rows // R,),
            in_specs=[pl.BlockSpec((R, HALF), lambda i, te, tr, tb: (tb[i], 0)),
                      pl.BlockSpec((1, D, EXPERT_FF), lambda i, te, tr, tb: (te[i], 0, 0)),
                      pl.BlockSpec((1, D, EXPERT_FF), lambda i, te, tr, tb: (te[i], 0, 0)),
                      pl.BlockSpec((1, EXPERT_FF, D), lambda i, te, tr, tb: (te[i], 0, 0))],
            out_specs=pl.BlockSpec((R, HALF), lambda i, te, tr, tb: (tb[i], 0)),
            scratch_shapes=[pltpu.VMEM((D, 2 * EXPERT_FF), BF16), pltpu.VMEM((EXPERT_FF, D), BF16)]),
        out_shape=jax.ShapeDtypeStruct((n_rows, HALF), I32),
        compiler_params=pltpu.CompilerParams(
            dimension_semantics=("arbitrary",), vmem_limit_bytes=VMEM_LIMIT),
    )(tile_expert, tile_rows, tile_block, xs, w_exp_gate, w_exp_up, w_exp_down)


def _final_kernel(h_ref, routed_ref, wsgu_ref, wsd_ref, lng_ref, lnb_ref, *rest, alpha):
    out_ref = rest[-1]
    h = h_ref[...]
    a = _dot(h.astype(BF16), wsgu_ref[...])
    act = a[:, :SHARED_FF]
    hid = act * jax.nn.sigmoid(act) * a[:, SHARED_FF:]
    shared = _dot(hid.astype(BF16), wsd_ref[...])
    lo, hi = _unpack_rows(routed_ref[...])
    ffn = shared + jnp.concatenate([lo, hi], axis=1)
    out_ref[...] = _layer_norm(alpha * h + ffn, lng_ref[...], lnb_ref[...])


def _final(h2, routed, w_sgu, w_sd, ln_g, ln_b, alpha, out_prev, row0, total_rows):
    T, D = h2.shape
    tm = FINAL_TILE
    assert T % tm == 0 and row0 % tm == 0
    tile0 = row0 // tm
    operands = [h2, routed, w_sgu, w_sd, ln_g.reshape(1, D), ln_b.reshape(1, D)]
    in_specs = [pl.BlockSpec((tm, D), lambda i: (i, 0)),
                pl.BlockSpec((tm, HALF), lambda i: (i, 0)),
                _const_spec((D, 2 * SHARED_FF)), _const_spec((SHARED_FF, D)),
                _const_spec((1, D)), _const_spec((1, D))]
    aliases = {}
    if out_prev is not None:
        aliases = {len(operands): 0}
        operands.append(out_prev)
        in_specs.append(pl.BlockSpec(memory_space=pl.ANY))
    return pl.pallas_call(
        functools.partial(_final_kernel, alpha=alpha),
        name="final",
        grid=(T // tm,),
        in_specs=in_specs,
        out_specs=pl.BlockSpec((tm, D), lambda i: (i + tile0, 0)),
        out_shape=jax.ShapeDtypeStruct((total_rows, D), F32),
        input_output_aliases=aliases,
        compiler_params=pltpu.CompilerParams(
            dimension_semantics=("arbitrary",), vmem_limit_bytes=VMEM_LIMIT),
    )(*operands)


def _route_and_dispatch(h2, words, w_router, router_bias):
    T, D = h2.shape
    R = EXPERT_ROWS
    eid, pos, gates, counts = _router(h2, w_router, router_bias)
    counts = counts.reshape(N_EXPERTS).astype(I32)
    padded = (counts + (R - 1)) // R * R
    ends = jnp.cumsum(padded)
    offsets = ends - padded
    n_rows = T * TOP_K + N_EXPERTS * R
    tile_start = jnp.arange(n_rows // R, dtype=I32) * R
    tile_block = jnp.minimum(tile_start, jnp.maximum(ends[-1] - R, 0)) // R
    past = ((tile_block * R)[:, None] >= ends[None, :]).astype(I32)
    tile_expert = jnp.minimum(jnp.sum(past, axis=1), N_EXPERTS - 1)
    mine = (tile_expert[:, None] == jnp.arange(N_EXPERTS, dtype=I32)[None, :]).astype(I32)
    valid_end = jnp.sum(mine * (offsets + counts)[None, :], axis=1)
    tile_rows = jnp.clip(valid_end - tile_start, 0, R).astype(I32)
    slots = _slots(offsets.astype(I32), eid, pos)

    def chunked(a):
        return a.reshape(TOP_K, T // SC_CHUNK, SC_CHUNK).transpose(1, 0, 2)

    slots = chunked(slots)
    xs = _sc_dispatch(words, slots, n_rows)
    return xs, slots, chunked(gates), (tile_expert, tile_rows, tile_block.astype(I32))


def kernel(x, w_in, w_gate_up, b_gate, gla_norm_g, w_gla_up, w_pool_grp, pool_scale, w_pool_up,
           w_out, ln1_g, ln1_b, w_router, router_bias, w_exp_gate, w_exp_up, w_exp_down,
           w_sh_gate, w_sh_up, w_sh_down, ln2_g, ln2_b):
    B, S, D = x.shape
    depth = w_in.shape[0]
    alpha = (2.0 * depth) ** 0.25
    n_groups = TOKEN_GROUPS if B % TOKEN_GROUPS == 0 else 1
    gb = B // n_groups
    gt = gb * S
    h = x
    for l in range(depth):
        mix_ops = _mixer_operands(w_in[l], w_gate_up[l], b_gate[l], gla_norm_g[l], w_gla_up[l],
                                  w_pool_grp[l], pool_scale[l], w_pool_up[l], w_out[l],
                                  ln1_g[l], ln1_b[l])
        w_sgu = jnp.concatenate([w_sh_gate[l], w_sh_up[l]], axis=-1).astype(BF16)
        w_sd = w_sh_down[l].astype(BF16)
        staged = []
        for g in range(n_groups):
            hg, words = _mixer(h, mix_ops, alpha, g * gb, gb)
            hg = hg.reshape(gt, D)
            staged.append((hg,) + _route_and_dispatch(hg, words.reshape(gt, HALF),
                                                      w_router[l], router_bias[l]))
        out = None
        for g, (hg, xs, slots, gates, tiles) in enumerate(staged):
            ys = _experts(xs, *tiles, w_exp_gate[l], w_exp_up[l], w_exp_down[l])
            routed = _sc_combine(ys, slots, gates, gt)
            out = _final(hg, routed, w_sgu, w_sd, ln2_g[l], ln2_b[l], alpha, out, g * gt, B * S)
        h = out.reshape(B, S, D)
    return h
```

```python
import functools

import jax
import jax.numpy as jnp
from jax import lax
from jax.experimental import pallas as pl
from jax.experimental.pallas import tpu as pltpu
from jax.experimental.pallas import tpu_sc as plsc

F32 = jnp.float32
BF16 = jnp.bfloat16
I32 = jnp.int32
U32 = jnp.uint32

D_MODEL = 1024
GLA_HEADS = 4
GLA_DK = D_MODEL // 2
GLA_DV = D_MODEL
HEAD_K = GLA_DK // GLA_HEADS
HEAD_V = GLA_DV // GLA_HEADS
GATE_RANK = 16
GATE_TEMP = 16.0
POOL_WIDTH = D_MODEL // 2
POOL_GROUPS = 4
POOL_GROUP_DIM = POOL_WIDTH // POOL_GROUPS
POOL_WINDOWS = (2, 4, 8, 16)
POOL_HALO = 16
N_EXPERTS = 64
TOP_K = 8
N_GROUPS = 8
TOP_GROUPS = 4
EXPERTS_PER_GROUP = N_EXPERTS // N_GROUPS
EXPERT_FF = 256
SHARED_FF = 256
ROUTE_SCALE = 2.5
LN_EPS = 1e-5
RMS_EPS = 1e-6
LANES = 128
SUBLANES = 8

GLA_CHUNK = 256
DECAY_LIMIT = 60.0
MIX_TILE = 512
ROUTER_TILE = 1024
FINAL_TILE = 1024
EXPERT_ROWS = 1024
SC_CORES = 2
SC_SUBCORES = 16
SC_LANES = 16
SC_CHUNK = 128
SC_SUM_CHUNK = 8
HALF = D_MODEL // 2
BF16_BITS = 16
VMEM_LIMIT = 56 * 1024 * 1024

_dot = functools.partial(jnp.dot, preferred_element_type=F32)
_NT = (((1,), (1,)), ((), ()))
_TN = (((0,), (0,)), ((), ()))


def _layer_norm(y, g, b):
    mu = jnp.mean(y, axis=-1, keepdims=True)
    yc = y - mu
    var = jnp.mean(yc * yc, axis=-1, keepdims=True)
    return yc * lax.rsqrt(var + LN_EPS) * g + b


def _split3(a):
    hi = a.astype(BF16)
    r1 = a - hi.astype(F32)
    mid = r1.astype(BF16)
    lo = (r1 - mid.astype(F32)).astype(BF16)
    return hi, mid, lo


def _pack_rows(y):
    lo = lax.bitcast_convert_type(y[:, :HALF].astype(BF16).astype(F32), U32)
    hi = lax.bitcast_convert_type(y[:, HALF:].astype(BF16).astype(F32), U32)
    return lax.bitcast_convert_type(hi | (lo >> BF16_BITS), I32)


def _unpack_rows(w):
    u = lax.bitcast_convert_type(w, U32)
    lo = lax.bitcast_convert_type(u << BF16_BITS, F32)
    hi = lax.bitcast_convert_type(u & jnp.uint32(0xFFFFFFFF << BF16_BITS & 0xFFFFFFFF), F32)
    return lo, hi


def _mixer_kernel(x_ref, wqk_ref, wv_ref, wr_ref, wglr_ref, wgup_ref, bgate_ref, wu_ref,
                  wga_ref, wgb_ref, gnorm_ref, wglaup_ref, wpgrp_ref, pscale_ref, wpup_ref,
                  wout_ref, lng_ref, lnb_ref, out_ref, words_ref, state_ref, ucat_ref,
                  qk_s, v_s, b_s, *, alpha, tm):
    j = pl.program_id(1)

    @pl.when(j == 0)
    def _():
        state_ref[...] = jnp.zeros_like(state_ref)
        ucat_ref[0:POOL_HALO, :] = jnp.zeros((POOL_HALO, POOL_WIDTH), F32)

    C = GLA_CHUNK
    causal = lax.broadcasted_iota(I32, (C, C), 0) >= lax.broadcasted_iota(I32, (C, C), 1)
    tri = causal.astype(BF16)
    gnorm = gnorm_ref[...]
    states = [state_ref[h] for h in range(GLA_HEADS)]

    n_chunks = tm // C

    def finish(rows, x, intra, inter, r, gate_a, gate_b, y_pool):
        heads = []
        for h in range(GLA_HEADS):
            o = intra[h] + inter[h]
            ms = jnp.mean(o * o, axis=-1, keepdims=True)
            heads.append(o * lax.rsqrt(ms + RMS_EPS) * gnorm)
        o = jnp.concatenate(heads, axis=1) * (r * jax.nn.sigmoid(r))
        y_gla = _dot(o.astype(BF16), wglaup_ref[...])
        merged = jax.nn.sigmoid(gate_a) * y_gla + jax.nn.sigmoid(gate_b) * y_pool
        mix = _dot(merged.astype(BF16), wout_ref[...])
        y = _layer_norm(alpha * x + mix, lng_ref[...], lnb_ref[...])
        out_ref[0, rows, :] = y
        words_ref[0, rows, :] = _pack_rows(y)

    kept = []
    for c in range(n_chunks):
        rows = pl.ds(c * C, C)
        x = x_ref[0, rows, :]
        xb = x.astype(BF16)

        qk = _dot(xb, wqk_ref[...])
        v = _dot(xb, wv_ref[...])
        g_lr = _dot(xb, wglr_ref[...])
        z = _dot(g_lr.astype(BF16), wgup_ref[...]) + bgate_ref[...]
        u = _dot(xb, wu_ref[...])
        r = _dot(xb, wr_ref[...])
        log_decay = (jnp.minimum(z, 0.0) - jnp.log1p(jnp.exp(-jnp.abs(z)))) * (1.0 / GATE_TEMP)
        g_hi, g_mid, _ = _split3(log_decay)
        gate_a = _dot(xb, wga_ref[...])
        bcum = _dot(tri, g_hi) + _dot(tri, g_mid)

        ucat_ref[POOL_HALO:, :] = u
        pos = j * tm + c * C - POOL_HALO + lax.broadcasted_iota(I32, (C + POOL_HALO, 1), 0)

        def pool_group(gi):
            window = POOL_WINDOWS[gi]
            a = ucat_ref[:, gi * POOL_GROUP_DIM:(gi + 1) * POOL_GROUP_DIM]
            s = a
            step = 1
            while step < window:
                s = s + pltpu.roll(s, step, 0)
                step *= 2
            count = jnp.clip(pos + 1, 1, window).astype(F32)
            pooled = (s / count - a)[POOL_HALO:, :]
            return _dot(pooled.astype(BF16), wpgrp_ref[gi])

        intra, inter, mixed = [], [], []
        for h in range(GLA_HEADS):
            ks = slice(h * HEAD_K, (h + 1) * HEAD_K)
            b = bcum[:, ks]
            b_last = b[C - 1:C, :]
            q_h = qk[:, ks] * (HEAD_K ** -0.5)
            k_h = qk[:, GLA_DK + h * HEAD_K:GLA_DK + (h + 1) * HEAD_K]
            v_h = v[:, h * HEAD_V:(h + 1) * HEAD_V].astype(BF16)
            q_in = (q_h * jnp.exp(b)).astype(BF16)
            k_out = (k_h * jnp.exp(-b)).astype(BF16)
            k_end = (k_h * jnp.exp(b_last - b)).astype(BF16)
            scores = lax.dot_general(q_in, k_out, _NT, preferred_element_type=F32)
            scores = jnp.where(causal, scores, 0.0).astype(BF16)
            st = states[h]
            intra.append(_dot(scores, v_h))
            inter.append(lax.dot_general(q_in, st.astype(BF16), _NT, preferred_element_type=F32))
            states[h] = st * jnp.exp(b_last) + lax.dot_general(
                v_h, k_end, _TN, preferred_element_type=F32)
            mixed.append(pool_group(h))
            if h == 1:
                gate_b = _dot(xb, wgb_ref[...])
        ucat_ref[0:POOL_HALO, :] = u[C - POOL_HALO:, :]
        mixed = jnp.concatenate(mixed, axis=1) * pscale_ref[...]
        y_pool = _dot(mixed.astype(BF16), wpup_ref[...])
        finish(rows, x, intra, inter, r, gate_a, gate_b, y_pool)
        kept.append((qk, v, bcum, inter, y_pool))

    for h in range(GLA_HEADS):
        state_ref[h] = states[h]

    steepest = kept[0][2][C - 1:C, :]
    for c in range(1, n_chunks):
        steepest = jnp.minimum(steepest, kept[c][2][C - 1:C, :])

    @pl.when(jnp.max(-steepest) > DECAY_LIMIT)
    def _():
        row_id = lax.broadcasted_iota(I32, (C, 1), 0)
        for c in range(n_chunks):
            rows = pl.ds(c * C, C)
            qk, v, bcum, inter, y_pool = kept[c]
            qk_s[...] = qk
            v_s[...] = v
            b_s[...] = bcum
            intra = []
            for h in range(GLA_HEADS):
                ks = slice(h * HEAD_K, (h + 1) * HEAD_K)
                kks = slice(GLA_DK + h * HEAD_K, GLA_DK + (h + 1) * HEAD_K)
                vs = slice(h * HEAD_V, (h + 1) * HEAD_V)
                q_h = qk[:, ks] * (HEAD_K ** -0.5)
                b_h = bcum[:, ks]

                def add_rows(group, acc):
                    first = pl.multiple_of(group * SUBLANES, SUBLANES)
                    b_g = b_s[pl.ds(first, SUBLANES), ks]
                    k_g = qk_s[pl.ds(first, SUBLANES), kks]
                    v_g = v_s[pl.ds(first, SUBLANES), vs]
                    for i in range(SUBLANES):
                        decay = jnp.exp(jnp.minimum(b_h - b_g[i:i + 1, :], 0.0))
                        decay = jnp.where(row_id >= first + i, decay, 0.0)
                        score = jnp.sum(q_h * k_g[i:i + 1, :] * decay, axis=1, keepdims=True)
                        acc = acc + score * v_g[i:i + 1, :]
                    return acc

                intra.append(lax.fori_loop(0, C // SUBLANES, add_rows, jnp.zeros((C, HEAD_V), F32)))
            x = x_ref[0, rows, :]
            xb = x.astype(BF16)
            finish(rows, x, intra, inter, _dot(xb, wr_ref[...]), _dot(xb, wga_ref[...]),
                   _dot(xb, wgb_ref[...]), y_pool)


def _const_spec(shape):
    zeros = (0,) * len(shape)
    return pl.BlockSpec(shape, lambda *_: zeros, pipeline_mode=pl.Buffered(1))


def _mixer_operands(w_in, w_gate_up, b_gate, gla_norm_g, w_gla_up, w_pool_grp, pool_scale,
                    w_pool_up, w_out, ln_g, ln_b):
    D = D_MODEL
    c = 0
    w_qk = w_in[:, c:c + 2 * GLA_DK].astype(BF16); c += 2 * GLA_DK
    w_v = w_in[:, c:c + GLA_DV].astype(BF16); c += GLA_DV
    w_r = w_in[:, c:c + GLA_DV].astype(BF16); c += GLA_DV
    w_glr = jnp.pad(w_in[:, c:c + GATE_RANK], ((0, 0), (0, LANES - GATE_RANK))).astype(BF16); c += GATE_RANK
    w_u = w_in[:, c:c + POOL_WIDTH].astype(BF16); c += POOL_WIDTH
    w_ga = w_in[:, c:c + D].astype(BF16); c += D
    w_gb = w_in[:, c:c + D].astype(BF16); c += D
    assert c == w_in.shape[1]
    w_gup = jnp.pad(w_gate_up, ((0, LANES - GATE_RANK), (0, 0))).astype(BF16)
    return (
        w_qk, w_v, w_r, w_glr, w_gup, b_gate.reshape(1, GLA_DK), w_u, w_ga, w_gb,
        gla_norm_g.reshape(1, HEAD_V), w_gla_up.astype(BF16), w_pool_grp.astype(BF16),
        pool_scale.reshape(1, POOL_WIDTH), w_pool_up.astype(BF16), w_out.astype(BF16),
        ln_g.reshape(1, D), ln_b.reshape(1, D))


def _mixer(x, operands, alpha):
    B, S, D = x.shape
    tm = MIX_TILE
    assert D == D_MODEL and S % tm == 0 and tm % GLA_CHUNK == 0
    return pl.pallas_call(
        functools.partial(_mixer_kernel, alpha=alpha, tm=tm),
        name="mixer",
        grid=(B, S // tm),
        in_specs=[pl.BlockSpec((1, tm, D), lambda b, j: (b, j, 0))]
        + [_const_spec(w.shape) for w in operands],
        out_specs=[pl.BlockSpec((1, tm, D), lambda b, j: (b, j, 0)),
                   pl.BlockSpec((1, tm, HALF), lambda b, j: (b, j, 0))],
        out_shape=[jax.ShapeDtypeStruct((B, S, D), F32),
                   jax.ShapeDtypeStruct((B, S, HALF), I32)],
        scratch_shapes=[
            pltpu.VMEM((GLA_HEADS, HEAD_V, HEAD_K), F32),
            pltpu.VMEM((GLA_CHUNK + POOL_HALO, POOL_WIDTH), F32),
            pltpu.VMEM((GLA_CHUNK, 2 * GLA_DK), F32),
            pltpu.VMEM((GLA_CHUNK, GLA_DV), F32),
            pltpu.VMEM((GLA_CHUNK, GLA_DK), F32),
        ],
        compiler_params=pltpu.CompilerParams(
            dimension_semantics=("arbitrary", "arbitrary"), vmem_limit_bytes=VMEM_LIMIT),
    )(x, *operands)


def _rank_desc(vals, n):
    idx = lax.broadcasted_iota(I32, vals.shape, 0)
    rank = jnp.zeros(vals.shape, I32)
    for other in range(n):
        o = vals[other:other + 1, :]
        before = (o > vals) | ((o == vals) & (other < idx))
        rank = rank + before.astype(I32)
    return rank


def _router_kernel(h_ref, wrt_ref, bias_ref, earlier_ref, eid_ref, pos_ref, wk_ref, cnt_ref, carry_ref):
    i = pl.program_id(0)

    @pl.when(i == 0)
    def _():
        carry_ref[...] = jnp.zeros_like(carry_ref)

    h = h_ref[...]
    tr = h.shape[0]
    w_hi, w_mid, _ = _split3(wrt_ref[...])
    h_hi, h_mid, _ = _split3(h)
    by_hi = lax.dot_general(jnp.concatenate([w_hi, w_mid], axis=0), h_hi, _NT,
                            preferred_element_type=F32)
    logits = (by_hi[:N_EXPERTS] + by_hi[N_EXPERTS:]
              + lax.dot_general(w_hi, h_mid, _NT, preferred_element_type=F32))
    scores = jax.nn.sigmoid(logits)
    biased = scores + bias_ref[...]
    grp = biased.reshape(N_GROUPS, EXPERTS_PER_GROUP, tr)
    slot = lax.broadcasted_iota(I32, grp.shape, 1)
    top1 = jnp.max(grp, axis=1, keepdims=True)
    first = jnp.min(jnp.where(grp == top1, slot, EXPERTS_PER_GROUP), axis=1, keepdims=True)
    top2 = jnp.max(jnp.where(slot == first, -jnp.inf, grp), axis=1)
    grp_score = top1[:, 0, :] + top2
    grp_on = _rank_desc(grp_score, N_GROUPS) < TOP_GROUPS
    emask = jnp.broadcast_to(grp_on[:, None, :], grp.shape).reshape(N_EXPERTS, tr)
    masked = jnp.where(emask, biased, -jnp.inf)

    eidx = lax.broadcasted_iota(I32, (N_EXPERTS, tr), 0).astype(F32)
    rest = masked
    eids, sels = [], []
    for _ in range(TOP_K):
        top = jnp.max(rest, axis=0, keepdims=True)
        eid = jnp.min(jnp.where(rest == top, eidx, float(N_EXPERTS)), axis=0, keepdims=True)
        hit = eidx == eid
        rest = jnp.where(hit, -jnp.inf, rest)
        eids.append(eid)
        sels.append(jnp.sum(jnp.where(hit, scores, 0.0), axis=0, keepdims=True))
    eid_k = jnp.concatenate(eids, axis=0)
    sel_k = jnp.concatenate(sels, axis=0)
    gate_k = sel_k / jnp.sum(sel_k, axis=0, keepdims=True) * ROUTE_SCALE
    chosen = rest != masked

    pos = carry_ref[...] + _dot(chosen.astype(BF16), earlier_ref[...])
    carry_ref[...] += jnp.sum(chosen.astype(F32), axis=1, keepdims=True)
    cnt_ref[...] = carry_ref[...]

    poss = [jnp.sum(jnp.where(eidx == eid, pos, 0.0), axis=0, keepdims=True) for eid in eids]
    eid_ref[...] = eid_k.astype(I32)
    pos_ref[...] = jnp.concatenate(poss, axis=0).astype(I32)
    wk_ref[...] = gate_k


def _router(h2, w_router, router_bias):
    T, D = h2.shape
    tr = ROUTER_TILE
    assert T % tr == 0
    token = jnp.arange(tr, dtype=I32)
    earlier = (token[:, None] < token[None, :]).astype(BF16)
    return pl.pallas_call(
        _router_kernel,
        name="router",
        grid=(T // tr,),
        in_specs=[pl.BlockSpec((tr, D), lambda i: (i, 0)),
                  _const_spec((N_EXPERTS, D)), _const_spec((N_EXPERTS, 1)), _const_spec((tr, tr))],
        out_specs=[pl.BlockSpec((TOP_K, tr), lambda i: (0, i)),
                   pl.BlockSpec((TOP_K, tr), lambda i: (0, i)),
                   pl.BlockSpec((TOP_K, tr), lambda i: (0, i)),
                   pl.BlockSpec((N_EXPERTS, 1), lambda i: (0, 0))],
        out_shape=[jax.ShapeDtypeStruct((TOP_K, T), I32),
                   jax.ShapeDtypeStruct((TOP_K, T), I32),
                   jax.ShapeDtypeStruct((TOP_K, T), F32),
                   jax.ShapeDtypeStruct((N_EXPERTS, 1), F32)],
        scratch_shapes=[pltpu.VMEM((N_EXPERTS, 1), F32)],
        compiler_params=pltpu.CompilerParams(
            dimension_semantics=("arbitrary",), vmem_limit_bytes=VMEM_LIMIT),
    )(h2, w_router.T, router_bias.reshape(N_EXPERTS, 1), earlier)


def _slot_kernel(offs_ref, eid_ref, pos_ref, slot_ref):
    eid = eid_ref[...]
    base = jnp.zeros(eid.shape, I32)
    for e in range(N_EXPERTS):
        base = jnp.where(eid == e, offs_ref[e], base)
    slot_ref[...] = pos_ref[...] + base


def _slots(offsets, eid, pos):
    shape = eid.shape
    return pl.pallas_call(
        _slot_kernel,
        name="slots",
        grid_spec=pltpu.PrefetchScalarGridSpec(
            num_scalar_prefetch=1, grid=(1,),
            in_specs=[pl.BlockSpec(shape, lambda i, offs: (0, 0)),
                      pl.BlockSpec(shape, lambda i, offs: (0, 0))],
            out_specs=pl.BlockSpec(shape, lambda i, offs: (0, 0))),
        out_shape=jax.ShapeDtypeStruct(shape, I32),
        compiler_params=pltpu.CompilerParams(vmem_limit_bytes=VMEM_LIMIT),
    )(offsets, eid, pos)


def _sc_mesh():
    return plsc.VectorSubcoreMesh(core_axis_name="core", subcore_axis_name="subcore",
                                  num_cores=SC_CORES, num_subcores=SC_SUBCORES)


def _sc_worker():
    return lax.axis_index("subcore") * SC_CORES + lax.axis_index("core")


def _sc_dispatch(words, slots, n_rows):
    T = words.shape[0]
    n_chunks = slots.shape[0]
    workers = SC_CORES * SC_SUBCORES
    assert T == n_chunks * SC_CHUNK and n_chunks % workers == 0
    per_worker = n_chunks // workers

    @functools.partial(
        pl.kernel, mesh=_sc_mesh(),
        out_type=jax.ShapeDtypeStruct((n_rows, HALF), I32),
        scratch_types=[pltpu.VMEM((SC_CHUNK, HALF), I32), pltpu.VMEM((TOP_K, SC_CHUNK), I32),
                       pltpu.SemaphoreType.DMA])
    def dispatch(words_hbm, slots_hbm, out_hbm, rows_v, idx_v, sem):
        first = _sc_worker() * per_worker

        @pl.loop(0, per_worker)
        def _(i):
            c = first + i
            pltpu.sync_copy(slots_hbm.at[c], idx_v)
            pltpu.sync_copy(words_hbm.at[pl.ds(c * SC_CHUNK, SC_CHUNK)], rows_v)
            copies = [pltpu.async_copy(rows_v, out_hbm.at[idx_v.at[k]], sem) for k in range(TOP_K)]
            for cp in copies:
                cp.wait()

    return dispatch(words, slots)


def _sc_combine(rows, slots, gates, T):
    CH = SC_SUM_CHUNK
    steps = SC_CHUNK // CH
    n_blocks = slots.shape[0]
    workers = SC_CORES * SC_SUBCORES
    assert T == n_blocks * SC_CHUNK and n_blocks % workers == 0 and steps >= 2
    per_worker = n_blocks // workers
    vecs = HALF // SC_LANES

    @functools.partial(
        pl.kernel, mesh=_sc_mesh(),
        compiler_params=pltpu.CompilerParams(needs_layout_passes=False),
        out_type=jax.ShapeDtypeStruct((T, HALF), I32),
        scratch_types=[pltpu.VMEM((2, TOP_K, CH, HALF), I32), pltpu.VMEM((2, CH, HALF), I32),
                       pltpu.VMEM((TOP_K, SC_CHUNK), I32), pltpu.VMEM((TOP_K, SC_CHUNK), F32),
                       pltpu.SemaphoreType.DMA((2,)), pltpu.SemaphoreType.DMA((2,))])
    def combine(rows_hbm, slots_hbm, gates_hbm, out_hbm, rows_v, sum_v, idx_v, gates_v, sem_in, sem_out):
        first = _sc_worker() * per_worker

        def gathers(j, b):
            return [pltpu.make_async_copy(rows_hbm.at[idx_v.at[k, pl.ds(j * CH, CH)]], rows_v.at[b, k],
                                          sem_in.at[b]) for k in range(TOP_K)]

        def write_out(token0, b):
            return pltpu.make_async_copy(sum_v.at[b], out_hbm.at[pl.ds(token0, CH)], sem_out.at[b])

        def add_rows(j, b):
            @pl.loop(0, CH)
            def _(t):
                token = jnp.full((SC_LANES,), j * CH + t, I32)
                gate = []
                for k in range(TOP_K):
                    g = plsc.load_gather(gates_v.at[k], [token])
                    gate.append(plsc.pack(g, g, format=plsc.PackFormat.INTERLEAVED))

                @plsc.parallel_loop(0, vecs, 1, unroll=2)
                def _(n):
                    lanes = pl.ds(n * SC_LANES, SC_LANES)
                    terms = [plsc.bitcast(rows_v[b, k, t, lanes], BF16) * gate[k] for k in range(TOP_K)]
                    while len(terms) > 1:
                        terms = [x + y for x, y in zip(terms[::2], terms[1::2])]
                    sum_v[b, t, lanes] = plsc.bitcast(terms[0], I32)

        @pl.loop(0, per_worker)
        def _(i):
            block = first + i
            token0 = block * SC_CHUNK
            pltpu.sync_copy(slots_hbm.at[block], idx_v)
            pltpu.sync_copy(gates_hbm.at[block], gates_v)
            for cp in gathers(0, 0):
                cp.start()
            for j in range(steps):
                b = j % 2
                if j + 1 < steps:
                    for cp in gathers(j + 1, 1 - b):
                        cp.start()
                for cp in gathers(j, b):
                    cp.wait()
                if j >= 2:
                    write_out(token0 + (j - 2) * CH, b).wait()
                add_rows(j, b)
                write_out(token0 + j * CH, b).start()
            write_out(token0 + (steps - 2) * CH, steps % 2).wait()
            write_out(token0 + (steps - 1) * CH, (steps - 1) % 2).wait()

    return combine(rows, slots, gates)


def _expert_kernel(tile_expert_ref, tile_rows_ref, tile_block_ref, xs_ref, wg_ref, wu_ref, wd_ref,
                   ys_ref, wgu_ref, wdn_ref):
    del tile_block_ref
    i = pl.program_id(0)
    e = tile_expert_ref[i]
    n_valid = tile_rows_ref[i]
    e_prev = tile_expert_ref[jnp.maximum(i - 1, 0)]

    @pl.when((i == 0) | (e != e_prev))
    def _():
        wgu_ref[:, :EXPERT_FF] = wg_ref[0].astype(BF16)
        wgu_ref[:, EXPERT_FF:] = wu_ref[0].astype(BF16)
        wdn_ref[...] = wd_ref[0].astype(BF16)

    @pl.when(n_valid > 0)
    def _():
        words = xs_ref[...]
        live = lax.broadcasted_iota(I32, (words.shape[0], 1), 0) < n_valid
        lo, hi = _unpack_rows(jnp.where(live, words, 0))
        a = _dot(lo.astype(BF16), wgu_ref[:HALF, :]) + _dot(hi.astype(BF16), wgu_ref[HALF:, :])
        act = a[:, :EXPERT_FF]
        hid = act * jax.nn.sigmoid(act) * a[:, EXPERT_FF:]
        ys_ref[...] = _pack_rows(_dot(hid.astype(BF16), wdn_ref[...]))


def _experts(xs, tile_expert, tile_rows, tile_block, w_exp_gate, w_exp_up, w_exp_down):
    n_rows = xs.shape[0]
    R = EXPERT_ROWS
    D = D_MODEL
    assert n_rows % R == 0
    return pl.pallas_call(
        _expert_kernel,
        name="experts",
        grid_spec=pltpu.PrefetchScalarGridSpec(
            num_scalar_prefetch=3, grid=(n_rows // R,),
            in_specs=[pl.BlockSpec((R, HALF), lambda i, te, tr, tb: (tb[i], 0)),
                      pl.BlockSpec((1, D, EXPERT_FF), lambda i, te, tr, tb: (te[i], 0, 0)),
                      pl.BlockSpec((1, D, EXPERT_FF), lambda i, te, tr, tb: (te[i], 0, 0)),
                      pl.BlockSpec((1, EXPERT_FF, D), lambda i, te, tr, tb: (te[i], 0, 0))],
            out_specs=pl.BlockSpec((R, HALF), lambda i, te, tr, tb: (tb[i], 0)),
            scratch_shapes=[pltpu.VMEM((D, 2 * EXPERT_FF), BF16), pltpu.VMEM((EXPERT_FF, D), BF16)]),
        out_shape=jax.ShapeDtypeStruct((n_rows, HALF), I32),
        compiler_params=pltpu.CompilerParams(
            dimension_semantics=("arbitrary",), vmem_limit_bytes=VMEM_LIMIT),
    )(tile_expert, tile_rows, tile_block, xs, w_exp_gate, w_exp_up, w_exp_down)


def _final_kernel(h_ref, routed_ref, wsgu_ref, wsd_ref, lng_ref, lnb_ref, out_ref, *, alpha):
    h = h_ref[...]
    a = _dot(h.astype(BF16), wsgu_ref[...])
    act = a[:, :SHARED_FF]
    hid = act * jax.nn.sigmoid(act) * a[:, SHARED_FF:]
    shared = _dot(hid.astype(BF16), wsd_ref[...])
    lo, hi = _unpack_rows(routed_ref[...])
    ffn = shared + jnp.concatenate([lo, hi], axis=1)
    out_ref[...] = _layer_norm(alpha * h + ffn, lng_ref[...], lnb_ref[...])


def _final(h2, routed, w_sgu, w_sd, ln_g, ln_b, alpha):
    T, D = h2.shape
    tm = FINAL_TILE
    assert T % tm == 0
    return pl.pallas_call(
        functools.partial(_final_kernel, alpha=alpha),
        name="final",
        grid=(T // tm,),
        in_specs=[pl.BlockSpec((tm, D), lambda i: (i, 0)),
                  pl.BlockSpec((tm, HALF), lambda i: (i, 0)),
                  _const_spec((D, 2 * SHARED_FF)), _const_spec((SHARED_FF, D)),
                  _const_spec((1, D)), _const_spec((1, D))],
        out_specs=pl.BlockSpec((tm, D), lambda i: (i, 0)),
        out_shape=jax.ShapeDtypeStruct((T, D), F32),
        compiler_params=pltpu.CompilerParams(
            dimension_semantics=("arbitrary",), vmem_limit_bytes=VMEM_LIMIT),
    )(h2, routed, w_sgu, w_sd, ln_g.reshape(1, D), ln_b.reshape(1, D))


def _route_and_dispatch(h2, words, w_router, router_bias):
    T, D = h2.shape
    R = EXPERT_ROWS
    eid, pos, gates, counts = _router(h2, w_router, router_bias)
    counts = counts.reshape(N_EXPERTS).astype(I32)
    padded = (counts + (R - 1)) // R * R
    ends = jnp.cumsum(padded)
    offsets = ends - padded
    n_rows = T * TOP_K + N_EXPERTS * R
    tile_start = jnp.arange(n_rows // R, dtype=I32) * R
    tile_block = jnp.minimum(tile_start, jnp.maximum(ends[-1] - R, 0)) // R
    past = ((tile_block * R)[:, None] >= ends[None, :]).astype(I32)
    tile_expert = jnp.minimum(jnp.sum(past, axis=1), N_EXPERTS - 1)
    mine = (tile_expert[:, None] == jnp.arange(N_EXPERTS, dtype=I32)[None, :]).astype(I32)
    valid_end = jnp.sum(mine * (offsets + counts)[None, :], axis=1)
    tile_rows = jnp.clip(valid_end - tile_start, 0, R).astype(I32)
    slots = _slots(offsets.astype(I32), eid, pos)

    def chunked(a):
        return a.reshape(TOP_K, T // SC_CHUNK, SC_CHUNK).transpose(1, 0, 2)

    slots = chunked(slots)
    xs = _sc_dispatch(words, slots, n_rows)
    return xs, slots, chunked(gates), (tile_expert, tile_rows, tile_block.astype(I32))


def kernel(x, w_in, w_gate_up, b_gate, gla_norm_g, w_gla_up, w_pool_grp, pool_scale, w_pool_up,
           w_out, ln1_g, ln1_b, w_router, router_bias, w_exp_gate, w_exp_up, w_exp_down,
           w_sh_gate, w_sh_up, w_sh_down, ln2_g, ln2_b):
    B, S, D = x.shape
    T = B * S
    depth = w_in.shape[0]
    alpha = (2.0 * depth) ** 0.25
    h = x
    for l in range(depth):
        mix_ops = _mixer_operands(w_in[l], w_gate_up[l], b_gate[l], gla_norm_g[l], w_gla_up[l],
                                  w_pool_grp[l], pool_scale[l], w_pool_up[l], w_out[l],
                                  ln1_g[l], ln1_b[l])
        w_sgu = jnp.concatenate([w_sh_gate[l], w_sh_up[l]], axis=-1).astype(BF16)
        w_sd = w_sh_down[l].astype(BF16)
        h, words = _mixer(h, mix_ops, alpha)
        h = h.reshape(T, D)
        xs, slots, gates, tiles = _route_and_dispatch(h, words.reshape(T, HALF),
                                                      w_router[l], router_bias[l])
        ys = _experts(xs, *tiles, w_exp_gate[l], w_exp_up[l], w_exp_down[l])
        routed = _sc_combine(ys, slots, gates, T)
        h = _final(h, routed, w_sgu, w_sd, ln2_g[l], ln2_b[l], alpha).reshape(B, S, D)
    return h
```

```python
import functools

import jax
import jax.numpy as jnp
from jax import lax
from jax.experimental import pallas as pl
from jax.experimental.pallas import tpu as pltpu
from jax.experimental.pallas import tpu_sc as plsc

F32 = jnp.float32
BF16 = jnp.bfloat16
I32 = jnp.int32
U32 = jnp.uint32

D_MODEL = 1024
GLA_HEADS = 4
GLA_DK = D_MODEL // 2
GLA_DV = D_MODEL
HEAD_K = GLA_DK // GLA_HEADS
HEAD_V = GLA_DV // GLA_HEADS
GATE_RANK = 16
GATE_TEMP = 16.0
POOL_WIDTH = D_MODEL // 2
POOL_GROUPS = 4
POOL_GROUP_DIM = POOL_WIDTH // POOL_GROUPS
POOL_WINDOWS = (2, 4, 8, 16)
POOL_HALO = 16
N_EXPERTS = 64
TOP_K = 8
N_GROUPS = 8
TOP_GROUPS = 4
EXPERTS_PER_GROUP = N_EXPERTS // N_GROUPS
EXPERT_FF = 256
SHARED_FF = 256
ROUTE_SCALE = 2.5
LN_EPS = 1e-5
RMS_EPS = 1e-6
LANES = 128
SUBLANES = 8

GLA_CHUNK = 256
DECAY_LIMIT = 60.0
MIX_TILE = 512
ROUTER_TILE = 1024
FINAL_TILE = 1024
EXPERT_ROWS = 1024
SC_CORES = 2
SC_SUBCORES = 16
SC_LANES = 16
SC_CHUNK = 128
SC_SUM_CHUNK = 8
COMBINE_PARTS = 2
HALF = D_MODEL // 2
BF16_BITS = 16
VMEM_LIMIT = 56 * 1024 * 1024

_dot = functools.partial(jnp.dot, preferred_element_type=F32)
_NT = (((1,), (1,)), ((), ()))
_TN = (((0,), (0,)), ((), ()))


def _layer_norm(y, g, b):
    mu = jnp.mean(y, axis=-1, keepdims=True)
    yc = y - mu
    var = jnp.mean(yc * yc, axis=-1, keepdims=True)
    return yc * lax.rsqrt(var + LN_EPS) * g + b


def _split3(a):
    hi = a.astype(BF16)
    r1 = a - hi.astype(F32)
    mid = r1.astype(BF16)
    lo = (r1 - mid.astype(F32)).astype(BF16)
    return hi, mid, lo


def _pack_rows(y):
    lo = lax.bitcast_convert_type(y[:, :HALF].astype(BF16).astype(F32), U32)
    hi = lax.bitcast_convert_type(y[:, HALF:].astype(BF16).astype(F32), U32)
    return lax.bitcast_convert_type(hi | (lo >> BF16_BITS), I32)


def _unpack_rows(w):
    u = lax.bitcast_convert_type(w, U32)
    lo = lax.bitcast_convert_type(u << BF16_BITS, F32)
    hi = lax.bitcast_convert_type(u & jnp.uint32(0xFFFFFFFF << BF16_BITS & 0xFFFFFFFF), F32)
    return lo, hi


def _mixer_kernel(x_ref, wqk_ref, wv_ref, wr_ref, wglr_ref, wgup_ref, bgate_ref, wu_ref,
                  wga_ref, wgb_ref, gnorm_ref, wglaup_ref, wpgrp_ref, pscale_ref, wpup_ref,
                  wout_ref, lng_ref, lnb_ref, out_ref, words_ref, state_ref, ucat_ref,
                  qk_s, v_s, b_s, *, alpha, tm):
    j = pl.program_id(1)

    @pl.when(j == 0)
    def _():
        state_ref[...] = jnp.zeros_like(state_ref)
        ucat_ref[0:POOL_HALO, :] = jnp.zeros((POOL_HALO, POOL_WIDTH), F32)

    C = GLA_CHUNK
    causal = lax.broadcasted_iota(I32, (C, C), 0) >= lax.broadcasted_iota(I32, (C, C), 1)
    tri = causal.astype(BF16)
    gnorm = gnorm_ref[...]
    states = [state_ref[h] for h in range(GLA_HEADS)]

    n_chunks = tm // C

    def finish(rows, x, intra, inter, r, gate_a, gate_b, y_pool):
        heads = []
        for h in range(GLA_HEADS):
            o = intra[h] + inter[h]
            ms = jnp.mean(o * o, axis=-1, keepdims=True)
            heads.append(o * lax.rsqrt(ms + RMS_EPS) * gnorm)
        o = jnp.concatenate(heads, axis=1) * (r * jax.nn.sigmoid(r))
        y_gla = _dot(o.astype(BF16), wglaup_ref[...])
        merged = jax.nn.sigmoid(gate_a) * y_gla + jax.nn.sigmoid(gate_b) * y_pool
        mix = _dot(merged.astype(BF16), wout_ref[...])
        y = _layer_norm(alpha * x + mix, lng_ref[...], lnb_ref[...])
        out_ref[0, rows, :] = y
        words_ref[0, rows, :] = _pack_rows(y)

    kept = []
    for c in range(n_chunks):
        rows = pl.ds(c * C, C)
        x = x_ref[0, rows, :]
        xb = x.astype(BF16)

        qk = _dot(xb, wqk_ref[...])
        v = _dot(xb, wv_ref[...])
        g_lr = _dot(xb, wglr_ref[...])
        z = _dot(g_lr.astype(BF16), wgup_ref[...]) + bgate_ref[...]
        u = _dot(xb, wu_ref[...])
        r = _dot(xb, wr_ref[...])
        log_decay = (jnp.minimum(z, 0.0) - jnp.log1p(jnp.exp(-jnp.abs(z)))) * (1.0 / GATE_TEMP)
        g_hi, g_mid, _ = _split3(log_decay)
        gate_a = _dot(xb, wga_ref[...])
        bcum = _dot(tri, g_hi) + _dot(tri, g_mid)

        ucat_ref[POOL_HALO:, :] = u
        pos = j * tm + c * C - POOL_HALO + lax.broadcasted_iota(I32, (C + POOL_HALO, 1), 0)

        def pool_group(gi):
            window = POOL_WINDOWS[gi]
            a = ucat_ref[:, gi * POOL_GROUP_DIM:(gi + 1) * POOL_GROUP_DIM]
            s = a
            step = 1
            while step < window:
                s = s + pltpu.roll(s, step, 0)
                step *= 2
            count = jnp.clip(pos + 1, 1, window).astype(F32)
            pooled = (s / count - a)[POOL_HALO:, :]
            return _dot(pooled.astype(BF16), wpgrp_ref[gi])

        intra, inter, mixed = [], [], []
        for h in range(GLA_HEADS):
            ks = slice(h * HEAD_K, (h + 1) * HEAD_K)
            b = bcum[:, ks]
            b_last = b[C - 1:C, :]
            q_h = qk[:, ks] * (HEAD_K ** -0.5)
            k_h = qk[:, GLA_DK + h * HEAD_K:GLA_DK + (h + 1) * HEAD_K]
            v_h = v[:, h * HEAD_V:(h + 1) * HEAD_V].astype(BF16)
            q_in = (q_h * jnp.exp(b)).astype(BF16)
            k_out = (k_h * jnp.exp(-b)).astype(BF16)
            k_end = (k_h * jnp.exp(b_last - b)).astype(BF16)
            scores = lax.dot_general(q_in, k_out, _NT, preferred_element_type=F32)
            scores = jnp.where(causal, scores, 0.0).astype(BF16)
            st = states[h]
            intra.append(_dot(scores, v_h))
            inter.append(lax.dot_general(q_in, st.astype(BF16), _NT, preferred_element_type=F32))
            states[h] = st * jnp.exp(b_last) + lax.dot_general(
                v_h, k_end, _TN, preferred_element_type=F32)
            mixed.append(pool_group(h))
            if h == 1:
                gate_b = _dot(xb, wgb_ref[...])
        ucat_ref[0:POOL_HALO, :] = u[C - POOL_HALO:, :]
        mixed = jnp.concatenate(mixed, axis=1) * pscale_ref[...]
        y_pool = _dot(mixed.astype(BF16), wpup_ref[...])
        finish(rows, x, intra, inter, r, gate_a, gate_b, y_pool)
        kept.append((qk, v, bcum, inter, y_pool))

    for h in range(GLA_HEADS):
        state_ref[h] = states[h]

    steepest = kept[0][2][C - 1:C, :]
    for c in range(1, n_chunks):
        steepest = jnp.minimum(steepest, kept[c][2][C - 1:C, :])

    @pl.when(jnp.max(-steepest) > DECAY_LIMIT)
    def _():
        row_id = lax.broadcasted_iota(I32, (C, 1), 0)
        for c in range(n_chunks):
            rows = pl.ds(c * C, C)
            qk, v, bcum, inter, y_pool = kept[c]
            qk_s[...] = qk
            v_s[...] = v
            b_s[...] = bcum
            intra = []
            for h in range(GLA_HEADS):
                ks = slice(h * HEAD_K, (h + 1) * HEAD_K)
                kks = slice(GLA_DK + h * HEAD_K, GLA_DK + (h + 1) * HEAD_K)
                vs = slice(h * HEAD_V, (h + 1) * HEAD_V)
                q_h = qk[:, ks] * (HEAD_K ** -0.5)
                b_h = bcum[:, ks]

                def add_rows(group, acc):
                    first = pl.multiple_of(group * SUBLANES, SUBLANES)
                    b_g = b_s[pl.ds(first, SUBLANES), ks]
                    k_g = qk_s[pl.ds(first, SUBLANES), kks]
                    v_g = v_s[pl.ds(first, SUBLANES), vs]
                    for i in range(SUBLANES):
                        decay = jnp.exp(jnp.minimum(b_h - b_g[i:i + 1, :], 0.0))
                        decay = jnp.where(row_id >= first + i, decay, 0.0)
                        score = jnp.sum(q_h * k_g[i:i + 1, :] * decay, axis=1, keepdims=True)
                        acc = acc + score * v_g[i:i + 1, :]
                    return acc

                intra.append(lax.fori_loop(0, C // SUBLANES, add_rows, jnp.zeros((C, HEAD_V), F32)))
            x = x_ref[0, rows, :]
            xb = x.astype(BF16)
            finish(rows, x, intra, inter, _dot(xb, wr_ref[...]), _dot(xb, wga_ref[...]),
                   _dot(xb, wgb_ref[...]), y_pool)


def _const_spec(shape):
    zeros = (0,) * len(shape)
    return pl.BlockSpec(shape, lambda *_: zeros, pipeline_mode=pl.Buffered(1))


def _mixer_operands(w_in, w_gate_up, b_gate, gla_norm_g, w_gla_up, w_pool_grp, pool_scale,
                    w_pool_up, w_out, ln_g, ln_b):
    D = D_MODEL
    c = 0
    w_qk = w_in[:, c:c + 2 * GLA_DK].astype(BF16); c += 2 * GLA_DK
    w_v = w_in[:, c:c + GLA_DV].astype(BF16); c += GLA_DV
    w_r = w_in[:, c:c + GLA_DV].astype(BF16); c += GLA_DV
    w_glr = jnp.pad(w_in[:, c:c + GATE_RANK], ((0, 0), (0, LANES - GATE_RANK))).astype(BF16); c += GATE_RANK
    w_u = w_in[:, c:c + POOL_WIDTH].astype(BF16); c += POOL_WIDTH
    w_ga = w_in[:, c:c + D].astype(BF16); c += D
    w_gb = w_in[:, c:c + D].astype(BF16); c += D
    assert c == w_in.shape[1]
    w_gup = jnp.pad(w_gate_up, ((0, LANES - GATE_RANK), (0, 0))).astype(BF16)
    return (
        w_qk, w_v, w_r, w_glr, w_gup, b_gate.reshape(1, GLA_DK), w_u, w_ga, w_gb,
        gla_norm_g.reshape(1, HEAD_V), w_gla_up.astype(BF16), w_pool_grp.astype(BF16),
        pool_scale.reshape(1, POOL_WIDTH), w_pool_up.astype(BF16), w_out.astype(BF16),
        ln_g.reshape(1, D), ln_b.reshape(1, D))


def _mixer(x, operands, alpha):
    B, S, D = x.shape
    tm = MIX_TILE
    assert D == D_MODEL and S % tm == 0 and tm % GLA_CHUNK == 0
    return pl.pallas_call(
        functools.partial(_mixer_kernel, alpha=alpha, tm=tm),
        name="mixer",
        grid=(B, S // tm),
        in_specs=[pl.BlockSpec((1, tm, D), lambda b, j: (b, j, 0))]
        + [_const_spec(w.shape) for w in operands],
        out_specs=[pl.BlockSpec((1, tm, D), lambda b, j: (b, j, 0)),
                   pl.BlockSpec((1, tm, HALF), lambda b, j: (b, j, 0))],
        out_shape=[jax.ShapeDtypeStruct((B, S, D), F32),
                   jax.ShapeDtypeStruct((B, S, HALF), I32)],
        scratch_shapes=[
            pltpu.VMEM((GLA_HEADS, HEAD_V, HEAD_K), F32),
            pltpu.VMEM((GLA_CHUNK + POOL_HALO, POOL_WIDTH), F32),
            pltpu.VMEM((GLA_CHUNK, 2 * GLA_DK), F32),
            pltpu.VMEM((GLA_CHUNK, GLA_DV), F32),
            pltpu.VMEM((GLA_CHUNK, GLA_DK), F32),
        ],
        compiler_params=pltpu.CompilerParams(
            dimension_semantics=("arbitrary", "arbitrary"), vmem_limit_bytes=VMEM_LIMIT),
    )(x, *operands)


def _rank_desc(vals, n):
    idx = lax.broadcasted_iota(I32, vals.shape, 0)
    rank = jnp.zeros(vals.shape, I32)
    for other in range(n):
        o = vals[other:other + 1, :]
        before = (o > vals) | ((o == vals) & (other < idx))
        rank = rank + before.astype(I32)
    return rank


def _router_kernel(h_ref, wrt_ref, bias_ref, eid_ref, pos_ref, wk_ref, cnt_ref, carry_ref):
    i = pl.program_id(0)

    @pl.when(i == 0)
    def _():
        carry_ref[...] = jnp.zeros_like(carry_ref)

    h = h_ref[...]
    tr = h.shape[0]
    w_hi, w_mid, _ = _split3(wrt_ref[...])
    h_hi, h_mid, _ = _split3(h)
    by_hi = lax.dot_general(jnp.concatenate([w_hi, w_mid], axis=0), h_hi, _NT,
                            preferred_element_type=F32)
    logits = (by_hi[:N_EXPERTS] + by_hi[N_EXPERTS:]
              + lax.dot_general(w_hi, h_mid, _NT, preferred_element_type=F32))
    scores = jax.nn.sigmoid(logits)
    biased = scores + bias_ref[...]
    grp = biased.reshape(N_GROUPS, EXPERTS_PER_GROUP, tr)
    slot = lax.broadcasted_iota(I32, grp.shape, 1)
    top1 = jnp.max(grp, axis=1, keepdims=True)
    first = jnp.min(jnp.where(grp == top1, slot, EXPERTS_PER_GROUP), axis=1, keepdims=True)
    top2 = jnp.max(jnp.where(slot == first, -jnp.inf, grp), axis=1)
    grp_score = top1[:, 0, :] + top2
    grp_on = _rank_desc(grp_score, N_GROUPS) < TOP_GROUPS
    emask = jnp.broadcast_to(grp_on[:, None, :], grp.shape).reshape(N_EXPERTS, tr)
    masked = jnp.where(emask, biased, -jnp.inf)

    eidx = lax.broadcasted_iota(I32, (N_EXPERTS, tr), 0).astype(F32)
    rest = masked
    eids, sels = [], []
    for _ in range(TOP_K):
        top = jnp.max(rest, axis=0, keepdims=True)
        eid = jnp.min(jnp.where(rest == top, eidx, float(N_EXPERTS)), axis=0, keepdims=True)
        hit = eidx == eid
        rest = jnp.where(hit, -jnp.inf, rest)
        eids.append(eid)
        sels.append(jnp.sum(jnp.where(hit, scores, 0.0), axis=0, keepdims=True))
    eid_k = jnp.concatenate(eids, axis=0)
    sel_k = jnp.concatenate(sels, axis=0)
    gate_k = sel_k / jnp.sum(sel_k, axis=0, keepdims=True) * ROUTE_SCALE
    chosen = rest != masked

    t_row = lax.broadcasted_iota(I32, (tr, tr), 0)
    t_col = lax.broadcasted_iota(I32, (tr, tr), 1)
    earlier = (t_row < t_col).astype(BF16)
    pos = carry_ref[...] + _dot(chosen.astype(BF16), earlier)
    carry_ref[...] += jnp.sum(chosen.astype(F32), axis=1, keepdims=True)
    cnt_ref[...] = carry_ref[...]

    poss = [jnp.sum(jnp.where(eidx == eid, pos, 0.0), axis=0, keepdims=True) for eid in eids]
    eid_ref[...] = eid_k.astype(I32)
    pos_ref[...] = jnp.concatenate(poss, axis=0).astype(I32)
    wk_ref[...] = gate_k


def _router(h2, w_router, router_bias):
    T, D = h2.shape
    tr = ROUTER_TILE
    assert T % tr == 0
    return pl.pallas_call(
        _router_kernel,
        name="router",
        grid=(T // tr,),
        in_specs=[pl.BlockSpec((tr, D), lambda i: (i, 0)),
                  _const_spec((N_EXPERTS, D)), _const_spec((N_EXPERTS, 1))],
        out_specs=[pl.BlockSpec((TOP_K, tr), lambda i: (0, i)),
                   pl.BlockSpec((TOP_K, tr), lambda i: (0, i)),
                   pl.BlockSpec((TOP_K, tr), lambda i: (0, i)),
                   pl.BlockSpec((N_EXPERTS, 1), lambda i: (0, 0))],
        out_shape=[jax.ShapeDtypeStruct((TOP_K, T), I32),
                   jax.ShapeDtypeStruct((TOP_K, T), I32),
                   jax.ShapeDtypeStruct((TOP_K, T), F32),
                   jax.ShapeDtypeStruct((N_EXPERTS, 1), F32)],
        scratch_shapes=[pltpu.VMEM((N_EXPERTS, 1), F32)],
        compiler_params=pltpu.CompilerParams(
            dimension_semantics=("arbitrary",), vmem_limit_bytes=VMEM_LIMIT),
    )(h2, w_router.T, router_bias.reshape(N_EXPERTS, 1))


def _slot_kernel(offs_ref, eid_ref, pos_ref, slot_ref):
    eid = eid_ref[...]
    base = jnp.zeros(eid.shape, I32)
    for e in range(N_EXPERTS):
        base = jnp.where(eid == e, offs_ref[e], base)
    slot_ref[...] = pos_ref[...] + base


def _slots(offsets, eid, pos):
    shape = eid.shape
    return pl.pallas_call(
        _slot_kernel,
        name="slots",
        grid_spec=pltpu.PrefetchScalarGridSpec(
            num_scalar_prefetch=1, grid=(1,),
            in_specs=[pl.BlockSpec(shape, lambda i, offs: (0, 0)),
                      pl.BlockSpec(shape, lambda i, offs: (0, 0))],
            out_specs=pl.BlockSpec(shape, lambda i, offs: (0, 0))),
        out_shape=jax.ShapeDtypeStruct(shape, I32),
        compiler_params=pltpu.CompilerParams(vmem_limit_bytes=VMEM_LIMIT),
    )(offsets, eid, pos)


def _sc_mesh():
    return plsc.VectorSubcoreMesh(core_axis_name="core", subcore_axis_name="subcore",
                                  num_cores=SC_CORES, num_subcores=SC_SUBCORES)


def _sc_worker():
    return lax.axis_index("subcore") * SC_CORES + lax.axis_index("core")


def _sc_dispatch(words, slots, n_rows):
    T = words.shape[0]
    n_chunks = slots.shape[0]
    workers = SC_CORES * SC_SUBCORES
    assert T == n_chunks * SC_CHUNK and n_chunks % workers == 0
    per_worker = n_chunks // workers

    @functools.partial(
        pl.kernel, mesh=_sc_mesh(),
        out_type=jax.ShapeDtypeStruct((n_rows, HALF), I32),
        scratch_types=[pltpu.VMEM((SC_CHUNK, HALF), I32), pltpu.VMEM((TOP_K, SC_CHUNK), I32),
                       pltpu.SemaphoreType.DMA])
    def dispatch(words_hbm, slots_hbm, out_hbm, rows_v, idx_v, sem):
        first = _sc_worker() * per_worker

        @pl.loop(0, per_worker)
        def _(i):
            c = first + i
            pltpu.sync_copy(slots_hbm.at[c], idx_v)
            pltpu.sync_copy(words_hbm.at[pl.ds(c * SC_CHUNK, SC_CHUNK)], rows_v)
            copies = [pltpu.async_copy(rows_v, out_hbm.at[idx_v.at[k]], sem) for k in range(TOP_K)]
            for cp in copies:
                cp.wait()

    return dispatch(words, slots)


def _sc_combine(rows, slots, gates, T):
    CH = SC_SUM_CHUNK
    steps = SC_CHUNK // CH
    n_blocks = slots.shape[0]
    workers = SC_CORES * SC_SUBCORES
    assert T == n_blocks * SC_CHUNK and n_blocks % workers == 0 and steps >= 2
    per_worker = n_blocks // workers
    vecs = HALF // SC_LANES

    @functools.partial(
        pl.kernel, mesh=_sc_mesh(),
        compiler_params=pltpu.CompilerParams(needs_layout_passes=False),
        out_type=jax.ShapeDtypeStruct((T, HALF), I32),
        scratch_types=[pltpu.VMEM((2, TOP_K, CH, HALF), I32), pltpu.VMEM((2, CH, HALF), I32),
                       pltpu.VMEM((TOP_K, SC_CHUNK), I32), pltpu.VMEM((TOP_K, SC_CHUNK), F32),
                       pltpu.SemaphoreType.DMA((2,)), pltpu.SemaphoreType.DMA((2,))])
    def combine(rows_hbm, slots_hbm, gates_hbm, out_hbm, rows_v, sum_v, idx_v, gates_v, sem_in, sem_out):
        first = _sc_worker() * per_worker

        def gathers(j, b):
            return [pltpu.make_async_copy(rows_hbm.at[idx_v.at[k, pl.ds(j * CH, CH)]], rows_v.at[b, k],
                                          sem_in.at[b]) for k in range(TOP_K)]

        def write_out(token0, b):
            return pltpu.make_async_copy(sum_v.at[b], out_hbm.at[pl.ds(token0, CH)], sem_out.at[b])

        def add_rows(j, b):
            @pl.loop(0, CH)
            def _(t):
                token = jnp.full((SC_LANES,), j * CH + t, I32)
                gate = []
                for k in range(TOP_K):
                    g = plsc.load_gather(gates_v.at[k], [token])
                    gate.append(plsc.pack(g, g, format=plsc.PackFormat.INTERLEAVED))

                @plsc.parallel_loop(0, vecs, 1, unroll=2)
                def _(n):
                    lanes = pl.ds(n * SC_LANES, SC_LANES)
                    terms = [plsc.bitcast(rows_v[b, k, t, lanes], BF16) * gate[k] for k in range(TOP_K)]
                    while len(terms) > 1:
                        terms = [x + y for x, y in zip(terms[::2], terms[1::2])]
                    sum_v[b, t, lanes] = plsc.bitcast(terms[0], I32)

        @pl.loop(0, per_worker)
        def _(i):
            block = first + i
            token0 = block * SC_CHUNK
            pltpu.sync_copy(slots_hbm.at[block], idx_v)
            pltpu.sync_copy(gates_hbm.at[block], gates_v)
            for cp in gathers(0, 0):
                cp.start()
            for j in range(steps):
                b = j % 2
                if j + 1 < steps:
                    for cp in gathers(j + 1, 1 - b):
                        cp.start()
                for cp in gathers(j, b):
                    cp.wait()
                if j >= 2:
                    write_out(token0 + (j - 2) * CH, b).wait()
                add_rows(j, b)
                write_out(token0 + j * CH, b).start()
            write_out(token0 + (steps - 2) * CH, steps % 2).wait()
            write_out(token0 + (steps - 1) * CH, (steps - 1) % 2).wait()

    return combine(rows, slots, gates)


def _expert_kernel(tile_expert_ref, tile_rows_ref, tile_block_ref, xs_ref, wg_ref, wu_ref, wd_ref,
                   ys_ref, wgu_ref, wdn_ref):
    del tile_block_ref
    i = pl.program_id(0)
    e = tile_expert_ref[i]
    n_valid = tile_rows_ref[i]
    e_prev = tile_expert_ref[jnp.maximum(i - 1, 0)]

    @pl.when((i == 0) | (e != e_prev))
    def _():
        wgu_ref[:, :EXPERT_FF] = wg_ref[0].astype(BF16)
        wgu_ref[:, EXPERT_FF:] = wu_ref[0].astype(BF16)
        wdn_ref[...] = wd_ref[0].astype(BF16)

    @pl.when(n_valid > 0)
    def _():
        words = xs_ref[...]
        live = lax.broadcasted_iota(I32, (words.shape[0], 1), 0) < n_valid
        lo, hi = _unpack_rows(jnp.where(live, words, 0))
        a = _dot(lo.astype(BF16), wgu_ref[:HALF, :]) + _dot(hi.astype(BF16), wgu_ref[HALF:, :])
        act = a[:, :EXPERT_FF]
        hid = act * jax.nn.sigmoid(act) * a[:, EXPERT_FF:]
        ys_ref[...] = _pack_rows(_dot(hid.astype(BF16), wdn_ref[...]))


def _experts(xs, tile_expert, tile_rows, tile_block, w_exp_gate, w_exp_up, w_exp_down):
    n_rows = xs.shape[0]
    R = EXPERT_ROWS
    D = D_MODEL
    assert n_rows % R == 0
    return pl.pallas_call(
        _expert_kernel,
        name="experts",
        grid_spec=pltpu.PrefetchScalarGridSpec(
            num_scalar_prefetch=3, grid=(n_rows // R,),
            in_specs=[pl.BlockSpec((R, HALF), lambda i, te, tr, tb: (tb[i], 0)),
                      pl.BlockSpec((1, D, EXPERT_FF), lambda i, te, tr, tb: (te[i], 0, 0)),
                      pl.BlockSpec((1, D, EXPERT_FF), lambda i, te, tr, tb: (te[i], 0, 0)),
                      pl.BlockSpec((1, EXPERT_FF, D), lambda i, te, tr, tb: (te[i], 0, 0))],
            out_specs=pl.BlockSpec((R, HALF), lambda i, te, tr, tb: (tb[i], 0)),
            scratch_shapes=[pltpu.VMEM((D, 2 * EXPERT_FF), BF16), pltpu.VMEM((EXPERT_FF, D), BF16)]),
        out_shape=jax.ShapeDtypeStruct((n_rows, HALF), I32),
        compiler_params=pltpu.CompilerParams(
            dimension_semantics=("arbitrary",), vmem_limit_bytes=VMEM_LIMIT),
    )(tile_expert, tile_rows, tile_block, xs, w_exp_gate, w_exp_up, w_exp_down)


def _final_kernel(h_ref, routed_ref, wsgu_ref, wsd_ref, lng_ref, lnb_ref, *rest, alpha):
    out_ref = rest[-1]
    h = h_ref[...]
    a = _dot(h.astype(BF16), wsgu_ref[...])
    act = a[:, :SHARED_FF]
    hid = act * jax.nn.sigmoid(act) * a[:, SHARED_FF:]
    shared = _dot(hid.astype(BF16), wsd_ref[...])
    lo, hi = _unpack_rows(routed_ref[...])
    ffn = shared + jnp.concatenate([lo, hi], axis=1)
    out_ref[...] = _layer_norm(alpha * h + ffn, lng_ref[...], lnb_ref[...])


def _final(h2, routed, w_sgu, w_sd, ln_g, ln_b, alpha, out_prev, row0):
    T, D = h2.shape
    tm = FINAL_TILE
    rows = routed.shape[0]
    assert rows % tm == 0 and row0 % tm == 0
    tile0 = row0 // tm
    operands = [h2, routed, w_sgu, w_sd, ln_g.reshape(1, D), ln_b.reshape(1, D)]
    in_specs = [pl.BlockSpec((tm, D), lambda i: (i + tile0, 0)),
                pl.BlockSpec((tm, HALF), lambda i: (i, 0)),
                _const_spec((D, 2 * SHARED_FF)), _const_spec((SHARED_FF, D)),
                _const_spec((1, D)), _const_spec((1, D))]
    aliases = {}
    if out_prev is not None:
        aliases = {len(operands): 0}
        operands.append(out_prev)
        in_specs.append(pl.BlockSpec(memory_space=pl.ANY))
    return pl.pallas_call(
        functools.partial(_final_kernel, alpha=alpha),
        name="final",
        grid=(rows // tm,),
        in_specs=in_specs,
        out_specs=pl.BlockSpec((tm, D), lambda i: (i + tile0, 0)),
        out_shape=jax.ShapeDtypeStruct((T, D), F32),
        input_output_aliases=aliases,
        compiler_params=pltpu.CompilerParams(
            dimension_semantics=("arbitrary",), vmem_limit_bytes=VMEM_LIMIT),
    )(*operands)


def _route_and_dispatch(h2, words, w_router, router_bias):
    T, D = h2.shape
    R = EXPERT_ROWS
    eid, pos, gates, counts = _router(h2, w_router, router_bias)
    counts = counts.reshape(N_EXPERTS).astype(I32)
    padded = (counts + (R - 1)) // R * R
    ends = jnp.cumsum(padded)
    offsets = ends - padded
    n_rows = T * TOP_K + N_EXPERTS * R
    tile_start = jnp.arange(n_rows // R, dtype=I32) * R
    tile_block = jnp.minimum(tile_start, jnp.maximum(ends[-1] - R, 0)) // R
    past = ((tile_block * R)[:, None] >= ends[None, :]).astype(I32)
    tile_expert = jnp.minimum(jnp.sum(past, axis=1), N_EXPERTS - 1)
    mine = (tile_expert[:, None] == jnp.arange(N_EXPERTS, dtype=I32)[None, :]).astype(I32)
    valid_end = jnp.sum(mine * (offsets + counts)[None, :], axis=1)
    tile_rows = jnp.clip(valid_end - tile_start, 0, R).astype(I32)
    slots = _slots(offsets.astype(I32), eid, pos)

    def chunked(a):
        return a.reshape(TOP_K, T // SC_CHUNK, SC_CHUNK).transpose(1, 0, 2)

    slots = chunked(slots)
    xs = _sc_dispatch(words, slots, n_rows)
    return xs, slots, chunked(gates), (tile_expert, tile_rows, tile_block.astype(I32))


def kernel(x, w_in, w_gate_up, b_gate, gla_norm_g, w_gla_up, w_pool_grp, pool_scale, w_pool_up,
           w_out, ln1_g, ln1_b, w_router, router_bias, w_exp_gate, w_exp_up, w_exp_down,
           w_sh_gate, w_sh_up, w_sh_down, ln2_g, ln2_b):
    B, S, D = x.shape
    T = B * S
    depth = w_in.shape[0]
    alpha = (2.0 * depth) ** 0.25
    h = x
    for l in range(depth):
        mix_ops = _mixer_operands(w_in[l], w_gate_up[l], b_gate[l], gla_norm_g[l], w_gla_up[l],
                                  w_pool_grp[l], pool_scale[l], w_pool_up[l], w_out[l],
                                  ln1_g[l], ln1_b[l])
        w_sgu = jnp.concatenate([w_sh_gate[l], w_sh_up[l]], axis=-1).astype(BF16)
        w_sd = w_sh_down[l].astype(BF16)
        h, words = _mixer(h, mix_ops, alpha)
        h = h.reshape(T, D)
        xs, slots, gates, tiles = _route_and_dispatch(h, words.reshape(T, HALF),
                                                      w_router[l], router_bias[l])
        ys = _experts(xs, *tiles, w_exp_gate[l], w_exp_up[l], w_exp_down[l])
        blocks = slots.shape[0] // COMBINE_PARTS
        part = T // COMBINE_PARTS
        out = None
        for p in range(COMBINE_PARTS):
            sel = slice(p * blocks, (p + 1) * blocks)
            routed = _sc_combine(ys, slots[sel], gates[sel], part)
            out = _final(h, routed, w_sgu, w_sd, ln2_g[l], ln2_b[l], alpha, out, p * part)
        h = out.reshape(B, S, D)
    return h
```

```python
import functools

import jax
import jax.numpy as jnp
from jax import lax
from jax.experimental import pallas as pl
from jax.experimental.pallas import tpu as pltpu
from jax.experimental.pallas import tpu_sc as plsc

F32 = jnp.float32
BF16 = jnp.bfloat16
I32 = jnp.int32
U32 = jnp.uint32

D_MODEL = 1024
GLA_HEADS = 4
GLA_DK = D_MODEL // 2
GLA_DV = D_MODEL
HEAD_K = GLA_DK // GLA_HEADS
HEAD_V = GLA_DV // GLA_HEADS
GATE_RANK = 16
GATE_TEMP = 16.0
POOL_WIDTH = D_MODEL // 2
POOL_GROUPS = 4
POOL_GROUP_DIM = POOL_WIDTH // POOL_GROUPS
POOL_WINDOWS = (2, 4, 8, 16)
POOL_HALO = 16
N_EXPERTS = 64
TOP_K = 8
N_GROUPS = 8
TOP_GROUPS = 4
EXPERTS_PER_GROUP = N_EXPERTS // N_GROUPS
EXPERT_FF = 256
SHARED_FF = 256
ROUTE_SCALE = 2.5
LN_EPS = 1e-5
RMS_EPS = 1e-6
LANES = 128
SUBLANES = 8

GLA_CHUNK = 256
DECAY_LIMIT = 60.0
MIX_TILE = 512
ROUTER_TILE = 1024
FINAL_TILE = 1024
EXPERT_ROWS = 1024
SC_CORES = 2
SC_SUBCORES = 16
SC_LANES = 16
SC_CHUNK = 128
SC_SUM_CHUNK = 8
HALF = D_MODEL // 2
BF16_BITS = 16
VMEM_LIMIT = 56 * 1024 * 1024

_dot = functools.partial(jnp.dot, preferred_element_type=F32)
_NT = (((1,), (1,)), ((), ()))
_TN = (((0,), (0,)), ((), ()))


def _layer_norm(y, g, b):
    mu = jnp.mean(y, axis=-1, keepdims=True)
    yc = y - mu
    var = jnp.mean(yc * yc, axis=-1, keepdims=True)
    return yc * lax.rsqrt(var + LN_EPS) * g + b


def _split3(a):
    hi = a.astype(BF16)
    r1 = a - hi.astype(F32)
    mid = r1.astype(BF16)
    lo = (r1 - mid.astype(F32)).astype(BF16)
    return hi, mid, lo


def _pack_rows(y):
    lo = lax.bitcast_convert_type(y[:, :HALF].astype(BF16).astype(F32), U32)
    hi = lax.bitcast_convert_type(y[:, HALF:].astype(BF16).astype(F32), U32)
    return lax.bitcast_convert_type(hi | (lo >> BF16_BITS), I32)


def _unpack_rows(w):
    u = lax.bitcast_convert_type(w, U32)
    lo = lax.bitcast_convert_type(u << BF16_BITS, F32)
    hi = lax.bitcast_convert_type(u & jnp.uint32(0xFFFFFFFF << BF16_BITS & 0xFFFFFFFF), F32)
    return lo, hi


def _mixer_kernel(x_ref, wqk_ref, wv_ref, wr_ref, wglr_ref, wgup_ref, bgate_ref, wu_ref,
                  wga_ref, wgb_ref, gnorm_ref, wglaup_ref, wpgrp_ref, pscale_ref, wpup_ref,
                  wout_ref, lng_ref, lnb_ref, out_ref, words_ref, state_ref, ucat_ref,
                  qk_s, v_s, b_s, *, alpha, tm):
    j = pl.program_id(1)

    @pl.when(j == 0)
    def _():
        state_ref[...] = jnp.zeros_like(state_ref)
        ucat_ref[0:POOL_HALO, :] = jnp.zeros((POOL_HALO, POOL_WIDTH), F32)

    C = GLA_CHUNK
    causal = lax.broadcasted_iota(I32, (C, C), 0) >= lax.broadcasted_iota(I32, (C, C), 1)
    tri = causal.astype(BF16)
    gnorm = gnorm_ref[...]
    states = [state_ref[h] for h in range(GLA_HEADS)]

    n_chunks = tm // C

    def finish(rows, x, intra, inter, r, gate_a, gate_b, y_pool):
        heads = []
        for h in range(GLA_HEADS):
            o = intra[h] + inter[h]
            ms = jnp.mean(o * o, axis=-1, keepdims=True)
            heads.append(o * lax.rsqrt(ms + RMS_EPS) * gnorm)
        o = jnp.concatenate(heads, axis=1) * (r * jax.nn.sigmoid(r))
        y_gla = _dot(o.astype(BF16), wglaup_ref[...])
        merged = jax.nn.sigmoid(gate_a) * y_gla + jax.nn.sigmoid(gate_b) * y_pool
        mix = _dot(merged.astype(BF16), wout_ref[...])
        y = _layer_norm(alpha * x + mix, lng_ref[...], lnb_ref[...])
        out_ref[0, rows, :] = y
        words_ref[0, rows, :] = _pack_rows(y)

    kept = []
    for c in range(n_chunks):
        rows = pl.ds(c * C, C)
        x = x_ref[0, rows, :]
        xb = x.astype(BF16)

        qk = _dot(xb, wqk_ref[...])
        v = _dot(xb, wv_ref[...])
        g_lr = _dot(xb, wglr_ref[...])
        z = _dot(g_lr.astype(BF16), wgup_ref[...]) + bgate_ref[...]
        u = _dot(xb, wu_ref[...])
        r = _dot(xb, wr_ref[...])
        log_decay = (jnp.minimum(z, 0.0) - jnp.log1p(jnp.exp(-jnp.abs(z)))) * (1.0 / GATE_TEMP)
        g_hi, g_mid, _ = _split3(log_decay)
        gate_a = _dot(xb, wga_ref[...])
        bcum = _dot(tri, g_hi) + _dot(tri, g_mid)

        ucat_ref[POOL_HALO:, :] = u
        pos = j * tm + c * C - POOL_HALO + lax.broadcasted_iota(I32, (C + POOL_HALO, 1), 0)

        def pool_group(gi):
            window = POOL_WINDOWS[gi]
            a = ucat_ref[:, gi * POOL_GROUP_DIM:(gi + 1) * POOL_GROUP_DIM]
            s = a
            step = 1
            while step < window:
                s = s + pltpu.roll(s, step, 0)
                step *= 2
            count = jnp.clip(pos + 1, 1, window).astype(F32)
            pooled = (s / count - a)[POOL_HALO:, :]
            return _dot(pooled.astype(BF16), wpgrp_ref[gi])

        intra, inter, mixed = [], [], []
        for h in range(GLA_HEADS):
            ks = slice(h * HEAD_K, (h + 1) * HEAD_K)
            b = bcum[:, ks]
            b_last = b[C - 1:C, :]
            q_h = qk[:, ks] * (HEAD_K ** -0.5)
            k_h = qk[:, GLA_DK + h * HEAD_K:GLA_DK + (h + 1) * HEAD_K]
            v_h = v[:, h * HEAD_V:(h + 1) * HEAD_V].astype(BF16)
            q_in = (q_h * jnp.exp(b)).astype(BF16)
            k_out = (k_h * jnp.exp(-b)).astype(BF16)
            k_end = (k_h * jnp.exp(b_last - b)).astype(BF16)
            scores = lax.dot_general(q_in, k_out, _NT, preferred_element_type=F32)
            scores = jnp.where(causal, scores, 0.0).astype(BF16)
            st = states[h]
            intra.append(_dot(scores, v_h))
            inter.append(lax.dot_general(q_in, st.astype(BF16), _NT, preferred_element_type=F32))
            states[h] = st * jnp.exp(b_last) + lax.dot_general(
                v_h, k_end, _TN, preferred_element_type=F32)
            mixed.append(pool_group(h))
            if h == 1:
                gate_b = _dot(xb, wgb_ref[...])
        ucat_ref[0:POOL_HALO, :] = u[C - POOL_HALO:, :]
        mixed = jnp.concatenate(mixed, axis=1) * pscale_ref[...]
        y_pool = _dot(mixed.astype(BF16), wpup_ref[...])
        finish(rows, x, intra, inter, r, gate_a, gate_b, y_pool)
        kept.append((qk, v, bcum, inter, y_pool))

    for h in range(GLA_HEADS):
        state_ref[h] = states[h]

    steepest = kept[0][2][C - 1:C, :]
    for c in range(1, n_chunks):
        steepest = jnp.minimum(steepest, kept[c][2][C - 1:C, :])

    @pl.when(jnp.max(-steepest) > DECAY_LIMIT)
    def _():
        row_id = lax.broadcasted_iota(I32, (C, 1), 0)
        for c in range(n_chunks):
            rows = pl.ds(c * C, C)
            qk, v, bcum, inter, y_pool = kept[c]
            qk_s[...] = qk
            v_s[...] = v
            b_s[...] = bcum
            intra = []
            for h in range(GLA_HEADS):
                ks = slice(h * HEAD_K, (h + 1) * HEAD_K)
                kks = slice(GLA_DK + h * HEAD_K, GLA_DK + (h + 1) * HEAD_K)
                vs = slice(h * HEAD_V, (h + 1) * HEAD_V)
                q_h = qk[:, ks] * (HEAD_K ** -0.5)
                b_h = bcum[:, ks]

                def add_rows(group, acc):
                    first = pl.multiple_of(group * SUBLANES, SUBLANES)
                    b_g = b_s[pl.ds(first, SUBLANES), ks]
                    k_g = qk_s[pl.ds(first, SUBLANES), kks]
                    v_g = v_s[pl.ds(first, SUBLANES), vs]
                    for i in range(SUBLANES):
                        decay = jnp.exp(jnp.minimum(b_h - b_g[i:i + 1, :], 0.0))
                        decay = jnp.where(row_id >= first + i, decay, 0.0)
                        score = jnp.sum(q_h * k_g[i:i + 1, :] * decay, axis=1, keepdims=True)
                        acc = acc + score * v_g[i:i + 1, :]
                    return acc

                intra.append(lax.fori_loop(0, C // SUBLANES, add_rows, jnp.zeros((C, HEAD_V), F32)))
            x = x_ref[0, rows, :]
            xb = x.astype(BF16)
            finish(rows, x, intra, inter, _dot(xb, wr_ref[...]), _dot(xb, wga_ref[...]),
                   _dot(xb, wgb_ref[...]), y_pool)


def _const_spec(shape):
    zeros = (0,) * len(shape)
    return pl.BlockSpec(shape, lambda *_: zeros, pipeline_mode=pl.Buffered(1))


def _mixer_operands(w_in, w_gate_up, b_gate, gla_norm_g, w_gla_up, w_pool_grp, pool_scale,
                    w_pool_up, w_out, ln_g, ln_b):
    D = D_MODEL
    c = 0
    w_qk = w_in[:, c:c + 2 * GLA_DK].astype(BF16); c += 2 * GLA_DK
    w_v = w_in[:, c:c + GLA_DV].astype(BF16); c += GLA_DV
    w_r = w_in[:, c:c + GLA_DV].astype(BF16); c += GLA_DV
    w_glr = jnp.pad(w_in[:, c:c + GATE_RANK], ((0, 0), (0, LANES - GATE_RANK))).astype(BF16); c += GATE_RANK
    w_u = w_in[:, c:c + POOL_WIDTH].astype(BF16); c += POOL_WIDTH
    w_ga = w_in[:, c:c + D].astype(BF16); c += D
    w_gb = w_in[:, c:c + D].astype(BF16); c += D
    assert c == w_in.shape[1]
    w_gup = jnp.pad(w_gate_up, ((0, LANES - GATE_RANK), (0, 0))).astype(BF16)
    return (
        w_qk, w_v, w_r, w_glr, w_gup, b_gate.reshape(1, GLA_DK), w_u, w_ga, w_gb,
        gla_norm_g.reshape(1, HEAD_V), w_gla_up.astype(BF16), w_pool_grp.astype(BF16),
        pool_scale.reshape(1, POOL_WIDTH), w_pool_up.astype(BF16), w_out.astype(BF16),
        ln_g.reshape(1, D), ln_b.reshape(1, D))


def _mixer(x, operands, alpha):
    B, S, D = x.shape
    tm = MIX_TILE
    assert D == D_MODEL and S % tm == 0 and tm % GLA_CHUNK == 0
    return pl.pallas_call(
        functools.partial(_mixer_kernel, alpha=alpha, tm=tm),
        name="mixer",
        grid=(B, S // tm),
        in_specs=[pl.BlockSpec((1, tm, D), lambda b, j: (b, j, 0))]
        + [_const_spec(w.shape) for w in operands],
        out_specs=[pl.BlockSpec((1, tm, D), lambda b, j: (b, j, 0)),
                   pl.BlockSpec((1, tm, HALF), lambda b, j: (b, j, 0))],
        out_shape=[jax.ShapeDtypeStruct((B, S, D), F32),
                   jax.ShapeDtypeStruct((B, S, HALF), I32)],
        scratch_shapes=[
            pltpu.VMEM((GLA_HEADS, HEAD_V, HEAD_K), F32),
            pltpu.VMEM((GLA_CHUNK + POOL_HALO, POOL_WIDTH), F32),
            pltpu.VMEM((GLA_CHUNK, 2 * GLA_DK), F32),
            pltpu.VMEM((GLA_CHUNK, GLA_DV), F32),
            pltpu.VMEM((GLA_CHUNK, GLA_DK), F32),
        ],
        compiler_params=pltpu.CompilerParams(
            dimension_semantics=("arbitrary", "arbitrary"), vmem_limit_bytes=VMEM_LIMIT),
    )(x, *operands)


def _rank_desc(vals, n):
    idx = lax.broadcasted_iota(I32, vals.shape, 0)
    rank = jnp.zeros(vals.shape, I32)
    for other in range(n):
        o = vals[other:other + 1, :]
        before = (o > vals) | ((o == vals) & (other < idx))
        rank = rank + before.astype(I32)
    return rank


def _router_kernel(h_ref, wrt_ref, bias_ref, eid_ref, pos_ref, wk_ref, cnt_ref, carry_ref):
    i = pl.program_id(0)

    @pl.when(i == 0)
    def _():
        carry_ref[...] = jnp.zeros_like(carry_ref)

    h = h_ref[...]
    tr = h.shape[0]
    w_hi, w_mid, _ = _split3(wrt_ref[...])
    h_hi, h_mid, _ = _split3(h)
    by_hi = lax.dot_general(jnp.concatenate([w_hi, w_mid], axis=0), h_hi, _NT,
                            preferred_element_type=F32)
    logits = (by_hi[:N_EXPERTS] + by_hi[N_EXPERTS:]
              + lax.dot_general(w_hi, h_mid, _NT, preferred_element_type=F32))
    scores = jax.nn.sigmoid(logits)
    biased = scores + bias_ref[...]
    grp = biased.reshape(N_GROUPS, EXPERTS_PER_GROUP, tr)
    slot = lax.broadcasted_iota(I32, grp.shape, 1)
    top1 = jnp.max(grp, axis=1, keepdims=True)
    first = jnp.min(jnp.where(grp == top1, slot, EXPERTS_PER_GROUP), axis=1, keepdims=True)
    top2 = jnp.max(jnp.where(slot == first, -jnp.inf, grp), axis=1)
    grp_score = top1[:, 0, :] + top2
    grp_on = _rank_desc(grp_score, N_GROUPS) < TOP_GROUPS
    emask = jnp.broadcast_to(grp_on[:, None, :], grp.shape).reshape(N_EXPERTS, tr)
    masked = jnp.where(emask, biased, -jnp.inf)

    eidx = lax.broadcasted_iota(I32, (N_EXPERTS, tr), 0).astype(F32)
    rest = masked
    eids, sels = [], []
    for _ in range(TOP_K):
        top = jnp.max(rest, axis=0, keepdims=True)
        eid = jnp.min(jnp.where(rest == top, eidx, float(N_EXPERTS)), axis=0, keepdims=True)
        hit = eidx == eid
        rest = jnp.where(hit, -jnp.inf, rest)
        eids.append(eid)
        sels.append(jnp.sum(jnp.where(hit, scores, 0.0), axis=0, keepdims=True))
    eid_k = jnp.concatenate(eids, axis=0)
    sel_k = jnp.concatenate(sels, axis=0)
    gate_k = sel_k / jnp.sum(sel_k, axis=0, keepdims=True) * ROUTE_SCALE
    chosen = rest != masked

    t_row = lax.broadcasted_iota(I32, (tr, tr), 0)
    t_col = lax.broadcasted_iota(I32, (tr, tr), 1)
    earlier = (t_row < t_col).astype(BF16)
    pos = carry_ref[...] + _dot(chosen.astype(BF16), earlier)
    carry_ref[...] += jnp.sum(chosen.astype(F32), axis=1, keepdims=True)
    cnt_ref[...] = carry_ref[...]

    poss = [jnp.sum(jnp.where(eidx == eid, pos, 0.0), axis=0, keepdims=True) for eid in eids]
    eid_ref[...] = eid_k.astype(I32)
    pos_ref[...] = jnp.concatenate(poss, axis=0).astype(I32)
    wk_ref[...] = gate_k


def _router(h2, w_router, router_bias):
    T, D = h2.shape
    tr = ROUTER_TILE
    assert T % tr == 0
    return pl.pallas_call(
        _router_kernel,
        name="router",
        grid=(T // tr,),
        in_specs=[pl.BlockSpec((tr, D), lambda i: (i, 0)),
                  _const_spec((N_EXPERTS, D)), _const_spec((N_EXPERTS, 1))],
        out_specs=[pl.BlockSpec((TOP_K, tr), lambda i: (0, i)),
                   pl.BlockSpec((TOP_K, tr), lambda i: (0, i)),
                   pl.BlockSpec((TOP_K, tr), lambda i: (0, i)),
                   pl.BlockSpec((N_EXPERTS, 1), lambda i: (0, 0))],
        out_shape=[jax.ShapeDtypeStruct((TOP_K, T), I32),
                   jax.ShapeDtypeStruct((TOP_K, T), I32),
                   jax.ShapeDtypeStruct((TOP_K, T), F32),
                   jax.ShapeDtypeStruct((N_EXPERTS, 1), F32)],
        scratch_shapes=[pltpu.VMEM((N_EXPERTS, 1), F32)],
        compiler_params=pltpu.CompilerParams(
            dimension_semantics=("arbitrary",), vmem_limit_bytes=VMEM_LIMIT),
    )(h2, w_router.T, router_bias.reshape(N_EXPERTS, 1))


def _slot_kernel(offs_ref, eid_ref, pos_ref, slot_ref):
    eid = eid_ref[...]
    base = jnp.zeros(eid.shape, I32)
    for e in range(N_EXPERTS):
        base = jnp.where(eid == e, offs_ref[e], base)
    slot_ref[...] = pos_ref[...] + base


def _slots(offsets, eid, pos):
    shape = eid.shape
    return pl.pallas_call(
        _slot_kernel,
        name="slots",
        grid_spec=pltpu.PrefetchScalarGridSpec(
            num_scalar_prefetch=1, grid=(1,),
            in_specs=[pl.BlockSpec(shape, lambda i, offs: (0, 0)),
                      pl.BlockSpec(shape, lambda i, offs: (0, 0))],
            out_specs=pl.BlockSpec(shape, lambda i, offs: (0, 0))),
        out_shape=jax.ShapeDtypeStruct(shape, I32),
        compiler_params=pltpu.CompilerParams(vmem_limit_bytes=VMEM_LIMIT),
    )(offsets, eid, pos)


def _sc_mesh():
    return plsc.VectorSubcoreMesh(core_axis_name="core", subcore_axis_name="subcore",
                                  num_cores=SC_CORES, num_subcores=SC_SUBCORES)


def _sc_worker():
    return lax.axis_index("subcore") * SC_CORES + lax.axis_index("core")


def _sc_dispatch(words, slots, n_rows):
    T = words.shape[0]
    n_chunks = slots.shape[0]
    workers = SC_CORES * SC_SUBCORES
    assert T == n_chunks * SC_CHUNK and n_chunks % workers == 0
    per_worker = n_chunks // workers

    @functools.partial(
        pl.kernel, mesh=_sc_mesh(),
        out_type=jax.ShapeDtypeStruct((n_rows, HALF), I32),
        scratch_types=[pltpu.VMEM((SC_CHUNK, HALF), I32), pltpu.VMEM((TOP_K, SC_CHUNK), I32),
                       pltpu.SemaphoreType.DMA])
    def dispatch(words_hbm, slots_hbm, out_hbm, rows_v, idx_v, sem):
        first = _sc_worker() * per_worker

        @pl.loop(0, per_worker)
        def _(i):
            c = first + i
            pltpu.sync_copy(slots_hbm.at[c], idx_v)
            pltpu.sync_copy(words_hbm.at[pl.ds(c * SC_CHUNK, SC_CHUNK)], rows_v)
            copies = [pltpu.async_copy(rows_v, out_hbm.at[idx_v.at[k]], sem) for k in range(TOP_K)]
            for cp in copies:
                cp.wait()

    return dispatch(words, slots)


def _sc_combine(rows, slots, gates, T):
    CH = SC_SUM_CHUNK
    steps = SC_CHUNK // CH
    n_blocks = slots.shape[0]
    workers = SC_CORES * SC_SUBCORES
    assert T == n_blocks * SC_CHUNK and n_blocks % workers == 0 and steps >= 2
    per_worker = n_blocks // workers
    vecs = HALF // SC_LANES

    @functools.partial(
        pl.kernel, mesh=_sc_mesh(),
        compiler_params=pltpu.CompilerParams(needs_layout_passes=False),
        out_type=jax.ShapeDtypeStruct((T, HALF), I32),
        scratch_types=[pltpu.VMEM((2, TOP_K, CH, HALF), I32), pltpu.VMEM((2, CH, HALF), I32),
                       pltpu.VMEM((per_worker, TOP_K, SC_CHUNK), I32),
                       pltpu.VMEM((per_worker, TOP_K, SC_CHUNK), F32),
                       pltpu.SemaphoreType.DMA((2,)), pltpu.SemaphoreType.DMA((2,))])
    def combine(rows_hbm, slots_hbm, gates_hbm, out_hbm, rows_v, sum_v, idx_v, gates_v, sem_in, sem_out):
        first = _sc_worker() * per_worker
        pltpu.sync_copy(slots_hbm.at[pl.ds(first, per_worker)], idx_v)
        pltpu.sync_copy(gates_hbm.at[pl.ds(first, per_worker)], gates_v)

        def gathers(i, j, b):
            return [pltpu.make_async_copy(rows_hbm.at[idx_v.at[i, k, pl.ds(j * CH, CH)]], rows_v.at[b, k],
                                          sem_in.at[b]) for k in range(TOP_K)]

        def write_out(token0, b):
            return pltpu.make_async_copy(sum_v.at[b], out_hbm.at[pl.ds(token0, CH)], sem_out.at[b])

        def add_rows(i, j, b):
            @pl.loop(0, CH)
            def _(t):
                token = jnp.full((SC_LANES,), j * CH + t, I32)
                gate = []
                for k in range(TOP_K):
                    g = plsc.load_gather(gates_v.at[i, k], [token])
                    gate.append(plsc.pack(g, g, format=plsc.PackFormat.INTERLEAVED))

                @plsc.parallel_loop(0, vecs, 1, unroll=2)
                def _(n):
                    lanes = pl.ds(n * SC_LANES, SC_LANES)
                    terms = [plsc.bitcast(rows_v[b, k, t, lanes], BF16) * gate[k] for k in range(TOP_K)]
                    while len(terms) > 1:
                        terms = [x + y for x, y in zip(terms[::2], terms[1::2])]
                    sum_v[b, t, lanes] = plsc.bitcast(terms[0], I32)

        @pl.loop(0, per_worker)
        def _(i):
            token0 = (first + i) * SC_CHUNK
            for cp in gathers(i, 0, 0):
                cp.start()
            for j in range(steps):
                b = j % 2
                if j + 1 < steps:
                    for cp in gathers(i, j + 1, 1 - b):
                        cp.start()
                for cp in gathers(i, j, b):
                    cp.wait()
                if j >= 2:
                    write_out(token0 + (j - 2) * CH, b).wait()
                add_rows(i, j, b)
                write_out(token0 + j * CH, b).start()
            write_out(token0 + (steps - 2) * CH, steps % 2).wait()
            write_out(token0 + (steps - 1) * CH, (steps - 1) % 2).wait()

    return combine(rows, slots, gates)


def _expert_kernel(tile_expert_ref, tile_rows_ref, tile_block_ref, xs_ref, wg_ref, wu_ref, wd_ref,
                   ys_ref, wgu_ref, wdn_ref):
    del tile_block_ref
    i = pl.program_id(0)
    e = tile_expert_ref[i]
    n_valid = tile_rows_ref[i]
    e_prev = tile_expert_ref[jnp.maximum(i - 1, 0)]

    @pl.when((i == 0) | (e != e_prev))
    def _():
        wgu_ref[:, :EXPERT_FF] = wg_ref[0].astype(BF16)
        wgu_ref[:, EXPERT_FF:] = wu_ref[0].astype(BF16)
        wdn_ref[...] = wd_ref[0].astype(BF16)

    @pl.when(n_valid > 0)
    def _():
        words = xs_ref[...]
        live = lax.broadcasted_iota(I32, (words.shape[0], 1), 0) < n_valid
        lo, hi = _unpack_rows(jnp.where(live, words, 0))
        a = _dot(lo.astype(BF16), wgu_ref[:HALF, :]) + _dot(hi.astype(BF16), wgu_ref[HALF:, :])
        act = a[:, :EXPERT_FF]
        hid = act * jax.nn.sigmoid(act) * a[:, EXPERT_FF:]
        ys_ref[...] = _pack_rows(_dot(hid.astype(BF16), wdn_ref[...]))


def _experts(xs, tile_expert, tile_rows, tile_block, w_exp_gate, w_exp_up, w_exp_down):
    n_rows = xs.shape[0]
    R = EXPERT_ROWS
    D = D_MODEL
    assert n_rows % R == 0
    return pl.pallas_call(
        _expert_kernel,
        name="experts",
        grid_spec=pltpu.PrefetchScalarGridSpec(
            num_scalar_prefetch=3, grid=(n_rows // R,),
            in_specs=[pl.BlockSpec((R, HALF), lambda i, te, tr, tb: (tb[i], 0)),
                      pl.BlockSpec((1, D, EXPERT_FF), lambda i, te, tr, tb: (te[i], 0, 0)),
                      pl.BlockSpec((1, D, EXPERT_FF), lambda i, te, tr, tb: (te[i], 0, 0)),
                      pl.BlockSpec((1, EXPERT_FF, D), lambda i, te, tr, tb: (te[i], 0, 0))],
            out_specs=pl.BlockSpec((R, HALF), lambda i, te, tr, tb: (tb[i], 0)),
            scratch_shapes=[pltpu.VMEM((D, 2 * EXPERT_FF), BF16), pltpu.VMEM((EXPERT_FF, D), BF16)]),
        out_shape=jax.ShapeDtypeStruct((n_rows, HALF), I32),
        compiler_params=pltpu.CompilerParams(
            dimension_semantics=("arbitrary",), vmem_limit_bytes=VMEM_LIMIT),
    )(tile_expert, tile_rows, tile_block, xs, w_exp_gate, w_exp_up, w_exp_down)


def _final_kernel(h_ref, routed_ref, wsgu_ref, wsd_ref, lng_ref, lnb_ref, out_ref, *, alpha):
    h = h_ref[...]
    a = _dot(h.astype(BF16), wsgu_ref[...])
    act = a[:, :SHARED_FF]
    hid = act * jax.nn.sigmoid(act) * a[:, SHARED_FF:]
    shared = _dot(hid.astype(BF16), wsd_ref[...])
    lo, hi = _unpack_rows(routed_ref[...])
    ffn = shared + jnp.concatenate([lo, hi], axis=1)
    out_ref[...] = _layer_norm(alpha * h + ffn, lng_ref[...], lnb_ref[...])


def _final(h2, routed, w_sgu, w_sd, ln_g, ln_b, alpha):
    T, D = h2.shape
    tm = FINAL_TILE
    assert T % tm == 0
    return pl.pallas_call(
        functools.partial(_final_kernel, alpha=alpha),
        name="final",
        grid=(T // tm,),
        in_specs=[pl.BlockSpec((tm, D), lambda i: (i, 0)),
                  pl.BlockSpec((tm, HALF), lambda i: (i, 0)),
                  _const_spec((D, 2 * SHARED_FF)), _const_spec((SHARED_FF, D)),
                  _const_spec((1, D)), _const_spec((1, D))],
        out_specs=pl.BlockSpec((tm, D), lambda i: (i, 0)),
        out_shape=jax.ShapeDtypeStruct((T, D), F32),
        compiler_params=pltpu.CompilerParams(
            dimension_semantics=("arbitrary",), vmem_limit_bytes=VMEM_LIMIT),
    )(h2, routed, w_sgu, w_sd, ln_g.reshape(1, D), ln_b.reshape(1, D))


def _route_and_dispatch(h2, words, w_router, router_bias):
    T, D = h2.shape
    R = EXPERT_ROWS
    eid, pos, gates, counts = _router(h2, w_router, router_bias)
    counts = counts.reshape(N_EXPERTS).astype(I32)
    padded = (counts + (R - 1)) // R * R
    ends = jnp.cumsum(padded)
    offsets = ends - padded
    n_rows = T * TOP_K + N_EXPERTS * R
    tile_start = jnp.arange(n_rows // R, dtype=I32) * R
    tile_block = jnp.minimum(tile_start, jnp.maximum(ends[-1] - R, 0)) // R
    past = ((tile_block * R)[:, None] >= ends[None, :]).astype(I32)
    tile_expert = jnp.minimum(jnp.sum(past, axis=1), N_EXPERTS - 1)
    mine = (tile_expert[:, None] == jnp.arange(N_EXPERTS, dtype=I32)[None, :]).astype(I32)
    valid_end = jnp.sum(mine * (offsets + counts)[None, :], axis=1)
    tile_rows = jnp.clip(valid_end - tile_start, 0, R).astype(I32)
    slots = _slots(offsets.astype(I32), eid, pos)

    def chunked(a):
        return a.reshape(TOP_K, T // SC_CHUNK, SC_CHUNK).transpose(1, 0, 2)

    slots = chunked(slots)
    xs = _sc_dispatch(words, slots, n_rows)
    return xs, slots, chunked(gates), (tile_expert, tile_rows, tile_block.astype(I32))


def kernel(x, w_in, w_gate_up, b_gate, gla_norm_g, w_gla_up, w_pool_grp, pool_scale, w_pool_up,
           w_out, ln1_g, ln1_b, w_router, router_bias, w_exp_gate, w_exp_up, w_exp_down,
           w_sh_gate, w_sh_up, w_sh_down, ln2_g, ln2_b):
    B, S, D = x.shape
    T = B * S
    depth = w_in.shape[0]
    alpha = (2.0 * depth) ** 0.25
    h = x
    for l in range(depth):
        mix_ops = _mixer_operands(w_in[l], w_gate_up[l], b_gate[l], gla_norm_g[l], w_gla_up[l],
                                  w_pool_grp[l], pool_scale[l], w_pool_up[l], w_out[l],
                                  ln1_g[l], ln1_b[l])
        w_sgu = jnp.concatenate([w_sh_gate[l], w_sh_up[l]], axis=-1).astype(BF16)
        w_sd = w_sh_down[l].astype(BF16)
        h, words = _mixer(h, mix_ops, alpha)
        h = h.reshape(T, D)
        xs, slots, gates, tiles = _route_and_dispatch(h, words.reshape(T, HALF),
                                                      w_router[l], router_bias[l])
        ys = _experts(xs, *tiles, w_exp_gate[l], w_exp_up[l], w_exp_down[l])
        routed = _sc_combine(ys, slots, gates, T)
        h = _final(h, routed, w_sgu, w_sd, ln2_g[l], ln2_b[l], alpha).reshape(B, S, D)
    return h
```

```python
import functools

import jax
import jax.numpy as jnp
from jax import lax
from jax.experimental import pallas as pl
from jax.experimental.pallas import tpu as pltpu
from jax.experimental.pallas import tpu_sc as plsc

F32 = jnp.float32
BF16 = jnp.bfloat16
I32 = jnp.int32
U32 = jnp.uint32

D_MODEL = 1024
GLA_HEADS = 4
GLA_DK = D_MODEL // 2
GLA_DV = D_MODEL
HEAD_K = GLA_DK // GLA_HEADS
HEAD_V = GLA_DV // GLA_HEADS
GATE_RANK = 16
GATE_TEMP = 16.0
POOL_WIDTH = D_MODEL // 2
POOL_GROUPS = 4
POOL_GROUP_DIM = POOL_WIDTH // POOL_GROUPS
POOL_WINDOWS = (2, 4, 8, 16)
POOL_HALO = 16
N_EXPERTS = 64
TOP_K = 8
N_GROUPS = 8
TOP_GROUPS = 4
EXPERTS_PER_GROUP = N_EXPERTS // N_GROUPS
EXPERT_FF = 256
SHARED_FF = 256
ROUTE_SCALE = 2.5
LN_EPS = 1e-5
RMS_EPS = 1e-6
LANES = 128
SUBLANES = 8

GLA_CHUNK = 256
DECAY_LIMIT = 60.0
MIX_TILE = 512
ROUTER_TILE = 1024
FINAL_TILE = 1024
EXPERT_ROWS = 1024
SC_CORES = 2
SC_SUBCORES = 16
SC_LANES = 16
SC_CHUNK = 128
SC_SUM_CHUNK = 8
HALF = D_MODEL // 2
BF16_BITS = 16
VMEM_LIMIT = 56 * 1024 * 1024

_dot = functools.partial(jnp.dot, preferred_element_type=F32)
_NT = (((1,), (1,)), ((), ()))
_TN = (((0,), (0,)), ((), ()))


def _layer_norm(y, g, b):
    mu = jnp.mean(y, axis=-1, keepdims=True)
    yc = y - mu
    var = jnp.mean(yc * yc, axis=-1, keepdims=True)
    return yc * lax.rsqrt(var + LN_EPS) * g + b


def _split3(a):
    hi = a.astype(BF16)
    r1 = a - hi.astype(F32)
    mid = r1.astype(BF16)
    lo = (r1 - mid.astype(F32)).astype(BF16)
    return hi, mid, lo


def _pack_rows(y):
    lo = lax.bitcast_convert_type(y[:, :HALF].astype(BF16).astype(F32), U32)
    hi = lax.bitcast_convert_type(y[:, HALF:].astype(BF16).astype(F32), U32)
    return lax.bitcast_convert_type(hi | (lo >> BF16_BITS), I32)


def _unpack_rows(w):
    u = lax.bitcast_convert_type(w, U32)
    lo = lax.bitcast_convert_type(u << BF16_BITS, F32)
    hi = lax.bitcast_convert_type(u & jnp.uint32(0xFFFFFFFF << BF16_BITS & 0xFFFFFFFF), F32)
    return lo, hi


def _mixer_kernel(x_ref, wqk_ref, wv_ref, wr_ref, wglr_ref, wgup_ref, bgate_ref, wu_ref,
                  wga_ref, wgb_ref, gnorm_ref, wglaup_ref, wpgrp_ref, pscale_ref, wpup_ref,
                  wout_ref, lng_ref, lnb_ref, out_ref, words_ref, state_ref, ucat_ref,
                  qk_s, v_s, b_s, *, alpha, tm):
    j = pl.program_id(1)

    @pl.when(j == 0)
    def _():
        state_ref[...] = jnp.zeros_like(state_ref)
        ucat_ref[0:POOL_HALO, :] = jnp.zeros((POOL_HALO, POOL_WIDTH), F32)

    C = GLA_CHUNK
    causal = lax.broadcasted_iota(I32, (C, C), 0) >= lax.broadcasted_iota(I32, (C, C), 1)
    tri = causal.astype(BF16)
    gnorm = gnorm_ref[...]
    states = [state_ref[h] for h in range(GLA_HEADS)]

    n_chunks = tm // C

    def finish(rows, x, intra, inter, r, gate_a, gate_b, y_pool):
        heads = []
        for h in range(GLA_HEADS):
            o = intra[h] + inter[h]
            ms = jnp.mean(o * o, axis=-1, keepdims=True)
            heads.append(o * lax.rsqrt(ms + RMS_EPS) * gnorm)
        o = jnp.concatenate(heads, axis=1) * (r * jax.nn.sigmoid(r))
        y_gla = _dot(o.astype(BF16), wglaup_ref[...])
        merged = jax.nn.sigmoid(gate_a) * y_gla + jax.nn.sigmoid(gate_b) * y_pool
        mix = _dot(merged.astype(BF16), wout_ref[...])
        y = _layer_norm(alpha * x + mix, lng_ref[...], lnb_ref[...])
        out_ref[0, rows, :] = y
        words_ref[0, rows, :] = _pack_rows(y)

    kept = []
    for c in range(n_chunks):
        rows = pl.ds(c * C, C)
        x = x_ref[0, rows, :]
        xb = x.astype(BF16)

        qk = _dot(xb, wqk_ref[...])
        v = _dot(xb, wv_ref[...])
        g_lr = _dot(xb, wglr_ref[...])
        z = _dot(g_lr.astype(BF16), wgup_ref[...]) + bgate_ref[...]
        u = _dot(xb, wu_ref[...])
        r = _dot(xb, wr_ref[...])
        log_decay = (jnp.minimum(z, 0.0) - jnp.log1p(jnp.exp(-jnp.abs(z)))) * (1.0 / GATE_TEMP)
        g_hi, g_mid, _ = _split3(log_decay)
        gate_a = _dot(xb, wga_ref[...])
        bcum = _dot(tri, g_hi) + _dot(tri, g_mid)

        ucat_ref[POOL_HALO:, :] = u
        pos = j * tm + c * C - POOL_HALO + lax.broadcasted_iota(I32, (C + POOL_HALO, 1), 0)

        def pool_group(gi):
            window = POOL_WINDOWS[gi]
            a = ucat_ref[:, gi * POOL_GROUP_DIM:(gi + 1) * POOL_GROUP_DIM]
            s = a
            step = 1
            while step < window:
                s = s + pltpu.roll(s, step, 0)
                step *= 2
            count = jnp.clip(pos + 1, 1, window).astype(F32)
            pooled = (s / count - a)[POOL_HALO:, :]
            return _dot(pooled.astype(BF16), wpgrp_ref[gi])

        intra, inter, mixed = [], [], []
        for h in range(GLA_HEADS):
            ks = slice(h * HEAD_K, (h + 1) * HEAD_K)
            b = bcum[:, ks]
            b_last = b[C - 1:C, :]
            q_h = qk[:, ks] * (HEAD_K ** -0.5)
            k_h = qk[:, GLA_DK + h * HEAD_K:GLA_DK + (h + 1) * HEAD_K]
            v_h = v[:, h * HEAD_V:(h + 1) * HEAD_V].astype(BF16)
            q_in = (q_h * jnp.exp(b)).astype(BF16)
            k_out = (k_h * jnp.exp(-b)).astype(BF16)
            k_end = (k_h * jnp.exp(b_last - b)).astype(BF16)
            scores = lax.dot_general(q_in, k_out, _NT, preferred_element_type=F32)
            scores = jnp.where(causal, scores, 0.0).astype(BF16)
            st = states[h]
            intra.append(_dot(scores, v_h))
            inter.append(lax.dot_general(q_in, st.astype(BF16), _NT, preferred_element_type=F32))
            states[h] = st * jnp.exp(b_last) + lax.dot_general(
                v_h, k_end, _TN, preferred_element_type=F32)
            mixed.append(pool_group(h))
            if h == 1:
                gate_b = _dot(xb, wgb_ref[...])
        ucat_ref[0:POOL_HALO, :] = u[C - POOL_HALO:, :]
        mixed = jnp.concatenate(mixed, axis=1) * pscale_ref[...]
        y_pool = _dot(mixed.astype(BF16), wpup_ref[...])
        finish(rows, x, intra, inter, r, gate_a, gate_b, y_pool)
        kept.append((qk, v, bcum, inter, y_pool))

    for h in range(GLA_HEADS):
        state_ref[h] = states[h]

    steepest = kept[0][2][C - 1:C, :]
    for c in range(1, n_chunks):
        steepest = jnp.minimum(steepest, kept[c][2][C - 1:C, :])

    @pl.when(jnp.max(-steepest) > DECAY_LIMIT)
    def _():
        row_id = lax.broadcasted_iota(I32, (C, 1), 0)
        for c in range(n_chunks):
            rows = pl.ds(c * C, C)
            qk, v, bcum, inter, y_pool = kept[c]
            qk_s[...] = qk
            v_s[...] = v
            b_s[...] = bcum
            intra = []
            for h in range(GLA_HEADS):
                ks = slice(h * HEAD_K, (h + 1) * HEAD_K)
                kks = slice(GLA_DK + h * HEAD_K, GLA_DK + (h + 1) * HEAD_K)
                vs = slice(h * HEAD_V, (h + 1) * HEAD_V)
                q_h = qk[:, ks] * (HEAD_K ** -0.5)
                b_h = bcum[:, ks]

                def add_rows(group, acc):
                    first = pl.multiple_of(group * SUBLANES, SUBLANES)
                    b_g = b_s[pl.ds(first, SUBLANES), ks]
                    k_g = qk_s[pl.ds(first, SUBLANES), kks]
                    v_g = v_s[pl.ds(first, SUBLANES), vs]
                    for i in range(SUBLANES):
                        decay = jnp.exp(jnp.minimum(b_h - b_g[i:i + 1, :], 0.0))
                        decay = jnp.where(row_id >= first + i, decay, 0.0)
                        score = jnp.sum(q_h * k_g[i:i + 1, :] * decay, axis=1, keepdims=True)
                        acc = acc + score * v_g[i:i + 1, :]
                    return acc

                intra.append(lax.fori_loop(0, C // SUBLANES, add_rows, jnp.zeros((C, HEAD_V), F32)))
            x = x_ref[0, rows, :]
            xb = x.astype(BF16)
            finish(rows, x, intra, inter, _dot(xb, wr_ref[...]), _dot(xb, wga_ref[...]),
                   _dot(xb, wgb_ref[...]), y_pool)


def _const_spec(shape):
    zeros = (0,) * len(shape)
    return pl.BlockSpec(shape, lambda *_: zeros, pipeline_mode=pl.Buffered(1))


def _mixer_operands(w_in, w_gate_up, b_gate, gla_norm_g, w_gla_up, w_pool_grp, pool_scale,
                    w_pool_up, w_out, ln_g, ln_b):
    D = D_MODEL
    c = 0
    w_qk = w_in[:, c:c + 2 * GLA_DK].astype(BF16); c += 2 * GLA_DK
    w_v = w_in[:, c:c + GLA_DV].astype(BF16); c += GLA_DV
    w_r = w_in[:, c:c + GLA_DV].astype(BF16); c += GLA_DV
    w_glr = jnp.pad(w_in[:, c:c + GATE_RANK], ((0, 0), (0, LANES - GATE_RANK))).astype(BF16); c += GATE_RANK
    w_u = w_in[:, c:c + POOL_WIDTH].astype(BF16); c += POOL_WIDTH
    w_ga = w_in[:, c:c + D].astype(BF16); c += D
    w_gb = w_in[:, c:c + D].astype(BF16); c += D
    assert c == w_in.shape[1]
    w_gup = jnp.pad(w_gate_up, ((0, LANES - GATE_RANK), (0, 0))).astype(BF16)
    return (
        w_qk, w_v, w_r, w_glr, w_gup, b_gate.reshape(1, GLA_DK), w_u, w_ga, w_gb,
        gla_norm_g.reshape(1, HEAD_V), w_gla_up.astype(BF16), w_pool_grp.astype(BF16),
        pool_scale.reshape(1, POOL_WIDTH), w_pool_up.astype(BF16), w_out.astype(BF16),
        ln_g.reshape(1, D), ln_b.reshape(1, D))


def _mixer(x, operands, alpha):
    B, S, D = x.shape
    tm = MIX_TILE
    assert D == D_MODEL and S % tm == 0 and tm % GLA_CHUNK == 0
    return pl.pallas_call(
        functools.partial(_mixer_kernel, alpha=alpha, tm=tm),
        name="mixer",
        grid=(B, S // tm),
        in_specs=[pl.BlockSpec((1, tm, D), lambda b, j: (b, j, 0))]
        + [_const_spec(w.shape) for w in operands],
        out_specs=[pl.BlockSpec((1, tm, D), lambda b, j: (b, j, 0)),
                   pl.BlockSpec((1, tm, HALF), lambda b, j: (b, j, 0))],
        out_shape=[jax.ShapeDtypeStruct((B, S, D), F32),
                   jax.ShapeDtypeStruct((B, S, HALF), I32)],
        scratch_shapes=[
            pltpu.VMEM((GLA_HEADS, HEAD_V, HEAD_K), F32),
            pltpu.VMEM((GLA_CHUNK + POOL_HALO, POOL_WIDTH), F32),
            pltpu.VMEM((GLA_CHUNK, 2 * GLA_DK), F32),
            pltpu.VMEM((GLA_CHUNK, GLA_DV), F32),
            pltpu.VMEM((GLA_CHUNK, GLA_DK), F32),
        ],
        compiler_params=pltpu.CompilerParams(
            dimension_semantics=("arbitrary", "arbitrary"), vmem_limit_bytes=VMEM_LIMIT),
    )(x, *operands)


def _rank_desc(vals, n):
    idx = lax.broadcasted_iota(I32, vals.shape, 0)
    rank = jnp.zeros(vals.shape, I32)
    for other in range(n):
        o = vals[other:other + 1, :]
        before = (o > vals) | ((o == vals) & (other < idx))
        rank = rank + before.astype(I32)
    return rank


def _router_kernel(h_ref, wrt_ref, bias_ref, eid_ref, pos_ref, wk_ref, cnt_ref, carry_ref):
    i = pl.program_id(0)

    @pl.when(i == 0)
    def _():
        carry_ref[...] = jnp.zeros_like(carry_ref)

    h = h_ref[...]
    tr = h.shape[0]
    w_hi, w_mid, _ = _split3(wrt_ref[...])
    h_hi, h_mid, _ = _split3(h)
    by_hi = lax.dot_general(jnp.concatenate([w_hi, w_mid], axis=0), h_hi, _NT,
                            preferred_element_type=F32)
    logits = (by_hi[:N_EXPERTS] + by_hi[N_EXPERTS:]
              + lax.dot_general(w_hi, h_mid, _NT, preferred_element_type=F32))
    scores = jax.nn.sigmoid(logits)
    biased = scores + bias_ref[...]
    grp = biased.reshape(N_GROUPS, EXPERTS_PER_GROUP, tr)
    slot = lax.broadcasted_iota(I32, grp.shape, 1)
    top1 = jnp.max(grp, axis=1, keepdims=True)
    first = jnp.min(jnp.where(grp == top1, slot, EXPERTS_PER_GROUP), axis=1, keepdims=True)
    top2 = jnp.max(jnp.where(slot == first, -jnp.inf, grp), axis=1)
    grp_score = top1[:, 0, :] + top2
    grp_on = _rank_desc(grp_score, N_GROUPS) < TOP_GROUPS
    emask = jnp.broadcast_to(grp_on[:, None, :], grp.shape).reshape(N_EXPERTS, tr)
    masked = jnp.where(emask, biased, -jnp.inf)

    eidx = lax.broadcasted_iota(I32, (N_EXPERTS, tr), 0).astype(F32)
    rest = masked
    eids, sels = [], []
    for _ in range(TOP_K):
        top = jnp.max(rest, axis=0, keepdims=True)
        eid = jnp.min(jnp.where(rest == top, eidx, float(N_EXPERTS)), axis=0, keepdims=True)
        hit = eidx == eid
        rest = jnp.where(hit, -jnp.inf, rest)
        eids.append(eid)
        sels.append(jnp.sum(jnp.where(hit, scores, 0.0), axis=0, keepdims=True))
    eid_k = jnp.concatenate(eids, axis=0)
    sel_k = jnp.concatenate(sels, axis=0)
    gate_k = sel_k / jnp.sum(sel_k, axis=0, keepdims=True) * ROUTE_SCALE
    chosen = rest != masked

    t_row = lax.broadcasted_iota(I32, (tr, tr), 0)
    t_col = lax.broadcasted_iota(I32, (tr, tr), 1)
    earlier = (t_row < t_col).astype(BF16)
    pos = carry_ref[...] + _dot(chosen.astype(BF16), earlier)
    carry_ref[...] += jnp.sum(chosen.astype(F32), axis=1, keepdims=True)
    cnt_ref[...] = carry_ref[...]

    poss = [jnp.sum(jnp.where(eidx == eid, pos, 0.0), axis=0, keepdims=True) for eid in eids]
    eid_ref[...] = eid_k.astype(I32)
    pos_ref[...] = jnp.concatenate(poss, axis=0).astype(I32)
    wk_ref[...] = gate_k


def _router(h2, w_router, router_bias):
    T, D = h2.shape
    tr = ROUTER_TILE
    assert T % tr == 0
    return pl.pallas_call(
        _router_kernel,
        name="router",
        grid=(T // tr,),
        in_specs=[pl.BlockSpec((tr, D), lambda i: (i, 0)),
                  _const_spec((N_EXPERTS, D)), _const_spec((N_EXPERTS, 1))],
        out_specs=[pl.BlockSpec((TOP_K, tr), lambda i: (0, i)),
                   pl.BlockSpec((TOP_K, tr), lambda i: (0, i)),
                   pl.BlockSpec((TOP_K, tr), lambda i: (0, i)),
                   pl.BlockSpec((N_EXPERTS, 1), lambda i: (0, 0))],
        out_shape=[jax.ShapeDtypeStruct((TOP_K, T), I32),
                   jax.ShapeDtypeStruct((TOP_K, T), I32),
                   jax.ShapeDtypeStruct((TOP_K, T), F32),
                   jax.ShapeDtypeStruct((N_EXPERTS, 1), F32)],
        scratch_shapes=[pltpu.VMEM((N_EXPERTS, 1), F32)],
        compiler_params=pltpu.CompilerParams(
            dimension_semantics=("arbitrary",), vmem_limit_bytes=VMEM_LIMIT),
    )(h2, w_router.T, router_bias.reshape(N_EXPERTS, 1))


def _slot_kernel(offs_ref, eid_ref, pos_ref, slot_ref):
    eid = eid_ref[...]
    base = jnp.zeros(eid.shape, I32)
    for e in range(N_EXPERTS):
        base = jnp.where(eid == e, offs_ref[e], base)
    slot_ref[...] = pos_ref[...] + base


def _slots(offsets, eid, pos):
    shape = eid.shape
    return pl.pallas_call(
        _slot_kernel,
        name="slots",
        grid_spec=pltpu.PrefetchScalarGridSpec(
            num_scalar_prefetch=1, grid=(1,),
            in_specs=[pl.BlockSpec(shape, lambda i, offs: (0, 0)),
                      pl.BlockSpec(shape, lambda i, offs: (0, 0))],
            out_specs=pl.BlockSpec(shape, lambda i, offs: (0, 0))),
        out_shape=jax.ShapeDtypeStruct(shape, I32),
        compiler_params=pltpu.CompilerParams(vmem_limit_bytes=VMEM_LIMIT),
    )(offsets, eid, pos)


def _sc_mesh():
    return plsc.VectorSubcoreMesh(core_axis_name="core", subcore_axis_name="subcore",
                                  num_cores=SC_CORES, num_subcores=SC_SUBCORES)


def _sc_worker():
    return lax.axis_index("subcore") * SC_CORES + lax.axis_index("core")


def _sc_dispatch(words, slots, n_rows):
    T = words.shape[0]
    n_chunks = slots.shape[0]
    workers = SC_CORES * SC_SUBCORES
    assert T == n_chunks * SC_CHUNK and n_chunks % workers == 0
    per_worker = n_chunks // workers

    @functools.partial(
        pl.kernel, mesh=_sc_mesh(),
        out_type=jax.ShapeDtypeStruct((n_rows, HALF), I32),
        scratch_types=[pltpu.VMEM((SC_CHUNK, HALF), I32), pltpu.VMEM((TOP_K, SC_CHUNK), I32),
                       pltpu.SemaphoreType.DMA])
    def dispatch(words_hbm, slots_hbm, out_hbm, rows_v, idx_v, sem):
        first = _sc_worker() * per_worker

        @pl.loop(0, per_worker)
        def _(i):
            c = first + i
            pltpu.sync_copy(slots_hbm.at[c], idx_v)
            pltpu.sync_copy(words_hbm.at[pl.ds(c * SC_CHUNK, SC_CHUNK)], rows_v)
            copies = [pltpu.async_copy(rows_v, out_hbm.at[idx_v.at[k]], sem) for k in range(TOP_K)]
            for cp in copies:
                cp.wait()

    return dispatch(words, slots)


def _sc_combine(rows, slots, gates, T):
    CH = SC_SUM_CHUNK
    steps = SC_CHUNK // CH
    n_blocks = slots.shape[0]
    workers = SC_CORES * SC_SUBCORES
    assert T == n_blocks * SC_CHUNK and n_blocks % workers == 0 and steps >= 2
    per_worker = n_blocks // workers
    vecs = HALF // SC_LANES

    @functools.partial(
        pl.kernel, mesh=_sc_mesh(),
        compiler_params=pltpu.CompilerParams(needs_layout_passes=False),
        out_type=jax.ShapeDtypeStruct((T, HALF), I32),
        scratch_types=[pltpu.VMEM((2, TOP_K, CH, HALF), I32), pltpu.VMEM((2, CH, HALF), I32),
                       pltpu.VMEM((per_worker, TOP_K, SC_CHUNK), I32),
                       pltpu.VMEM((per_worker, TOP_K, SC_CHUNK), F32),
                       pltpu.SemaphoreType.DMA((2,)), pltpu.SemaphoreType.DMA((2,))])
    def combine(rows_hbm, slots_hbm, gates_hbm, out_hbm, rows_v, sum_v, idx_v, gates_v, sem_in, sem_out):
        first = _sc_worker() * per_worker
        pltpu.sync_copy(slots_hbm.at[pl.ds(first, per_worker)], idx_v)
        pltpu.sync_copy(gates_hbm.at[pl.ds(first, per_worker)], gates_v)

        def gathers(i, j, b):
            return [pltpu.make_async_copy(rows_hbm.at[idx_v.at[i, k, pl.ds(j * CH, CH)]], rows_v.at[b, k],
                                          sem_in.at[b]) for k in range(TOP_K)]

        def write_out(token0, b):
            return pltpu.make_async_copy(sum_v.at[b], out_hbm.at[pl.ds(token0, CH)], sem_out.at[b])

        def add_rows(i, j, b):
            @pl.loop(0, CH)
            def _(t):
                token = jnp.full((SC_LANES,), j * CH + t, I32)
                gate = []
                for k in range(TOP_K):
                    g = plsc.load_gather(gates_v.at[i, k], [token])
                    gate.append(plsc.pack(g, g, format=plsc.PackFormat.INTERLEAVED))

                @plsc.parallel_loop(0, vecs, 1, unroll=2)
                def _(n):
                    lanes = pl.ds(n * SC_LANES, SC_LANES)
                    terms = [plsc.bitcast(rows_v[b, k, t, lanes], BF16) * gate[k] for k in range(TOP_K)]
                    while len(terms) > 1:
                        terms = [x + y for x, y in zip(terms[::2], terms[1::2])]
                    sum_v[b, t, lanes] = plsc.bitcast(terms[0], I32)

        @pl.loop(0, per_worker)
        def _(i):
            token0 = (first + i) * SC_CHUNK
            for cp in gathers(i, 0, 0):
                cp.start()
            for j in range(steps):
                b = j % 2
                if j + 1 < steps:
                    for cp in gathers(i, j + 1, 1 - b):
                        cp.start()
                for cp in gathers(i, j, b):
                    cp.wait()
                if j >= 2:
                    write_out(token0 + (j - 2) * CH, b).wait()
                add_rows(i, j, b)
                write_out(token0 + j * CH, b).start()
            write_out(token0 + (steps - 2) * CH, steps % 2).wait()
            write_out(token0 + (steps - 1) * CH, (steps - 1) % 2).wait()

    return combine(rows, slots, gates)


def _expert_kernel(tile_expert_ref, tile_rows_ref, tile_block_ref, next_expert_ref, xs_ref,
                   wg_hbm, wu_hbm, wd_hbm, ys_ref, wgu_ref, wdn_ref, stage_g, stage_u, stage_d, sem):
    del tile_block_ref
    i = pl.program_id(0)
    e = tile_expert_ref[i]
    n_valid = tile_rows_ref[i]
    e_prev = tile_expert_ref[jnp.maximum(i - 1, 0)]

    def weight_copies(expert):
        return [pltpu.make_async_copy(wg_hbm.at[expert], stage_g, sem.at[0]),
                pltpu.make_async_copy(wu_hbm.at[expert], stage_u, sem.at[1]),
                pltpu.make_async_copy(wd_hbm.at[expert], stage_d, sem.at[2])]

    @pl.when(i == 0)
    def _():
        for cp in weight_copies(e):
            cp.start()

    @pl.when((i == 0) | (e != e_prev))
    def _():
        for cp in weight_copies(e):
            cp.wait()
        wgu_ref[:, :EXPERT_FF] = stage_g[...].astype(BF16)
        wgu_ref[:, EXPERT_FF:] = stage_u[...].astype(BF16)
        wdn_ref[...] = stage_d[...].astype(BF16)
        nxt = next_expert_ref[i]

        @pl.when(nxt >= 0)
        def _():
            for cp in weight_copies(nxt):
                cp.start()

    @pl.when(n_valid > 0)
    def _():
        words = xs_ref[...]
        live = lax.broadcasted_iota(I32, (words.shape[0], 1), 0) < n_valid
        lo, hi = _unpack_rows(jnp.where(live, words, 0))
        a = _dot(lo.astype(BF16), wgu_ref[:HALF, :]) + _dot(hi.astype(BF16), wgu_ref[HALF:, :])
        act = a[:, :EXPERT_FF]
        hid = act * jax.nn.sigmoid(act) * a[:, EXPERT_FF:]
        ys_ref[...] = _pack_rows(_dot(hid.astype(BF16), wdn_ref[...]))


def _experts(xs, tile_expert, tile_rows, tile_block, next_expert, w_exp_gate, w_exp_up, w_exp_down):
    n_rows = xs.shape[0]
    R = EXPERT_ROWS
    D = D_MODEL
    assert n_rows % R == 0
    in_hbm = pl.BlockSpec(memory_space=pl.ANY)
    return pl.pallas_call(
        _expert_kernel,
        name="experts",
        grid_spec=pltpu.PrefetchScalarGridSpec(
            num_scalar_prefetch=4, grid=(n_rows // R,),
            in_specs=[pl.BlockSpec((R, HALF), lambda i, te, tr, tb, nx: (tb[i], 0)),
                      in_hbm, in_hbm, in_hbm],
            out_specs=pl.BlockSpec((R, HALF), lambda i, te, tr, tb, nx: (tb[i], 0)),
            scratch_shapes=[pltpu.VMEM((D, 2 * EXPERT_FF), BF16), pltpu.VMEM((EXPERT_FF, D), BF16),
                            pltpu.VMEM((D, EXPERT_FF), F32), pltpu.VMEM((D, EXPERT_FF), F32),
                            pltpu.VMEM((EXPERT_FF, D), F32), pltpu.SemaphoreType.DMA((3,))]),
        out_shape=jax.ShapeDtypeStruct((n_rows, HALF), I32),
        compiler_params=pltpu.CompilerParams(
            dimension_semantics=("arbitrary",), vmem_limit_bytes=VMEM_LIMIT),
    )(tile_expert, tile_rows, tile_block, next_expert, xs, w_exp_gate, w_exp_up, w_exp_down)


def _final_kernel(h_ref, routed_ref, wsgu_ref, wsd_ref, lng_ref, lnb_ref, out_ref, *, alpha):
    h = h_ref[...]
    a = _dot(h.astype(BF16), wsgu_ref[...])
    act = a[:, :SHARED_FF]
    hid = act * jax.nn.sigmoid(act) * a[:, SHARED_FF:]
    shared = _dot(hid.astype(BF16), wsd_ref[...])
    lo, hi = _unpack_rows(routed_ref[...])
    ffn = shared + jnp.concatenate([lo, hi], axis=1)
    out_ref[...] = _layer_norm(alpha * h + ffn, lng_ref[...], lnb_ref[...])


def _final(h2, routed, w_sgu, w_sd, ln_g, ln_b, alpha):
    T, D = h2.shape
    tm = FINAL_TILE
    assert T % tm == 0
    return pl.pallas_call(
        functools.partial(_final_kernel, alpha=alpha),
        name="final",
        grid=(T // tm,),
        in_specs=[pl.BlockSpec((tm, D), lambda i: (i, 0)),
                  pl.BlockSpec((tm, HALF), lambda i: (i, 0)),
                  _const_spec((D, 2 * SHARED_FF)), _const_spec((SHARED_FF, D)),
                  _const_spec((1, D)), _const_spec((1, D))],
        out_specs=pl.BlockSpec((tm, D), lambda i: (i, 0)),
        out_shape=jax.ShapeDtypeStruct((T, D), F32),
        compiler_params=pltpu.CompilerParams(
            dimension_semantics=("arbitrary",), vmem_limit_bytes=VMEM_LIMIT),
    )(h2, routed, w_sgu, w_sd, ln_g.reshape(1, D), ln_b.reshape(1, D))


def _route_and_dispatch(h2, words, w_router, router_bias):
    T, D = h2.shape
    R = EXPERT_ROWS
    eid, pos, gates, counts = _router(h2, w_router, router_bias)
    counts = counts.reshape(N_EXPERTS).astype(I32)
    padded = (counts + (R - 1)) // R * R
    ends = jnp.cumsum(padded)
    offsets = ends - padded
    n_rows = T * TOP_K + N_EXPERTS * R
    tile_start = jnp.arange(n_rows // R, dtype=I32) * R
    tile_block = jnp.minimum(tile_start, jnp.maximum(ends[-1] - R, 0)) // R
    past = ((tile_block * R)[:, None] >= ends[None, :]).astype(I32)
    tile_expert = jnp.minimum(jnp.sum(past, axis=1), N_EXPERTS - 1)
    mine = (tile_expert[:, None] == jnp.arange(N_EXPERTS, dtype=I32)[None, :]).astype(I32)
    valid_end = jnp.sum(mine * (offsets + counts)[None, :], axis=1)
    tile_rows = jnp.clip(valid_end - tile_start, 0, R).astype(I32)
    slots = _slots(offsets.astype(I32), eid, pos)

    def chunked(a):
        return a.reshape(TOP_K, T // SC_CHUNK, SC_CHUNK).transpose(1, 0, 2)

    slots = chunked(slots)
    xs = _sc_dispatch(words, slots, n_rows)
    next_start = jnp.sum(mine * ends[None, :], axis=1)
    next_expert = jnp.minimum(jnp.sum((next_start[:, None] >= ends[None, :]).astype(I32), axis=1),
                              N_EXPERTS - 1)
    next_expert = jnp.where(next_start < ends[-1], next_expert, -1).astype(I32)
    return xs, slots, chunked(gates), (tile_expert, tile_rows, tile_block.astype(I32), next_expert)


def kernel(x, w_in, w_gate_up, b_gate, gla_norm_g, w_gla_up, w_pool_grp, pool_scale, w_pool_up,
           w_out, ln1_g, ln1_b, w_router, router_bias, w_exp_gate, w_exp_up, w_exp_down,
           w_sh_gate, w_sh_up, w_sh_down, ln2_g, ln2_b):
    B, S, D = x.shape
    T = B * S
    depth = w_in.shape[0]
    alpha = (2.0 * depth) ** 0.25
    h = x
    for l in range(depth):
        mix_ops = _mixer_operands(w_in[l], w_gate_up[l], b_gate[l], gla_norm_g[l], w_gla_up[l],
                                  w_pool_grp[l], pool_scale[l], w_pool_up[l], w_out[l],
                                  ln1_g[l], ln1_b[l])
        w_sgu = jnp.concatenate([w_sh_gate[l], w_sh_up[l]], axis=-1).astype(BF16)
        w_sd = w_sh_down[l].astype(BF16)
        h, words = _mixer(h, mix_ops, alpha)
        h = h.reshape(T, D)
        xs, slots, gates, tiles = _route_and_dispatch(h, words.reshape(T, HALF),
                                                      w_router[l], router_bias[l])
        ys = _experts(xs, *tiles, w_exp_gate[l], w_exp_up[l], w_exp_down[l])
        routed = _sc_combine(ys, slots, gates, T)
        h = _final(h, routed, w_sgu, w_sd, ln2_g[l], ln2_b[l], alpha).reshape(B, S, D)
    return h
```
